```python
import functools
import jax, jax.numpy as jnp
from jax import lax
import numpy as np

D_MODEL = 1024
BATCH = 8
SEQ = 2048
DEPTH = 4
DEC_BATCH = 128
DEC_SEQ = 4
PAST_LEN = 2048
PAGE_SIZE = 128

HEAD_DIM = 64
ATT_HEADS = 8
KV_HEADS = 4
GQA_GROUP = ATT_HEADS // KV_HEADS
ATT_WIDTH = ATT_HEADS * HEAD_DIM
KV_WIDTH = KV_HEADS * HEAD_DIM
MOBA_BLOCK = 256
MOBA_TOPK = 3
Q_BLOCK = 64
ROPE_THETA = 10000.0
HG_HEADS = 4
HG_DK = 64
HG_DV = 64
HG_WIDTH = HG_HEADS * HG_DV
HG_CHUNK = 64
CONV_CH = 256
CONV_W = 3
MIX_WIDTH = ATT_WIDTH + HG_WIDTH + CONV_CH
N_MEM = 256
XA_HEADS = 4
XA_HEAD_DIM = 64
XA_WIDTH = XA_HEADS * XA_HEAD_DIM
D_FF = 2816
FFN_CONV_W = 3
EPS = 1e-6
IN_SPLITS = (ATT_WIDTH, KV_WIDTH, KV_WIDTH, HG_HEADS * HG_DK, HG_HEADS * HG_DK,
             HG_HEADS * HG_DV, HG_HEADS * HG_DV, CONV_CH, CONV_CH, CONV_CH)
IN_WIDTH = ATT_WIDTH + 2 * KV_WIDTH + 2 * HG_HEADS * HG_DK + 2 * HG_HEADS * HG_DV + 3 * CONV_CH

kernel_name = 'moba_hgrn2_shortconv_hybrid_step'


def rmsnorm(x, g):
    xf = x.astype(jnp.float32)
    y = xf * lax.rsqrt(jnp.mean(xf * xf, axis=-1, keepdims=True) + EPS)
    return (y * g.astype(jnp.float32)).astype(x.dtype)


def rope(x, pos):
    half = HEAD_DIM // 2
    inv = ROPE_THETA ** (-jnp.arange(half, dtype=jnp.float32) / half)
    ang = pos.astype(jnp.float32)[:, None] * inv[None, :]
    cos = jnp.cos(ang)[None, :, None, :]
    sin = jnp.sin(ang)[None, :, None, :]
    xf = x.astype(jnp.float32)
    x1, x2 = xf[..., :half], xf[..., half:]
    return jnp.concatenate([x1 * cos - x2 * sin, x2 * cos + x1 * sin], axis=-1).astype(x.dtype)


def causal_dwconv(u, prev, w):
    T = u.shape[1]
    W = w.shape[0]
    full = jnp.concatenate([prev.astype(u.dtype), u], axis=1)
    y = full[:, 0:T] * w[0]
    for j in range(1, W):
        y = y + full[:, j:j + T] * w[j]
    return y, full[:, full.shape[1] - (W - 1):]


def block_mean(kb):
    return jnp.mean(kb.astype(jnp.float32), axis=3).astype(kb.dtype)


def moba_attend(qg, kb, vb, kmean, n_past, own_k, own_v, own_mask, k_eff):
    scale = HEAD_DIM ** -0.5
    parts = []
    if k_eff > 0:
        n_cand = kmean.shape[2]
        gate = jnp.einsum('bkgqd,bknd->bkgqn', qg, kmean).astype(jnp.float32)
        gate = jnp.where(jnp.arange(n_cand) < n_past, gate, -jnp.inf)
        _, sel = lax.top_k(gate, k_eff)
        valid = jnp.arange(k_eff) < n_past
        bi = jnp.arange(kb.shape[0])[:, None, None, None]
        hi = jnp.arange(KV_HEADS)[None, :, None, None]
        for r in range(k_eff):
            k_sel = kb[bi, hi, sel[..., r]]
            s = jnp.einsum('bkgqd,bkgqmd->bkgqm', qg, k_sel).astype(jnp.float32) * scale
            parts.append(jnp.where(valid[r], s, -jnp.inf))
    s_own = jnp.einsum('bkgqd,bkmd->bkgqm', qg, own_k).astype(jnp.float32) * scale
    parts.append(jnp.where(own_mask, s_own, -jnp.inf))
    p = jax.nn.softmax(jnp.concatenate(parts, axis=-1), axis=-1).astype(own_v.dtype)
    out = jnp.einsum('bkgqm,bkmd->bkgqd', p[..., k_eff * MOBA_BLOCK:], own_v)
    for r in range(k_eff):
        v_sel = vb[bi, hi, sel[..., r]]
        out = out + jnp.einsum('bkgqm,bkgqmd->bkgqd', p[..., r * MOBA_BLOCK:(r + 1) * MOBA_BLOCK], v_sel)
    return out


def moba_prompt(q, k, v):
    B, T = q.shape[0], q.shape[1]
    nb = -(-T // MOBA_BLOCK)
    pad = nb * MOBA_BLOCK - T
    kp = jnp.pad(k, ((0, 0), (0, pad), (0, 0), (0, 0)))
    vp = jnp.pad(v, ((0, 0), (0, pad), (0, 0), (0, 0)))
    kb = kp.reshape(B, nb, MOBA_BLOCK, KV_HEADS, HEAD_DIM).transpose(0, 3, 1, 2, 4)
    vb = vp.reshape(B, nb, MOBA_BLOCK, KV_HEADS, HEAD_DIM).transpose(0, 3, 1, 2, 4)
    kmean = block_mean(kb)
    k_eff = min(MOBA_TOPK, nb - 1)
    nq = T // Q_BLOCK
    qg = q.reshape(B, nq, Q_BLOCK, KV_HEADS, GQA_GROUP, HEAD_DIM).transpose(1, 0, 3, 4, 2, 5)

    def one(args):
        qi, qc = args
        q0 = qi * Q_BLOCK
        b = q0 // MOBA_BLOCK
        own_k = lax.dynamic_index_in_dim(kb, b, axis=2, keepdims=False)
        own_v = lax.dynamic_index_in_dim(vb, b, axis=2, keepdims=False)
        kpos = b * MOBA_BLOCK + jnp.arange(MOBA_BLOCK)
        qpos = q0 + jnp.arange(Q_BLOCK)
        mask = kpos[None, :] <= qpos[:, None]
        return moba_attend(qc, kb, vb, kmean, b, own_k, own_v, mask, k_eff)

    out = lax.map(one, (jnp.arange(nq, dtype=jnp.int32), qg))
    return out.transpose(1, 0, 4, 2, 3, 5).reshape(B, T, ATT_WIDTH)


def moba_sample(q, k, v, k_past, v_past):
    DB, S = q.shape[0], q.shape[1]
    P = k_past.shape[1]
    nb = P // MOBA_BLOCK
    start = nb * MOBA_BLOCK
    qg = q.reshape(DB, S, KV_HEADS, GQA_GROUP, HEAD_DIM).transpose(0, 2, 3, 1, 4)
    k_eff = min(MOBA_TOPK, nb)
    kb = vb = kmean = None
    if k_eff > 0:
        kb = k_past[:, :start].reshape(DB, nb, MOBA_BLOCK, KV_HEADS, HEAD_DIM).transpose(0, 3, 1, 2, 4)
        vb = v_past[:, :start].reshape(DB, nb, MOBA_BLOCK, KV_HEADS, HEAD_DIM).transpose(0, 3, 1, 2, 4)
        kmean = block_mean(kb)
    own_k = jnp.concatenate([k_past[:, start:].astype(k.dtype), k], axis=1).transpose(0, 2, 1, 3)
    own_v = jnp.concatenate([v_past[:, start:].astype(v.dtype), v], axis=1).transpose(0, 2, 1, 3)
    kpos = start + jnp.arange(own_k.shape[2])
    qpos = P + jnp.arange(S)
    mask = kpos[None, :] <= qpos[:, None]
    out = moba_attend(qg, kb, vb, kmean, nb, own_k, own_v, mask, k_eff)
    return out.transpose(0, 3, 1, 2, 4).reshape(DB, S, ATT_WIDTH)


def hgrn2_recurrence(q, k, v, logf, s0):
    B, T = q.shape[0], q.shape[1]
    L = min(HG_CHUNK, T)
    n = -(-T // L)
    pad = n * L - T

    def chunks(a):
        a = jnp.pad(a.astype(jnp.float32), ((0, 0), (0, pad), (0, 0), (0, 0)))
        return a.reshape(B, n, L, a.shape[2], a.shape[3]).transpose(1, 0, 3, 2, 4)

    tri = jnp.tril(jnp.ones((L, L), dtype=bool))

    def step(S, inp):
        qc, kc, vc, lc = inp
        A = jnp.cumsum(lc, axis=2)
        o_inter = jnp.einsum('bhld,bhdv->bhlv', qc * jnp.exp(A), S)
        diff = A[:, :, :, None, :] - A[:, :, None, :, :]
        decay = jnp.exp(jnp.where(tri[:, :, None], diff, -jnp.inf))
        scores = jnp.einsum('bhtd,bhsd,bhtsd->bhts', qc, kc, decay)
        o_intra = jnp.einsum('bhts,bhsv->bhtv', scores, vc)
        A_end = A[:, :, -1:, :]
        S_new = jnp.exp(A_end[:, :, 0, :, None]) * S + jnp.einsum('bhsd,bhsv->bhdv', kc * jnp.exp(A_end - A), vc)
        return S_new, o_inter + o_intra

    S_fin, o = lax.scan(step, s0.astype(jnp.float32), (chunks(q), chunks(k), chunks(v), chunks(logf)))
    o = o.transpose(1, 0, 3, 2, 4).reshape(B, n * L, HG_HEADS, HG_DV)[:, :T]
    return o, S_fin


def hgrn2_mixer(hq, hf, hi, hg, lb, norm_g, s0):
    B, T = hq.shape[0], hq.shape[1]
    q = jax.nn.silu(hq).reshape(B, T, HG_HEADS, HG_DK)
    z = hf.astype(jnp.float32).reshape(B, T, HG_HEADS, HG_DK)
    lbh = lb.reshape(HG_HEADS, HG_DK)
    logf = jnp.log(lbh + (1.0 - lbh) * jax.nn.sigmoid(z))
    k = (1.0 - lbh) * jax.nn.sigmoid(-z)
    v = hi.reshape(B, T, HG_HEADS, HG_DV)
    o, S = hgrn2_recurrence(q, k, v, logf, s0)
    o = rmsnorm(o, norm_g.reshape(HG_HEADS, HG_DV))
    o = o.reshape(B, T, HG_WIDTH) * jax.nn.silu(hg.astype(jnp.float32))
    return o.astype(hq.dtype), S


def mixer_layer(x, pos, attn, hg_state, conv_prev, g, w_in_l, w_out_l, lb, norm_g, cw):
    B, T = x.shape[0], x.shape[1]
    h = rmsnorm(x, g) @ w_in_l
    idx = [int(i) for i in np.cumsum(IN_SPLITS)[:-1]]
    q, k, v, hq, hf, hi, hg, cb, cc, ch = jnp.split(h, idx, axis=-1)
    q = rope(q.reshape(B, T, ATT_HEADS, HEAD_DIM), pos)
    k = rope(k.reshape(B, T, KV_HEADS, HEAD_DIM), pos)
    v = v.reshape(B, T, KV_HEADS, HEAD_DIM)
    a = attn(q, k, v)
    o_h, S = hgrn2_mixer(hq, hf, hi, hg, lb, norm_g, hg_state)
    yc, conv_new = causal_dwconv(cc * ch, conv_prev, cw)
    o_c = cb * yc
    y = jnp.concatenate([a, o_h, o_c.astype(a.dtype)], axis=-1) @ w_out_l
    return x + y, k, v, S, conv_new


def memory_kv(mem, g, wk, wv):
    B = mem.shape[0]
    m = rmsnorm(mem, g)
    return ((m @ wk).reshape(B, N_MEM, XA_HEADS, XA_HEAD_DIM),
            (m @ wv).reshape(B, N_MEM, XA_HEADS, XA_HEAD_DIM))


def cross_attn(x, g, wq, wo, mk, mv):
    B, T = x.shape[0], x.shape[1]
    q = (rmsnorm(x, g) @ wq).reshape(B, T, XA_HEADS, XA_HEAD_DIM)
    s = jnp.einsum('bthd,bmhd->bhtm', q, mk.astype(q.dtype)).astype(jnp.float32) * XA_HEAD_DIM ** -0.5
    p = jax.nn.softmax(s, axis=-1).astype(q.dtype)
    o = jnp.einsum('bhtm,bmhd->bthd', p, mv.astype(q.dtype)).reshape(B, T, XA_WIDTH)
    return o @ wo


def conv_ffn(x, g, w_up_l, cw, w_down_l, prev):
    u = rmsnorm(x, g) @ w_up_l
    u, new = causal_dwconv(u, prev, cw)
    a, b = jnp.split(u, 2, axis=-1)
    return (jax.nn.silu(a) * b) @ w_down_l, new


def setup_inputs(seed: int = 0) -> dict:
    key = jax.random.key(seed)
    keys = jax.random.split(key, 32)
    c = [0]

    def nxt():
        kk = keys[c[0]]
        c[0] += 1
        return kk

    def nrm(shape, scale):
        return jax.random.normal(nxt(), shape, jnp.float32) * scale

    def gain(shape):
        return 1.0 + nrm(shape, 0.02)

    n_pages = PAST_LEN // PAGE_SIZE
    n_used = DEC_BATCH * n_pages
    n_pool = (5 * n_used + 3) // 4
    x_prompt = nrm((BATCH, SEQ, D_MODEL), 1.0)
    x_sample = nrm((DEC_BATCH, DEC_SEQ, D_MODEL), 1.0)
    cache_k = nrm((DEPTH, n_pool, PAGE_SIZE, KV_HEADS, HEAD_DIM), 1.0)
    cache_v = nrm((DEPTH, n_pool, PAGE_SIZE, KV_HEADS, HEAD_DIM), 1.0)
    cache_mem_k = nrm((DEPTH, DEC_BATCH, N_MEM, XA_HEADS, XA_HEAD_DIM), 1.0)
    cache_mem_v = nrm((DEPTH, DEC_BATCH, N_MEM, XA_HEADS, XA_HEAD_DIM), 1.0)
    state_hgrn = nrm((DEPTH, DEC_BATCH, HG_HEADS, HG_DK, HG_DV), 0.3)
    state_conv = nrm((DEPTH, DEC_BATCH, CONV_W - 1, CONV_CH), 1.0)
    state_ffn = nrm((DEPTH, DEC_BATCH, FFN_CONV_W - 1, 2 * D_FF), 1.0)
    page_table = jax.random.permutation(nxt(), n_pool)[:n_used].reshape(DEC_BATCH, n_pages).astype(jnp.int32)
    mem_prompt = nrm((BATCH, N_MEM, D_MODEL), 1.0)
    return {
        'x_prompt': x_prompt, 'x_sample': x_sample,
        'cache_k': cache_k, 'cache_v': cache_v,
        'cache_mem_k': cache_mem_k, 'cache_mem_v': cache_mem_v,
        'state_hgrn': state_hgrn, 'state_conv': state_conv, 'state_ffn': state_ffn,
        'page_table': page_table, 'mem_prompt': mem_prompt,
        'g_mix': gain((DEPTH, D_MODEL)),
        'w_in': nrm((DEPTH, D_MODEL, IN_WIDTH), D_MODEL ** -0.5),
        'w_out': nrm((DEPTH, MIX_WIDTH, D_MODEL), MIX_WIDTH ** -0.5),
        'hg_lb_logits': nrm((DEPTH, HG_HEADS * HG_DK), 0.5),
        'hg_norm_g': gain((DEPTH, HG_WIDTH)),
        'conv_w': nrm((DEPTH, CONV_W, CONV_CH), CONV_W ** -0.5),
        'g_xattn': gain((DEPTH, D_MODEL)),
        'g_mem': gain((DEPTH, D_MODEL)),
        'w_xq': nrm((DEPTH, D_MODEL, XA_WIDTH), D_MODEL ** -0.5),
        'w_mk': nrm((DEPTH, D_MODEL, XA_WIDTH), D_MODEL ** -0.5),
        'w_mv': nrm((DEPTH, D_MODEL, XA_WIDTH), D_MODEL ** -0.5),
        'w_xo': nrm((DEPTH, XA_WIDTH, D_MODEL), XA_WIDTH ** -0.5),
        'g_ffn': gain((DEPTH, D_MODEL)),
        'w_up': nrm((DEPTH, D_MODEL, 2 * D_FF), D_MODEL ** -0.5),
        'ffn_conv_w': nrm((DEPTH, FFN_CONV_W, 2 * D_FF), FFN_CONV_W ** -0.5),
        'w_down': nrm((DEPTH, D_FF, D_MODEL), D_FF ** -0.5),
        'g_final': gain((D_MODEL,)),
    }


def reference(x_prompt, x_sample, cache_k, cache_v, cache_mem_k, cache_mem_v, state_hgrn, state_conv,
              state_ffn, page_table, mem_prompt, g_mix, w_in, w_out, hg_lb_logits, hg_norm_g, conv_w,
              g_xattn, g_mem, w_xq, w_mk, w_mv, w_xo, g_ffn, w_up, ffn_conv_w, w_down, g_final):
    B, T = x_prompt.shape[0], x_prompt.shape[1]
    DB, S = x_sample.shape[0], x_sample.shape[1]
    past = page_table.shape[1] * cache_k.shape[2]
    pos_p = jnp.arange(T, dtype=jnp.int32)
    pos_s = past + jnp.arange(S, dtype=jnp.int32)
    lb_p = jax.nn.softmax(hg_lb_logits.astype(jnp.float32), axis=0)
    lower = jnp.cumsum(lb_p, axis=0) - lb_p[0:1]
    xp, xs = x_prompt, x_sample
    kp_l, vp_l, ks_l, vs_l, hp_l, hs_l, cp_l, cs_l, fp_l, fs_l, mk_l, mv_l = ([] for _ in range(12))
    for l in range(DEPTH):
        mix = (g_mix[l], w_in[l], w_out[l], lower[l], hg_norm_g[l], conv_w[l])
        xp, k1, v1, h1, c1 = mixer_layer(xp, pos_p, moba_prompt,
                                         jnp.zeros((B, HG_HEADS, HG_DK, HG_DV), jnp.float32),
                                         jnp.zeros((B, CONV_W - 1, CONV_CH), xp.dtype), *mix)
        mk, mv = memory_kv(mem_prompt, g_mem[l], w_mk[l], w_mv[l])
        xp = xp + cross_attn(xp, g_xattn[l], w_xq[l], w_xo[l], mk, mv)
        f1, fs1 = conv_ffn(xp, g_ffn[l], w_up[l], ffn_conv_w[l], w_down[l],
                           jnp.zeros((B, FFN_CONV_W - 1, 2 * D_FF), xp.dtype))
        xp = xp + f1
        k_past = cache_k[l][page_table].reshape(DB, past, KV_HEADS, HEAD_DIM)
        v_past = cache_v[l][page_table].reshape(DB, past, KV_HEADS, HEAD_DIM)
        attn_s = functools.partial(moba_sample, k_past=k_past, v_past=v_past)
        xs, k2, v2, h2, c2 = mixer_layer(xs, pos_s, attn_s, state_hgrn[l], state_conv[l], *mix)
        xs = xs + cross_attn(xs, g_xattn[l], w_xq[l], w_xo[l], cache_mem_k[l], cache_mem_v[l])
        f2, fs2 = conv_ffn(xs, g_ffn[l], w_up[l], ffn_conv_w[l], w_down[l], state_ffn[l])
        xs = xs + f2
        kp_l.append(k1); vp_l.append(v1); ks_l.append(k2); vs_l.append(v2)
        hp_l.append(h1); hs_l.append(h2); cp_l.append(c1); cs_l.append(c2)
        fp_l.append(fs1); fs_l.append(fs2); mk_l.append(mk); mv_l.append(mv)
    y_prompt = rmsnorm(xp, g_final)
    y_sample = rmsnorm(xs, g_final)
    return (y_prompt, y_sample,
            jnp.stack(kp_l), jnp.stack(vp_l), jnp.stack(ks_l), jnp.stack(vs_l),
            jnp.stack(hp_l), jnp.stack(hs_l), jnp.stack(cp_l), jnp.stack(cs_l),
            jnp.stack(fp_l), jnp.stack(fs_l), jnp.stack(mk_l), jnp.stack(mv_l))
```

```python
import functools

import numpy as np
import jax
import jax.numpy as jnp
from jax import lax
from jax.experimental import pallas as pl
from jax.experimental.pallas import tpu as pltpu

F32 = jnp.float32
BF16 = jnp.bfloat16
HIGHEST = lax.Precision.HIGHEST
NEG_INF = float("-inf")

D_MODEL = 1024
HEAD_DIM = 64
ATT_HEADS = 8
KV_HEADS = 4
ATT_WIDTH = ATT_HEADS * HEAD_DIM
KV_WIDTH = KV_HEADS * HEAD_DIM
MOBA_BLOCK = 256
MOBA_TOPK = 3
ROPE_THETA = 10000.0
HG_HEADS = 4
HG_DK = 64
HG_WIDTH = HG_HEADS * HG_DK
HG_CHUNK = 64
HG_LEAF = 8
CONV_CH = 256
N_MEM = 256
XA_HEADS = 4
XA_WIDTH = 256
D_FF = 2816
EPS = 1e-6
IN_WIDTH = 2816
COL = 256
LANES = 128
ROW_TILE = 512
HALO = 16
VMEM_LIMIT = 56 * 1024 * 1024

CB_K, CB_V, CB_HQ, CB_HF, CB_HI, CB_HG, CB_CB, CB_CC, CB_CH = 2, 3, 4, 5, 6, 7, 8, 9, 10


def _cparams(*sem):
    return pltpu.CompilerParams(dimension_semantics=sem, vmem_limit_bytes=VMEM_LIMIT)


def _rms(x, g):
    ms = jnp.mean(x * x, axis=-1, keepdims=True)
    return x * lax.rsqrt(ms + EPS) * g


def _sigmoid(z):
    return 1.0 / (1.0 + jnp.exp(-z))


def _silu(z):
    return z * _sigmoid(z)


def _dot(a, b):
    return jnp.dot(a, b, preferred_element_type=F32)


def _dot_nt(a, b, precision=None):
    return lax.dot_general(a, b, (((1,), (1,)), ((), ())), precision=precision,
                           preferred_element_type=F32)


def _dot_tn(a, b):
    return lax.dot_general(a, b, (((0,), (0,)), ((), ())), preferred_element_type=F32)


def _block_diag_mask(rows, cols, blk):
    r = lax.broadcasted_iota(jnp.int32, (rows, cols), 0) // blk
    c = lax.broadcasted_iota(jnp.int32, (rows, cols), 1) // blk
    return r == c


def _norm_proj_kernel(x_ref, g_ref, w_ref, cos_ref, sin_ref, o_ref, *, n_rope):
    xn = _rms(x_ref[...], g_ref[...]).astype(BF16)
    n_chunks = o_ref.shape[1] // COL
    if n_rope:
        cos = jnp.concatenate([cos_ref[...]] * (COL // LANES), axis=1)
        sin = jnp.concatenate([sin_ref[...]] * (COL // LANES), axis=1)
        lane = lax.broadcasted_iota(jnp.int32, cos.shape, 1)
        first_half = (lane % HEAD_DIM) < HEAD_DIM // 2
    for c in range(n_chunks):
        y = _dot(xn, w_ref[:, c * COL:(c + 1) * COL])
        if c < n_rope:
            partner = jnp.where(first_half, pltpu.roll(y, COL - HEAD_DIM // 2, 1),
                                pltpu.roll(y, HEAD_DIM // 2, 1))
            y = y * cos + partner * sin
        o_ref[:, c * COL:(c + 1) * COL] = y


def _norm_proj(x, g, w, rope=None, n_rope=0):
    rows, d = x.shape
    n = w.shape[1]
    tm = min(ROW_TILE, rows)
    if rope is None:
        cos = sin = jnp.zeros((tm, LANES), F32)
    else:
        cos, sin = rope
    tab_blocks = cos.shape[0] // tm
    return pl.pallas_call(
        functools.partial(_norm_proj_kernel, n_rope=n_rope),
        grid=(rows // tm,),
        in_specs=[
            pl.BlockSpec((tm, d), lambda i: (i, 0)),
            pl.BlockSpec((1, d), lambda i: (0, 0)),
            pl.BlockSpec((d, n), lambda i: (0, 0)),
            pl.BlockSpec((tm, LANES), lambda i: (i % tab_blocks, 0)),
            pl.BlockSpec((tm, LANES), lambda i: (i % tab_blocks, 0)),
        ],
        out_specs=pl.BlockSpec((tm, n), lambda i: (i, 0)),
        out_shape=jax.ShapeDtypeStruct((rows, n), F32),
        compiler_params=_cparams("parallel"),
        name="norm_proj",
    )(x, g.reshape(1, d), w, cos, sin)


def _rope_tables(pos):
    half = HEAD_DIM // 2
    inv = ROPE_THETA ** (-jnp.arange(half, dtype=F32) / half)
    ang = pos.astype(F32)[:, None] * inv[None, :]
    cos = jnp.cos(ang)
    sin = jnp.sin(ang)
    reps = LANES // HEAD_DIM
    cos_t = jnp.concatenate([cos, cos] * reps, axis=1)
    sin_t = jnp.concatenate([-sin, sin] * reps, axis=1)
    return cos_t, sin_t


def _resid_proj_kernel(*refs, n_in):
    x_ref = refs[0]
    a_refs = refs[1:1 + n_in]
    w_refs = refs[1 + n_in:1 + 2 * n_in]
    o_ref = refs[1 + 2 * n_in]
    acc = x_ref[...]
    for a_ref, w_ref in zip(a_refs, w_refs):
        acc = acc + _dot(a_ref[...].astype(BF16), w_ref[...])
    o_ref[...] = acc


def _resid_proj(x, acts, weights):
    rows, d = x.shape
    tm = min(ROW_TILE, rows)
    n_in = len(acts)
    in_specs = [pl.BlockSpec((tm, d), lambda i: (i, 0))]
    in_specs += [pl.BlockSpec((tm, a.shape[1]), lambda i: (i, 0)) for a in acts]
    in_specs += [pl.BlockSpec(w.shape, lambda i: (0, 0)) for w in weights]
    return pl.pallas_call(
        functools.partial(_resid_proj_kernel, n_in=n_in),
        grid=(rows // tm,),
        in_specs=in_specs,
        out_specs=pl.BlockSpec((tm, d), lambda i: (i, 0)),
        out_shape=jax.ShapeDtypeStruct((rows, d), F32),
        compiler_params=_cparams("parallel"),
        name="resid_proj",
    )(x, *acts, *weights)


def _topk_rank(gate, n_cand, n_valid, axis):
    idx = lax.broadcasted_iota(jnp.int32, gate.shape, axis)
    cnt = jnp.zeros(gate.shape, F32)
    for m in range(n_cand):
        gm = gate[m:m + 1, :] if axis == 0 else gate[:, m:m + 1]
        beats = jnp.where(gm > gate, 1.0, jnp.where(gm == gate, jnp.where(m < idx, 1.0, 0.0), 0.0))
        cnt = cnt + beats * jnp.where(m < n_valid, 1.0, 0.0)
    return cnt, idx


def _moba_prompt_kernel(q_ref, k_ref, v_ref, o_ref, kbf, vt, kmean, bias, m_sc, l_sc, acc):
    i = pl.program_id(2)
    n_blk = k_ref.shape[0] // MOBA_BLOCK
    nq = 4 * MOBA_BLOCK

    @pl.when(i == 0)
    def _():
        k = k_ref[...]
        kbf[...] = k.astype(BF16)
        vt[...] = v_ref[...].T.astype(BF16)
        kmean[...] = jnp.mean(k.reshape(n_blk, MOBA_BLOCK, LANES), axis=1)

    qb = q_ref[...]
    lane = lax.broadcasted_iota(jnp.int32, (MOBA_BLOCK, LANES), 1)
    lo = lane < HEAD_DIM
    t0, t1 = qb[:, :LANES], qb[:, LANES:]
    r0, r1 = pltpu.roll(t0, HEAD_DIM, 1), pltpu.roll(t1, HEAD_DIM, 1)
    zero = jnp.zeros_like(t0)
    qs = jnp.concatenate([jnp.where(lo, t0, zero), jnp.where(lo, r0, zero),
                          jnp.where(lo, zero, r1), jnp.where(lo, zero, t1)], axis=0)
    qst = qs.T

    gate = jnp.dot(kmean[...], qst, precision=HIGHEST, preferred_element_type=F32)
    cnt, bidx = _topk_rank(gate, n_blk, i, 0)
    bias[...] = jnp.where((bidx < i) & (cnt < MOBA_TOPK), 0.0, NEG_INF)

    qst_s = (qst * HEAD_DIM ** -0.5).astype(BF16)

    def scores(n):
        kn = kbf[pl.ds(pl.multiple_of(n * MOBA_BLOCK, MOBA_BLOCK), MOBA_BLOCK), :]
        return _dot(kn, qst_s)

    def values(n):
        return vt[:, pl.ds(pl.multiple_of(n * MOBA_BLOCK, MOBA_BLOCK), MOBA_BLOCK)]

    s = scores(i)
    kpos = lax.broadcasted_iota(jnp.int32, (MOBA_BLOCK, nq), 0)
    qpos = lax.broadcasted_iota(jnp.int32, (MOBA_BLOCK, nq), 1) % MOBA_BLOCK
    s = jnp.where(kpos <= qpos, s, NEG_INF)
    m0 = jnp.max(s, axis=0, keepdims=True)
    p = jnp.exp(s - m0)
    m_sc[...] = m0
    l_sc[...] = jnp.sum(p, axis=0, keepdims=True)
    acc[...] = _dot(values(i), p.astype(BF16))

    def body(n, carry):
        s = scores(n) + bias[pl.ds(n, 1), :]
        m_old = m_sc[...]
        m_new = jnp.maximum(m_old, jnp.max(s, axis=0, keepdims=True))
        alpha = jnp.exp(m_old - m_new)
        p = jnp.exp(s - m_new)
        m_sc[...] = m_new
        l_sc[...] = alpha * l_sc[...] + jnp.sum(p, axis=0, keepdims=True)
        acc[...] = alpha * acc[...] + _dot(values(n), p.astype(BF16))
        return carry

    lax.fori_loop(0, i, body, 0)

    o = (acc[...] / l_sc[...]).T
    b = MOBA_BLOCK
    tile0 = jnp.where(lo, o[0:b], pltpu.roll(o[b:2 * b], HEAD_DIM, 1))
    tile1 = jnp.where(lo, pltpu.roll(o[2 * b:3 * b], HEAD_DIM, 1), o[3 * b:4 * b])
    o_ref[...] = jnp.concatenate([tile0, tile1], axis=1)


def _moba_prompt(h, batch, seq):
    n_blk = seq // MOBA_BLOCK
    kcol = (CB_K * COL) // LANES
    vcol = (CB_V * COL) // LANES
    return pl.pallas_call(
        _moba_prompt_kernel,
        grid=(batch, KV_HEADS // 2, n_blk),
        in_specs=[
            pl.BlockSpec((MOBA_BLOCK, COL), lambda b, p, i: (b * n_blk + i, p)),
            pl.BlockSpec((seq, LANES), lambda b, p, i: (b, kcol + p)),
            pl.BlockSpec((seq, LANES), lambda b, p, i: (b, vcol + p)),
        ],
        out_specs=pl.BlockSpec((MOBA_BLOCK, COL), lambda b, p, i: (b * n_blk + i, p)),
        out_shape=jax.ShapeDtypeStruct((batch * seq, ATT_WIDTH), F32),
        scratch_shapes=[
            pltpu.VMEM((seq, LANES), BF16),
            pltpu.VMEM((LANES, seq), BF16),
            pltpu.VMEM((n_blk, LANES), F32),
            pltpu.VMEM((n_blk, 4 * MOBA_BLOCK), F32),
            pltpu.VMEM((1, 4 * MOBA_BLOCK), F32),
            pltpu.VMEM((1, 4 * MOBA_BLOCK), F32),
            pltpu.VMEM((LANES, 4 * MOBA_BLOCK), F32),
        ],
        compiler_params=_cparams("parallel", "parallel", "arbitrary"),
        name="moba_prompt",
    )(h, h, h)


def _moba_sample_kernel(pt_ref, qbd_ref, kn_ref, vn_ref, *rest, n_pages, page):
    del pt_ref
    kp = rest[:n_pages]
    vp = rest[n_pages:2 * n_pages]
    o_ref = rest[2 * n_pages]
    kc, vc, kmean = rest[2 * n_pages + 1:]
    n_blk = n_pages * page // MOBA_BLOCK
    per_blk = MOBA_BLOCK // page
    n_rows = qbd_ref.shape[1]
    seq_new = kn_ref.shape[1]

    kmean[...] = jnp.zeros(kmean.shape, F32)
    for n in range(n_blk):
        tot = jnp.zeros((1, KV_WIDTH), F32)
        for j in range(per_blk):
            pg = n * per_blk + j
            kpage = kp[pg][...]
            kc[pg * page:(pg + 1) * page, :] = kpage.astype(BF16)
            vc[pg * page:(pg + 1) * page, :] = vp[pg][...].astype(BF16)
            tot = tot + jnp.sum(kpage, axis=0, keepdims=True)
        kmean[n:n + 1, :] = tot * (1.0 / MOBA_BLOCK)

    qbd = qbd_ref[0]
    gate = _dot_nt(qbd, kmean[...], precision=HIGHEST)
    cnt, bidx = _topk_rank(gate, n_blk, n_blk, 1)
    bias = jnp.where((bidx < n_blk) & (cnt < MOBA_TOPK), 0.0, NEG_INF)

    qs = qbd * HEAD_DIM ** -0.5
    s_all = _dot_nt(qs.astype(BF16), kc[...])
    s_blk = [s_all[:, n * MOBA_BLOCK:(n + 1) * MOBA_BLOCK] + bias[:, n:n + 1] for n in range(n_blk)]

    kn = kn_ref[0]
    vn = vn_ref[0]
    tok = lax.broadcasted_iota(jnp.int32, (n_rows, 1), 0) % seq_new
    s_own = []
    for t in range(seq_new):
        st = jnp.sum(qs * kn[t:t + 1, :], axis=-1, keepdims=True)
        s_own.append(jnp.where(t <= tok, st, NEG_INF))

    m = s_own[0]
    for st in s_own[1:]:
        m = jnp.maximum(m, st)
    for sb in s_blk:
        m = jnp.maximum(m, jnp.max(sb, axis=-1, keepdims=True))

    l = jnp.zeros((n_rows, 1), F32)
    out = jnp.zeros((n_rows, KV_WIDTH), F32)
    for t in range(seq_new):
        pt = jnp.exp(s_own[t] - m)
        l = l + pt
        out = out + pt * vn[t:t + 1, :]
    for n in range(n_blk):
        pn = jnp.exp(s_blk[n] - m)
        l = l + jnp.sum(pn, axis=-1, keepdims=True)
        out = out + _dot(pn.astype(BF16), vc[n * MOBA_BLOCK:(n + 1) * MOBA_BLOCK, :])
    o_ref[0] = out / l


def _moba_sample(qbd, k_new, v_new, cache_k, cache_v, page_table, layer):
    n_req, n_pages = page_table.shape
    page = cache_k.shape[2]
    n_rows = qbd.shape[1]
    seq_new = k_new.shape[1]

    def page_spec(p):
        return pl.BlockSpec((None, None, page, KV_WIDTH), lambda r, pt: (layer, pt[r * n_pages + p], 0, 0))

    in_specs = [
        pl.BlockSpec((1, n_rows, KV_WIDTH), lambda r, pt: (r, 0, 0)),
        pl.BlockSpec((1, seq_new, KV_WIDTH), lambda r, pt: (r, 0, 0)),
        pl.BlockSpec((1, seq_new, KV_WIDTH), lambda r, pt: (r, 0, 0)),
    ]
    in_specs += [page_spec(p) for p in range(n_pages)]
    in_specs += [page_spec(p) for p in range(n_pages)]
    grid_spec = pltpu.PrefetchScalarGridSpec(
        num_scalar_prefetch=1,
        grid=(n_req,),
        in_specs=in_specs,
        out_specs=pl.BlockSpec((1, n_rows, KV_WIDTH), lambda r, pt: (r, 0, 0)),
        scratch_shapes=[
            pltpu.VMEM((n_pages * page, KV_WIDTH), BF16),
            pltpu.VMEM((n_pages * page, KV_WIDTH), BF16),
            pltpu.VMEM((LANES, KV_WIDTH), F32),
        ],
    )
    return pl.pallas_call(
        functools.partial(_moba_sample_kernel, n_pages=n_pages, page=page),
        grid_spec=grid_spec,
        out_shape=jax.ShapeDtypeStruct((n_req, n_rows, KV_WIDTH), F32),
        compiler_params=_cparams("arbitrary"),
        name="moba_sample",
    )(page_table.reshape(-1), qbd, k_new, v_new, *([cache_k] * n_pages), *([cache_v] * n_pages))


def _hgrn_gates(hq, hf, lb):
    q = _silu(hq)
    f = lb + (1.0 - lb) * _sigmoid(hf)
    k = (1.0 - lb) * _sigmoid(-hf)
    return q, k, f


def _head_rms(o, norm_g):
    head_mean = jnp.where(_block_diag_mask(HG_WIDTH, HG_WIDTH, HG_DK), 1.0 / HG_DK, 0.0)
    ms = jnp.dot(o * o, head_mean, precision=HIGHEST, preferred_element_type=F32)
    return o * lax.rsqrt(ms + EPS) * norm_g


def _short_conv(ext_ref, rows, cw):
    return (cw[0:1, :] * ext_ref[pl.ds(HALO - 2, rows), :]
            + cw[1:2, :] * ext_ref[pl.ds(HALO - 1, rows), :]
            + cw[2:3, :] * ext_ref[pl.ds(HALO, rows), :])


def _hgrn_prompt_kernel(hq_ref, hf_ref, hi_ref, hg_ref, cb_ref, cc_ref, ch_ref, lb_ref, ng_ref, cw_ref,
                        oh_ref, oc_ref, st_ref, tail_ref, st, ext):
    tt = pl.program_id(1)
    rows = hq_ref.shape[0]
    n_chunks = rows // HG_CHUNK
    width = HG_WIDTH
    bd = _block_diag_mask(width, width, HG_DK)

    @pl.when(tt == 0)
    def _():
        ext[pl.ds(0, HALO), :] = jnp.zeros((HALO, CONV_CH), F32)

    @pl.when(tt > 0)
    def _():
        ext[pl.ds(0, HALO), :] = ext[pl.ds(rows, HALO), :]

    u = cc_ref[...] * ch_ref[...]
    ext[pl.ds(HALO, rows), :] = u
    oc_ref[...] = cb_ref[...] * _short_conv(ext, rows, cw_ref[...])
    tail_ref[0] = u[rows - 8:rows, :]

    @pl.when(tt == 0)
    def _():
        st[...] = jnp.zeros(st.shape, F32)

    q, k, f = _hgrn_gates(hq_ref[...], hf_ref[...], lb_ref[...])
    logf = jnp.log(f)
    v = hi_ref[...]
    r_i =lax.broadcasted_iota(jnp.int32, (rows, rows), 0)
    c_i = lax.broadcasted_iota(jnp.int32, (rows, rows), 1)
    tri = jnp.where((r_i // HG_CHUNK == c_i // HG_CHUNK) & (c_i <= r_i), 1.0, 0.0)
    a_all = jnp.dot(tri, logf, precision=HIGHEST, preferred_element_type=F32)

    L = HG_CHUNK
    t_i = lax.broadcasted_iota(jnp.int32, (L, width), 0)
    tw = lax.broadcasted_iota(jnp.int32, (L, width), 0)
    sw_col = lax.broadcasted_iota(jnp.int32, (L, width), 1) % L
    head_ones = jnp.where(bd, 1.0, 0.0).astype(BF16)
    outs = []
    for c in range(n_chunks):
        sl = slice(c * L, (c + 1) * L)
        a, qc, kc, vc = a_all[sl], q[sl], k[sl], v[sl]
        a_end = a[L - 1:L, :]
        st_old = st[...]
        o = _dot_nt((qc * jnp.exp(a)).astype(BF16), st_old.astype(BF16))

        sw = jnp.zeros((L, width), F32)
        b = L // 2
        while b >= HG_LEAF:
            nb = L // b
            rq = jnp.concatenate(
                [jnp.broadcast_to(a[j * b - 1:j * b, :], (b, width)) if j % 2 else a[j * b:(j + 1) * b, :]
                 for j in range(nb)], axis=0)
            rk = jnp.concatenate(
                [a[j * b:(j + 1) * b, :] if j % 2 else jnp.broadcast_to(a[(j + 1) * b - 1:(j + 1) * b, :], (b, width))
                 for j in range(nb)], axis=0)
            odd = (t_i // b) % 2 == 1
            ql = jnp.where(odd, qc * jnp.exp(a - rq), 0.0).astype(BF16)
            kl = jnp.where(odd, 0.0, kc * jnp.exp(rk - a))
            k_bd = jnp.where(bd, jnp.concatenate([kl] * HG_HEADS, axis=0), 0.0).astype(BF16)
            sc = _dot_nt(ql, k_bd)
            pair = ((tw // b) % 2 == 1) & (sw_col // b == tw // b - 1)
            sw = sw + jnp.where(pair, sc, 0.0)
            b //= 2
        v_bd = jnp.where(bd, jnp.concatenate([vc] * HG_HEADS, axis=0), 0.0).astype(BF16)
        o = o + _dot(sw.astype(BF16), v_bd)

        prods = []
        vrs = []
        for delta in range(HG_LEAF):
            if delta:
                kr, ar, vr = (pltpu.roll(x, delta, 0) for x in (kc, a, vc))
            else:
                kr, ar, vr = kc, a, vc
            live = (t_i % HG_LEAF) >= delta
            prods.append(jnp.where(live, qc * kr * jnp.exp(jnp.where(live, a - ar, 0.0)), 0.0))
            vrs.append(vr)
        sc = _dot(jnp.concatenate(prods, axis=0).astype(BF16), head_ones)
        for delta in range(HG_LEAF):
            o = o + sc[delta * L:(delta + 1) * L] * vrs[delta]
        outs.append(o)

        ke = (kc * jnp.exp(a_end - a)).astype(BF16)
        upd = _dot_tn(vc.astype(BF16), ke)
        st[...] = st_old * jnp.exp(a_end) + jnp.where(bd, upd, 0.0)

    o_all = jnp.concatenate(outs, axis=0)
    oh_ref[...] = _head_rms(o_all, ng_ref[...]) * _silu(hg_ref[...])
    st_ref[0] = st[...]


def _hgrn_prompt(h, lb, norm_g, cw, batch, seq):
    rows = 256
    nt = seq // rows

    def col(cb):
        return pl.BlockSpec((rows, COL), lambda b, t: (b * nt + t, cb))

    const = lambda shape: pl.BlockSpec(shape, lambda b, t: (0, 0))
    row_out = pl.BlockSpec((rows, COL), lambda b, t: (b * nt + t, 0))
    return pl.pallas_call(
        _hgrn_prompt_kernel,
        grid=(batch, nt),
        in_specs=[col(CB_HQ), col(CB_HF), col(CB_HI), col(CB_HG), col(CB_CB), col(CB_CC), col(CB_CH),
                  const((1, HG_WIDTH)), const((1, HG_WIDTH)), const((3, CONV_CH))],
        out_specs=[row_out, row_out,
                   pl.BlockSpec((1, HG_WIDTH, HG_WIDTH), lambda b, t: (b, 0, 0)),
                   pl.BlockSpec((1, 8, CONV_CH), lambda b, t: (b, 0, 0))],
        out_shape=[jax.ShapeDtypeStruct((batch * seq, HG_WIDTH), F32),
                   jax.ShapeDtypeStruct((batch * seq, CONV_CH), F32),
                   jax.ShapeDtypeStruct((batch, HG_WIDTH, HG_WIDTH), F32),
                   jax.ShapeDtypeStruct((batch, 8, CONV_CH), F32)],
        scratch_shapes=[pltpu.VMEM((HG_WIDTH, HG_WIDTH), F32),
                        pltpu.VMEM((HALO + rows + HALO, CONV_CH), F32)],
        compiler_params=_cparams("parallel", "arbitrary"),
        name="hgrn_prompt",
    )(h, h, h, h, h, h, h, lb.reshape(1, -1), norm_g.reshape(1, -1), cw)


def _hgrn_sample_kernel(hq_ref, hf_ref, hi_ref, hg_ref, cb_ref, cc_ref, ch_ref, s0_ref, cprev_ref,
                        lb_ref, ng_ref, cw_ref, oh_ref, oc_ref, s_ref, cnew_ref, obuf):
    seq = hq_ref.shape[1]
    width = HG_WIDTH
    bd = _block_diag_mask(width, width, HG_DK)
    head_ones = jnp.where(bd, 1.0, 0.0).astype(BF16)

    cw = cw_ref[...]
    u = cc_ref[0] * ch_ref[0]
    cprev = cprev_ref[0]
    full = [cprev[0:1, :], cprev[1:2, :]] + [u[t:t + 1, :] for t in range(seq)]
    cb = cb_ref[0]
    for t in range(seq):
        y = cw[0:1, :] * full[t] + cw[1:2, :] * full[t + 1] + cw[2:3, :] * full[t + 2]
        oc_ref[0, t:t + 1, :] = cb[t:t + 1, :] * y
    cnew_ref[0, 0:1, :] = full[seq]
    cnew_ref[0, 1:2, :] = full[seq + 1]

    q, k, f = _hgrn_gates(hq_ref[0], hf_ref[0], lb_ref[...])
    f_hi = f.astype(BF16).astype(F32)
    f_lo = f - f_hi
    v = hi_ref[0]
    eye = (lax.broadcasted_iota(jnp.int32, (HG_DK, width), 0)
           == lax.broadcasted_iota(jnp.int32, (HG_DK, width), 1) % HG_DK)

    def expand(row):
        return jnp.where(eye, jnp.broadcast_to(row, (HG_DK, width)), 0.0).astype(BF16)

    stacked = []
    for t in range(seq):
        stacked += [expand(f_hi[t:t + 1, :]), expand(f_lo[t:t + 1, :]),
                    expand(k[t:t + 1, :]), expand(q[t:t + 1, :])]
    ex = _dot(jnp.concatenate(stacked, axis=0), head_ones)

    s = s0_ref[0]
    obuf[...] = jnp.zeros(obuf.shape, F32)
    for t in range(seq):
        base = t * 4 * HG_DK
        f_m = ex[base:base + HG_DK] + ex[base + HG_DK:base + 2 * HG_DK]
        k_m = ex[base + 2 * HG_DK:base + 3 * HG_DK]
        q_m = ex[base + 3 * HG_DK:base + 4 * HG_DK]
        s = f_m * s + k_m * v[t:t + 1, :]
        obuf[t:t + 1, :] = jnp.sum(q_m * s, axis=0, keepdims=True)
    s_ref[0] = s
    o = _head_rms(obuf[...], ng_ref[...])
    oh_ref[0] = o[0:seq, :] * _silu(hg_ref[0])


def _hgrn_sample(h3, s0, cprev, lb, norm_g, cw):
    n_req, seq, _ = h3.shape

    def col(cb):
        return pl.BlockSpec((1, seq, COL), lambda r: (r, 0, cb))

    const = lambda shape: pl.BlockSpec(shape, lambda r: (0, 0))
    row_out = pl.BlockSpec((1, seq, COL), lambda r: (r, 0, 0))
    return pl.pallas_call(
        _hgrn_sample_kernel,
        grid=(n_req,),
        in_specs=[col(CB_HQ), col(CB_HF), col(CB_HI), col(CB_HG), col(CB_CB), col(CB_CC), col(CB_CH),
                  pl.BlockSpec((1, HG_DK, HG_WIDTH), lambda r: (r, 0, 0)),
                  pl.BlockSpec((1, 2, CONV_CH), lambda r: (r, 0, 0)),
                  const((1, HG_WIDTH)), const((1, HG_WIDTH)), const((3, CONV_CH))],
        out_specs=[row_out, row_out,
                   pl.BlockSpec((1, HG_DK, HG_WIDTH), lambda r: (r, 0, 0)),
                   pl.BlockSpec((1, 2, CONV_CH), lambda r: (r, 0, 0))],
        out_shape=[jax.ShapeDtypeStruct((n_req, seq, HG_WIDTH), F32),
                   jax.ShapeDtypeStruct((n_req, seq, CONV_CH), F32),
                   jax.ShapeDtypeStruct((n_req, HG_DK, HG_WIDTH), F32),
                   jax.ShapeDtypeStruct((n_req, 2, CONV_CH), F32)],
        scratch_shapes=[pltpu.VMEM((8, HG_WIDTH), F32)],
        compiler_params=_cparams("parallel"),
        name="hgrn_sample",
    )(h3, h3, h3, h3, h3, h3, h3, s0, cprev, lb.reshape(1, -1), norm_g.reshape(1, -1), cw)


def _xattn_prompt_kernel(q_ref, mk_ref, mv_ref, o_ref):
    q = q_ref[...] * (XA_WIDTH // XA_HEADS) ** -0.5
    mk = mk_ref[...].astype(BF16)
    mv = mv_ref[...].astype(BF16)
    head = lax.broadcasted_iota(jnp.int32, q.shape, 1) // (XA_WIDTH // XA_HEADS)
    out = jnp.zeros(q.shape, F32)
    for hh in range(XA_HEADS):
        mine = head == hh
        s = _dot_nt(jnp.where(mine, q, 0.0).astype(BF16), mk)
        m = jnp.max(s, axis=-1, keepdims=True)
        p = jnp.exp(s - m)
        l = jnp.sum(p, axis=-1, keepdims=True)
        out = jnp.where(mine, _dot(p.astype(BF16), mv) / l, out)
    o_ref[...] = out


def _xattn_prompt(q, mkv, batch, seq):
    tm = min(ROW_TILE, seq)
    nt = seq // tm
    return pl.pallas_call(
        _xattn_prompt_kernel,
        grid=(batch, nt),
        in_specs=[pl.BlockSpec((tm, XA_WIDTH), lambda b, t: (b * nt + t, 0)),
                  pl.BlockSpec((N_MEM, XA_WIDTH), lambda b, t: (b, 0)),
                  pl.BlockSpec((N_MEM, XA_WIDTH), lambda b, t: (b, 1))],
        out_specs=pl.BlockSpec((tm, XA_WIDTH), lambda b, t: (b * nt + t, 0)),
        out_shape=jax.ShapeDtypeStruct((batch * seq, XA_WIDTH), F32),
        compiler_params=_cparams("parallel", "parallel"),
        name="xattn_prompt",
    )(q, mkv, mkv)


def _xattn_sample_kernel(qbd_ref, mk_ref, mv_ref, o_ref):
    q = (qbd_ref[0] * (XA_WIDTH // XA_HEADS) ** -0.5).astype(BF16)
    s = _dot_nt(q, mk_ref[...].astype(BF16))
    m = jnp.max(s, axis=-1, keepdims=True)
    p = jnp.exp(s - m)
    l = jnp.sum(p, axis=-1, keepdims=True)
    o_ref[0] = _dot(p.astype(BF16), mv_ref[...].astype(BF16)) / l


def _xattn_sample(qbd, mem_k, mem_v, layer):
    n_req, n_rows, _ = qbd.shape
    mem_spec = pl.BlockSpec((None, None, N_MEM, XA_WIDTH), lambda r: (layer, r, 0, 0))
    return pl.pallas_call(
        _xattn_sample_kernel,
        grid=(n_req,),
        in_specs=[pl.BlockSpec((1, n_rows, XA_WIDTH), lambda r: (r, 0, 0)), mem_spec, mem_spec],
        out_specs=pl.BlockSpec((1, n_rows, XA_WIDTH), lambda r: (r, 0, 0)),
        out_shape=jax.ShapeDtypeStruct((n_req, n_rows, XA_WIDTH), F32),
        compiler_params=_cparams("parallel"),
        name="xattn_sample",
    )(qbd, mem_k, mem_v)


def _ffn_prompt_kernel(x_ref, halo_ref, g_ref, wup_ref, cw_ref, wdn_ref, gf_ref, o_ref, u_ref,
                       xn_sc, ext_a, ext_b, acc, *, seq, final_norm):
    rows = x_ref.shape[0]
    x = x_ref[...]
    g = g_ref[...]
    xn_sc[pl.ds(0, HALO), :] = _rms(halo_ref[...], g).astype(BF16)
    xn_sc[pl.ds(HALO, rows), :] = _rms(x, g).astype(BF16)
    xn = xn_sc[...]

    t_pos = (pl.program_id(0) * rows + lax.broadcasted_iota(jnp.int32, (rows, 1), 0)) % seq
    keep1 = t_pos >= 1
    keep2 = t_pos >= 2

    def conv(ext, u_ext, c0):
        ext[...] = u_ext
        cw = cw_ref[:, pl.ds(c0, COL)]
        s1 = jnp.where(keep1, ext[pl.ds(HALO - 1, rows), :], 0.0)
        s2 = jnp.where(keep2, ext[pl.ds(HALO - 2, rows), :], 0.0)
        return cw[0:1, :] * s2 + cw[1:2, :] * s1 + cw[2:3, :] * u_ext[HALO:, :]

    acc[...] = jnp.zeros(acc.shape, F32)

    def body(j, carry):
        ca = pl.multiple_of(j * COL, COL)
        cb = pl.multiple_of(D_FF + j * COL, COL)
        ua = _dot(xn, wup_ref[:, pl.ds(ca, COL)])
        ub = _dot(xn, wup_ref[:, pl.ds(cb, COL)])
        u_ref[0, :, pl.ds(ca, COL)] = ua[HALO + rows - 8:, :]
        u_ref[0, :, pl.ds(cb, COL)] = ub[HALO + rows - 8:, :]
        hmid = (_silu(conv(ext_a, ua, ca)) * conv(ext_b, ub, cb)).astype(BF16)
        acc[...] += _dot(hmid, wdn_ref[pl.ds(ca, COL), :])
        return carry

    lax.fori_loop(0, D_FF // COL, body, 0)
    y = x + acc[...]
    if final_norm:
        y = _rms(y, gf_ref[...])
    o_ref[...] = y


def _ffn_sample_kernel(x_ref, p0_ref, p1_ref, g_ref, wup_ref, cw_ref, wdn_ref, gf_ref, o_ref, u2_ref, u3_ref,
                       acc, *, final_norm):
    seq, n_req, d = x_ref.shape
    x = x_ref[...].reshape(seq * n_req, d)
    xn = _rms(x, g_ref[...]).astype(BF16)
    acc[...] = jnp.zeros(acc.shape, F32)

    def conv(u, c0):
        cw = cw_ref[:, pl.ds(c0, COL)]
        full = [p0_ref[:, pl.ds(c0, COL)], p1_ref[:, pl.ds(c0, COL)]]
        full += [u[t * n_req:(t + 1) * n_req, :] for t in range(seq)]
        u2_ref[:, pl.ds(c0, COL)] = full[seq]
        u3_ref[:, pl.ds(c0, COL)] = full[seq + 1]
        return jnp.concatenate(
            [cw[0:1, :] * full[t] + cw[1:2, :] * full[t + 1] + cw[2:3, :] * full[t + 2] for t in range(seq)],
            axis=0)

    def body(j, carry):
        ca = pl.multiple_of(j * COL, COL)
        cb = pl.multiple_of(D_FF + j * COL, COL)
        ya = conv(_dot(xn, wup_ref[:, pl.ds(ca, COL)]), ca)
        yb = conv(_dot(xn, wup_ref[:, pl.ds(cb, COL)]), cb)
        acc[...] += _dot((_silu(ya) * yb).astype(BF16), wdn_ref[pl.ds(ca, COL), :])
        return carry

    lax.fori_loop(0, D_FF // COL, body, 0)
    y = x + acc[...]
    if final_norm:
        y = _rms(y, gf_ref[...])
    o_ref[...] = y.reshape(seq, n_req, d)


def _resident(shape):
    return pl.BlockSpec(shape, lambda i: (0,) * len(shape), pipeline_mode=pl.Buffered(1))


def _ffn_prompt(x, g, w_up, cw, w_dn, g_final, seq, final_norm):
    rows, d = x.shape
    tm = min(ROW_TILE, seq)
    n_up = w_up.shape[1]
    tiles_per_seq = seq // tm
    x_spec = pl.BlockSpec((tm, d), lambda i: (i, 0))
    halo_spec = pl.BlockSpec((HALO, d), lambda i: (jnp.maximum(i * (tm // HALO) - 1, 0), 0))
    return pl.pallas_call(
        functools.partial(_ffn_prompt_kernel, seq=seq, final_norm=final_norm),
        grid=(rows // tm,),
        in_specs=[x_spec, halo_spec, _resident((1, d)), _resident((d, n_up)), _resident((3, n_up)),
                  _resident((D_FF, d)), _resident((1, d))],
        out_specs=[x_spec, pl.BlockSpec((1, 8, n_up), lambda i: (i // tiles_per_seq, 0, 0))],
        out_shape=[jax.ShapeDtypeStruct((rows, d), F32),
                   jax.ShapeDtypeStruct((rows // seq, 8, n_up), F32)],
        scratch_shapes=[pltpu.VMEM((HALO + tm, d), BF16),
                        pltpu.VMEM((HALO + tm, COL), F32),
                        pltpu.VMEM((HALO + tm, COL), F32),
                        pltpu.VMEM((tm, d), F32)],
        compiler_params=_cparams("arbitrary"),
        name="ffn_prompt",
    )(x, x, g.reshape(1, d), w_up, cw, w_dn, g_final.reshape(1, d))


def _ffn_sample(x, prev, g, w_up, cw, w_dn, g_final, final_norm):
    seq, n_req, d = x.shape
    n_up = w_up.shape[1]
    full = lambda shape: pl.BlockSpec(shape, lambda i: (0,) * len(shape))
    return pl.pallas_call(
        functools.partial(_ffn_sample_kernel, final_norm=final_norm),
        grid=(1,),
        in_specs=[_resident((seq, n_req, d)), _resident((n_req, n_up)), _resident((n_req, n_up)),
                  _resident((1, d)), _resident((d, n_up)), _resident((3, n_up)), _resident((D_FF, d)),
                  _resident((1, d))],
        out_specs=[full((seq, n_req, d)), full((n_req, n_up)), full((n_req, n_up))],
        out_shape=[jax.ShapeDtypeStruct((seq, n_req, d), F32),
                   jax.ShapeDtypeStruct((n_req, n_up), F32),
                   jax.ShapeDtypeStruct((n_req, n_up), F32)],
        scratch_shapes=[pltpu.VMEM((seq * n_req, d), F32)],
        compiler_params=_cparams("arbitrary"),
        name="ffn_sample",
    )(x, prev[:, 0], prev[:, 1], g.reshape(1, d), w_up, cw, w_dn, g_final.reshape(1, d))


def _head_rows(q, n_heads, group):
    n_req, seq, _ = q.shape
    n_kv = n_heads // group
    qh = q.reshape(n_req, seq, n_heads, HEAD_DIM).transpose(0, 2, 1, 3)
    onehot = jax.nn.one_hot(jnp.arange(n_heads) // group, n_kv, dtype=q.dtype)
    out = qh[:, :, :, None, :] * onehot[None, :, None, :, None]
    return out.reshape(n_req, n_heads * seq, n_kv * HEAD_DIM)


def _head_rows_inverse(o, n_heads, group, seq):
    n_req = o.shape[0]
    n_kv = n_heads // group
    o5 = o.reshape(n_req, n_heads, seq, n_kv, HEAD_DIM)
    picked = jnp.stack([o5[:, hh, :, hh // group, :] for hh in range(n_heads)], axis=2)
    return picked.reshape(n_req * seq, n_heads * HEAD_DIM)


def kernel(x_prompt, x_sample, cache_k, cache_v, cache_mem_k, cache_mem_v, state_hgrn, state_conv,
           state_ffn, page_table, mem_prompt, g_mix, w_in, w_out, hg_lb_logits, hg_norm_g, conv_w,
           g_xattn, g_mem, w_xq, w_mk, w_mv, w_xo, g_ffn, w_up, ffn_conv_w, w_down, g_final):
    B, T, D = x_prompt.shape
    DB, S, _ = x_sample.shape
    depth = w_in.shape[0]
    n_pages = page_table.shape[1]
    page = cache_k.shape[2]
    past = n_pages * page

    lb_p = jax.nn.softmax(hg_lb_logits.astype(F32), axis=0)
    lower = jnp.cumsum(lb_p, axis=0) - lb_p[0:1]

    rope_p = _rope_tables(jnp.arange(T, dtype=jnp.int32))
    rope_s = _rope_tables(jnp.tile(past + jnp.arange(S, dtype=jnp.int32), DB))

    w_in_b, w_out_b, w_xq_b, w_xo_b = (w.astype(BF16) for w in (w_in, w_out, w_xq, w_xo))
    w_up_b, w_down_b = w_up.astype(BF16), w_down.astype(BF16)
    w_mkv_b = jnp.concatenate([w_mk, w_mv], axis=-1).astype(BF16)

    cache_k4 = cache_k.reshape(depth, cache_k.shape[1], page, KV_WIDTH)
    cache_v4 = cache_v.reshape(depth, cache_v.shape[1], page, KV_WIDTH)
    mem_k4 = cache_mem_k.reshape(depth, DB, N_MEM, XA_WIDTH)
    mem_v4 = cache_mem_v.reshape(depth, DB, N_MEM, XA_WIDTH)
    mem2 = mem_prompt.reshape(B * N_MEM, D)

    xp = x_prompt.reshape(B * T, D)
    xs = x_sample.reshape(DB * S, D)
    outs = [[] for _ in range(12)]
    for l in range(depth):
        last = l == depth - 1
        h = _norm_proj(xp, g_mix[l], w_in_b[l], rope=rope_p, n_rope=(ATT_WIDTH + KV_WIDTH) // COL)
        k1 = h[:, CB_K * COL:(CB_K + 1) * COL].reshape(B, T, KV_HEADS, HEAD_DIM)
        v1 = h[:, CB_V * COL:(CB_V + 1) * COL].reshape(B, T, KV_HEADS, HEAD_DIM)
        a = _moba_prompt(h, B, T)
        o_h, o_c, st_t, c_tail = _hgrn_prompt(h, lower[l], hg_norm_g[l], conv_w[l], B, T)
        st5 = st_t.reshape(B, HG_HEADS, HG_DK, HG_HEADS, HG_DK)
        h1 = jnp.stack([st5[:, hh, :, hh, :] for hh in range(HG_HEADS)], axis=1).swapaxes(-1, -2)
        c1 = c_tail[:, 6:8, :]
        xp = _resid_proj(xp, [a, o_h, o_c],
                         [w_out_b[l, :ATT_WIDTH], w_out_b[l, ATT_WIDTH:ATT_WIDTH + HG_WIDTH],
                          w_out_b[l, ATT_WIDTH + HG_WIDTH:]])
        mkv = _norm_proj(mem2, g_mem[l], w_mkv_b[l])
        mk = mkv[:, :XA_WIDTH].reshape(B, N_MEM, XA_HEADS, XA_WIDTH // XA_HEADS)
        mv = mkv[:, XA_WIDTH:].reshape(B, N_MEM, XA_HEADS, XA_WIDTH // XA_HEADS)
        qx = _norm_proj(xp, g_xattn[l], w_xq_b[l])
        ox = _xattn_prompt(qx, mkv, B, T)
        xp = _resid_proj(xp, [ox], [w_xo_b[l]])
        xp, u_tail = _ffn_prompt(xp, g_ffn[l], w_up_b[l], ffn_conv_w[l], w_down_b[l], g_final, T, last)
        fs1 = u_tail[:, 6:8, :]

        hs = _norm_proj(xs, g_mix[l], w_in_b[l], rope=rope_s, n_rope=(ATT_WIDTH + KV_WIDTH) // COL)
        hs3 = hs.reshape(DB, S, IN_WIDTH)
        k_new = hs3[:, :, CB_K * COL:(CB_K + 1) * COL]
        v_new = hs3[:, :, CB_V * COL:(CB_V + 1) * COL]
        qbd = _head_rows(hs3[:, :, :ATT_WIDTH], ATT_HEADS, ATT_HEADS // KV_HEADS)
        o_att = _moba_sample(qbd, k_new, v_new, cache_k4, cache_v4, page_table, l)
        a_s = _head_rows_inverse(o_att, ATT_HEADS, ATT_HEADS // KV_HEADS, S)
        s0 = state_hgrn[l].transpose(0, 2, 1, 3).reshape(DB, HG_DK, HG_WIDTH)
        oh_s, oc_s, s_new, c2 = _hgrn_sample(hs3, s0, state_conv[l], lower[l], hg_norm_g[l], conv_w[l])
        h2 = s_new.reshape(DB, HG_DK, HG_HEADS, HG_DK).transpose(0, 2, 1, 3)
        xs = _resid_proj(xs, [a_s, oh_s.reshape(DB * S, HG_WIDTH), oc_s.reshape(DB * S, CONV_CH)],
                         [w_out_b[l, :ATT_WIDTH], w_out_b[l, ATT_WIDTH:ATT_WIDTH + HG_WIDTH],
                          w_out_b[l, ATT_WIDTH + HG_WIDTH:]])
        qxs = _norm_proj(xs, g_xattn[l], w_xq_b[l])
        qxbd = _head_rows(qxs.reshape(DB, S, XA_WIDTH), XA_HEADS, 1)
        oxs = _xattn_sample(qxbd, mem_k4, mem_v4, l)
        xs = _resid_proj(xs, [_head_rows_inverse(oxs, XA_HEADS, 1, S)], [w_xo_b[l]])
        xs_t, u_a, u_b = _ffn_sample(xs.reshape(DB, S, D).swapaxes(0, 1), state_ffn[l], g_ffn[l], w_up_b[l],
                                     ffn_conv_w[l], w_down_b[l], g_final, last)
        xs = xs_t.swapaxes(0, 1).reshape(DB * S, D)
        fs2 = jnp.stack([u_a, u_b], axis=1)

        for lst, val in zip(outs, (k1, v1, k_new.reshape(DB, S, KV_HEADS, HEAD_DIM),
                                   v_new.reshape(DB, S, KV_HEADS, HEAD_DIM), h1, h2, c1, c2, fs1, fs2, mk, mv)):
            lst.append(val)

    return (xp.reshape(B, T, D), xs.reshape(DB, S, D)) + tuple(jnp.stack(o) for o in outs)
```

```python
import functools

import numpy as np
import jax
import jax.numpy as jnp
from jax import lax
from jax.experimental import pallas as pl
from jax.experimental.pallas import tpu as pltpu

F32 = jnp.float32
BF16 = jnp.bfloat16
HIGHEST = lax.Precision.HIGHEST
NEG_INF = float("-inf")

D_MODEL = 1024
HEAD_DIM = 64
ATT_HEADS = 8
KV_HEADS = 4
ATT_WIDTH = ATT_HEADS * HEAD_DIM
KV_WIDTH = KV_HEADS * HEAD_DIM
MOBA_BLOCK = 256
MOBA_TOPK = 3
ROPE_THETA = 10000.0
HG_HEADS = 4
HG_DK = 64
HG_WIDTH = HG_HEADS * HG_DK
HG_CHUNK = 64
HG_LEAF = 8
CONV_CH = 256
N_MEM = 256
XA_HEADS = 4
XA_WIDTH = 256
D_FF = 2816
EPS = 1e-6
IN_WIDTH = 2816
COL = 256
LANES = 128
ROW_TILE = 512
HALO = 16
VMEM_LIMIT = 56 * 1024 * 1024

CB_K, CB_V, CB_HQ, CB_HF, CB_HI, CB_HG, CB_CB, CB_CC, CB_CH = 2, 3, 4, 5, 6, 7, 8, 9, 10


def _cparams(*sem):
    return pltpu.CompilerParams(dimension_semantics=sem, vmem_limit_bytes=VMEM_LIMIT)


def _rms(x, g):
    ms = jnp.mean(x * x, axis=-1, keepdims=True)
    return x * lax.rsqrt(ms + EPS) * g


def _sigmoid(z):
    return 1.0 / (1.0 + jnp.exp(-z))


def _silu(z):
    return z * _sigmoid(z)


def _dot(a, b):
    return jnp.dot(a, b, preferred_element_type=F32)


def _dot_nt(a, b, precision=None):
    return lax.dot_general(a, b, (((1,), (1,)), ((), ())), precision=precision,
                           preferred_element_type=F32)


def _dot_tn(a, b):
    return lax.dot_general(a, b, (((0,), (0,)), ((), ())), preferred_element_type=F32)


def _block_diag_mask(rows, cols, blk):
    r = lax.broadcasted_iota(jnp.int32, (rows, cols), 0) // blk
    c = lax.broadcasted_iota(jnp.int32, (rows, cols), 1) // blk
    return r == c


def _norm_proj_kernel(x_ref, g_ref, w_ref, cos_ref, sin_ref, o_ref, *, n_rope):
    xn = _rms(x_ref[...], g_ref[...]).astype(BF16)
    n_chunks = o_ref.shape[1] // COL
    if n_rope:
        cos = jnp.concatenate([cos_ref[...]] * (COL // LANES), axis=1)
        sin = jnp.concatenate([sin_ref[...]] * (COL // LANES), axis=1)
        lane = lax.broadcasted_iota(jnp.int32, cos.shape, 1)
        first_half = (lane % HEAD_DIM) < HEAD_DIM // 2
    for c in range(n_chunks):
        y = _dot(xn, w_ref[:, c * COL:(c + 1) * COL])
        if c < n_rope:
            partner = jnp.where(first_half, pltpu.roll(y, COL - HEAD_DIM // 2, 1),
                                pltpu.roll(y, HEAD_DIM // 2, 1))
            y = y * cos + partner * sin
        o_ref[:, c * COL:(c + 1) * COL] = y


def _norm_proj(x, g, w, rope=None, n_rope=0):
    rows, d = x.shape
    n = w.shape[1]
    tm = min(ROW_TILE, rows)
    if rope is None:
        cos = sin = jnp.zeros((tm, LANES), F32)
    else:
        cos, sin = rope
    tab_blocks = cos.shape[0] // tm
    return pl.pallas_call(
        functools.partial(_norm_proj_kernel, n_rope=n_rope),
        grid=(rows // tm,),
        in_specs=[
            pl.BlockSpec((tm, d), lambda i: (i, 0)),
            pl.BlockSpec((1, d), lambda i: (0, 0)),
            pl.BlockSpec((d, n), lambda i: (0, 0)),
            pl.BlockSpec((tm, LANES), lambda i: (i % tab_blocks, 0)),
            pl.BlockSpec((tm, LANES), lambda i: (i % tab_blocks, 0)),
        ],
        out_specs=pl.BlockSpec((tm, n), lambda i: (i, 0)),
        out_shape=jax.ShapeDtypeStruct((rows, n), F32),
        compiler_params=_cparams("parallel"),
        name="norm_proj",
    )(x, g.reshape(1, d), w, cos, sin)


def _rope_tables(pos):
    half = HEAD_DIM // 2
    inv = ROPE_THETA ** (-jnp.arange(half, dtype=F32) / half)
    ang = pos.astype(F32)[:, None] * inv[None, :]
    cos = jnp.cos(ang)
    sin = jnp.sin(ang)
    reps = LANES // HEAD_DIM
    cos_t = jnp.concatenate([cos, cos] * reps, axis=1)
    sin_t = jnp.concatenate([-sin, sin] * reps, axis=1)
    return cos_t, sin_t


def _resid_proj_kernel(*refs, n_in):
    x_ref = refs[0]
    a_refs = refs[1:1 + n_in]
    w_refs = refs[1 + n_in:1 + 2 * n_in]
    o_ref = refs[1 + 2 * n_in]
    acc = x_ref[...]
    for a_ref, w_ref in zip(a_refs, w_refs):
        acc = acc + _dot(a_ref[...].astype(BF16), w_ref[...])
    o_ref[...] = acc


def _resid_proj(x, acts, weights):
    rows, d = x.shape
    tm = min(ROW_TILE, rows)
    n_in = len(acts)
    in_specs = [pl.BlockSpec((tm, d), lambda i: (i, 0))]
    in_specs += [pl.BlockSpec((tm, a.shape[1]), lambda i: (i, 0)) for a in acts]
    in_specs += [pl.BlockSpec(w.shape, lambda i: (0, 0)) for w in weights]
    return pl.pallas_call(
        functools.partial(_resid_proj_kernel, n_in=n_in),
        grid=(rows // tm,),
        in_specs=in_specs,
        out_specs=pl.BlockSpec((tm, d), lambda i: (i, 0)),
        out_shape=jax.ShapeDtypeStruct((rows, d), F32),
        compiler_params=_cparams("parallel"),
        name="resid_proj",
    )(x, *acts, *weights)


def _topk_rank(gate, n_cand, n_valid, axis):
    idx = lax.broadcasted_iota(jnp.int32, gate.shape, axis)
    cnt = jnp.zeros(gate.shape, F32)
    for m in range(n_cand):
        gm = gate[m:m + 1, :] if axis == 0 else gate[:, m:m + 1]
        beats = jnp.where(gm > gate, 1.0, jnp.where(gm == gate, jnp.where(m < idx, 1.0, 0.0), 0.0))
        cnt = cnt + beats * jnp.where(m < n_valid, 1.0, 0.0)
    return cnt, idx


def _moba_prompt_kernel(q_ref, k_ref, v_ref, o_ref, kbf, vt, kmean, bias, m_sc, l_sc, acc, qsc):
    i = pl.program_id(2)
    n_blk = k_ref.shape[0] // MOBA_BLOCK

    @pl.when(i == 0)
    def _():
        k = k_ref[...]
        kbf[...] = k.astype(BF16)
        vt[...] = v_ref[...].T.astype(BF16)
        kmean[...] = jnp.mean(k.reshape(n_blk, MOBA_BLOCK, LANES), axis=1)

    qb = q_ref[...]
    lane = lax.broadcasted_iota(jnp.int32, (MOBA_BLOCK, LANES), 1)
    lo = lane < HEAD_DIM
    t0, t1 = qb[:, :LANES], qb[:, LANES:]
    r0, r1 = pltpu.roll(t0, HEAD_DIM, 1), pltpu.roll(t1, HEAD_DIM, 1)
    zero = jnp.zeros_like(t0)
    qs = jnp.concatenate([jnp.where(lo, t0, zero), jnp.where(lo, r0, zero),
                          jnp.where(lo, zero, r1), jnp.where(lo, zero, t1)], axis=0)
    qst = qs.T

    gate = jnp.dot(kmean[...], qst, precision=HIGHEST, preferred_element_type=F32)
    cnt, bidx = _topk_rank(gate, n_blk, i, 0)
    bias[...] = jnp.where((bidx < i) & (cnt < MOBA_TOPK), 0.0, NEG_INF)

    qsc[...] = (qst * HEAD_DIM ** -0.5).astype(BF16)
    nq = 4 * MOBA_BLOCK

    def scores(n):
        kn = kbf[pl.ds(pl.multiple_of(n * MOBA_BLOCK, MOBA_BLOCK), MOBA_BLOCK), :]
        return _dot(kn, qsc[...])

    def values(n):
        return vt[:, pl.ds(pl.multiple_of(n * MOBA_BLOCK, MOBA_BLOCK), MOBA_BLOCK)]

    s = scores(i)
    kpos = lax.broadcasted_iota(jnp.int32, (MOBA_BLOCK, nq), 0)
    qpos = lax.broadcasted_iota(jnp.int32, (MOBA_BLOCK, nq), 1) % MOBA_BLOCK
    s = jnp.where(kpos <= qpos, s, NEG_INF)
    m0 = jnp.max(s, axis=0, keepdims=True)
    p = jnp.exp(s - m0)
    m_sc[...] = m0
    l_sc[...] = jnp.sum(p, axis=0, keepdims=True)
    acc[...] = _dot(values(i), p.astype(BF16))

    def body(n, carry):
        s = scores(n) + bias[pl.ds(n, 1), :]
        m_old = m_sc[...]
        m_new = jnp.maximum(m_old, jnp.max(s, axis=0, keepdims=True))
        alpha = jnp.exp(m_old - m_new)
        p = jnp.exp(s - m_new)
        m_sc[...] = m_new
        l_sc[...] = alpha * l_sc[...] + jnp.sum(p, axis=0, keepdims=True)
        acc[...] = alpha * acc[...] + _dot(values(n), p.astype(BF16))
        return carry

    lax.fori_loop(0, i, body, 0)

    o = (acc[...] / l_sc[...]).T
    b = MOBA_BLOCK
    tile0 = jnp.where(lo, o[0:b], pltpu.roll(o[b:2 * b], HEAD_DIM, 1))
    tile1 = jnp.where(lo, pltpu.roll(o[2 * b:3 * b], HEAD_DIM, 1), o[3 * b:4 * b])
    o_ref[...] = jnp.concatenate([tile0, tile1], axis=1)


def _moba_prompt(h, batch, seq):
    n_blk = seq // MOBA_BLOCK
    kcol = (CB_K * COL) // LANES
    vcol = (CB_V * COL) // LANES
    return pl.pallas_call(
        _moba_prompt_kernel,
        grid=(batch, KV_HEADS // 2, n_blk),
        in_specs=[
            pl.BlockSpec((MOBA_BLOCK, COL), lambda b, p, i: (b * n_blk + i, p)),
            pl.BlockSpec((seq, LANES), lambda b, p, i: (b, kcol + p)),
            pl.BlockSpec((seq, LANES), lambda b, p, i: (b, vcol + p)),
        ],
        out_specs=pl.BlockSpec((MOBA_BLOCK, COL), lambda b, p, i: (b * n_blk + i, p)),
        out_shape=jax.ShapeDtypeStruct((batch * seq, ATT_WIDTH), F32),
        scratch_shapes=[
            pltpu.VMEM((seq, LANES), BF16),
            pltpu.VMEM((LANES, seq), BF16),
            pltpu.VMEM((n_blk, LANES), F32),
            pltpu.VMEM((n_blk, 4 * MOBA_BLOCK), F32),
            pltpu.VMEM((1, 4 * MOBA_BLOCK), F32),
            pltpu.VMEM((1, 4 * MOBA_BLOCK), F32),
            pltpu.VMEM((LANES, 4 * MOBA_BLOCK), F32),
            pltpu.VMEM((LANES, 4 * MOBA_BLOCK), BF16),
        ],
        compiler_params=_cparams("parallel", "parallel", "arbitrary"),
        name="moba_prompt",
    )(h, h, h)


def _moba_sample_kernel(pt_ref, qbd_ref, kn_ref, vn_ref, *rest, n_pages, page):
    del pt_ref
    kp = rest[:n_pages]
    vp = rest[n_pages:2 * n_pages]
    o_ref = rest[2 * n_pages]
    kc, vc = rest[2 * n_pages + 1:]
    n_blk = n_pages * page // MOBA_BLOCK
    per_blk = MOBA_BLOCK // page
    n_rows = qbd_ref.shape[1]
    seq_new = kn_ref.shape[1]

    lane = lax.broadcasted_iota(jnp.int32, (KV_WIDTH, LANES), 1)
    kmean_t = jnp.zeros((KV_WIDTH, LANES), F32)
    for n in range(n_blk):
        tot = jnp.zeros((KV_WIDTH, page), F32)
        for j in range(per_blk):
            pg = n * per_blk + j
            kpage = kp[pg][...]
            kc[:, pg * page:(pg + 1) * page] = kpage.astype(BF16)
            vc[:, pg * page:(pg + 1) * page] = vp[pg][...].astype(BF16)
            tot = tot + kpage
        kmean_t = jnp.where(lane == n, jnp.sum(tot, axis=1, keepdims=True) * (1.0 / MOBA_BLOCK), kmean_t)

    qbd = qbd_ref[0]
    gate = jnp.dot(qbd, kmean_t, precision=HIGHEST, preferred_element_type=F32)
    cnt, bidx = _topk_rank(gate, n_blk, n_blk, 1)
    bias = jnp.where((bidx < n_blk) & (cnt < MOBA_TOPK), 0.0, NEG_INF)

    qs = qbd * HEAD_DIM ** -0.5
    s_all = _dot(qs.astype(BF16), kc[...])
    s_blk = [s_all[:, n * MOBA_BLOCK:(n + 1) * MOBA_BLOCK] + bias[:, n:n + 1] for n in range(n_blk)]

    kn = kn_ref[0]
    vn = vn_ref[0]
    tok = lax.broadcasted_iota(jnp.int32, (n_rows, 1), 0) % seq_new
    s_own = []
    for t in range(seq_new):
        st = jnp.sum(qs * kn[t:t + 1, :], axis=-1, keepdims=True)
        s_own.append(jnp.where(t <= tok, st, NEG_INF))

    m = s_own[0]
    for st in s_own[1:]:
        m = jnp.maximum(m, st)
    for sb in s_blk:
        m = jnp.maximum(m, jnp.max(sb, axis=-1, keepdims=True))

    l = jnp.zeros((n_rows, 1), F32)
    out = jnp.zeros((n_rows, KV_WIDTH), F32)
    for t in range(seq_new):
        pt = jnp.exp(s_own[t] - m)
        l = l + pt
        out = out + pt * vn[t:t + 1, :]
    for n in range(n_blk):
        pn = jnp.exp(s_blk[n] - m)
        l = l + jnp.sum(pn, axis=-1, keepdims=True)
        out = out + _dot_nt(pn.astype(BF16), vc[:, n * MOBA_BLOCK:(n + 1) * MOBA_BLOCK])
    o_ref[0] = out / l


def _moba_sample(qbd, k_new, v_new, cache_kt, cache_vt, page_table, layer):
    n_req, n_pages = page_table.shape
    page = cache_kt.shape[3]
    n_rows = qbd.shape[1]
    seq_new = k_new.shape[1]

    def page_spec(p):
        return pl.BlockSpec((None, None, KV_WIDTH, page), lambda r, pt: (layer, pt[r * n_pages + p], 0, 0))

    in_specs = [
        pl.BlockSpec((1, n_rows, KV_WIDTH), lambda r, pt: (r, 0, 0)),
        pl.BlockSpec((1, seq_new, KV_WIDTH), lambda r, pt: (r, 0, 0)),
        pl.BlockSpec((1, seq_new, KV_WIDTH), lambda r, pt: (r, 0, 0)),
    ]
    in_specs += [page_spec(p) for p in range(n_pages)]
    in_specs += [page_spec(p) for p in range(n_pages)]
    grid_spec = pltpu.PrefetchScalarGridSpec(
        num_scalar_prefetch=1,
        grid=(n_req,),
        in_specs=in_specs,
        out_specs=pl.BlockSpec((1, n_rows, KV_WIDTH), lambda r, pt: (r, 0, 0)),
        scratch_shapes=[
            pltpu.VMEM((KV_WIDTH, n_pages * page), BF16),
            pltpu.VMEM((KV_WIDTH, n_pages * page), BF16),
        ],
    )
    return pl.pallas_call(
        functools.partial(_moba_sample_kernel, n_pages=n_pages, page=page),
        grid_spec=grid_spec,
        out_shape=jax.ShapeDtypeStruct((n_req, n_rows, KV_WIDTH), F32),
        compiler_params=_cparams("arbitrary"),
        name="moba_sample",
    )(page_table.reshape(-1), qbd, k_new, v_new, *([cache_kt] * n_pages), *([cache_vt] * n_pages))


def _hgrn_gates(hq, hf, lb):
    q = _silu(hq)
    f = lb + (1.0 - lb) * _sigmoid(hf)
    k = (1.0 - lb) * _sigmoid(-hf)
    return q, k, f


def _head_rms(o, norm_g):
    head_mean = jnp.where(_block_diag_mask(HG_WIDTH, HG_WIDTH, HG_DK), 1.0 / HG_DK, 0.0)
    ms = jnp.dot(o * o, head_mean, precision=HIGHEST, preferred_element_type=F32)
    return o * lax.rsqrt(ms + EPS) * norm_g


def _short_conv(ext_ref, rows, cw):
    return (cw[0:1, :] * ext_ref[pl.ds(HALO - 2, rows), :]
            + cw[1:2, :] * ext_ref[pl.ds(HALO - 1, rows), :]
            + cw[2:3, :] * ext_ref[pl.ds(HALO, rows), :])


def _hgrn_prompt_kernel(hq_ref, hf_ref, hi_ref, hg_ref, cb_ref, cc_ref, ch_ref, lb_ref, ng_ref, cw_ref,
                        oh_ref, oc_ref, st_ref, tail_ref, st, ext):
    tt = pl.program_id(1)
    rows = hq_ref.shape[0]
    n_chunks = rows // HG_CHUNK
    width = HG_WIDTH
    bd = _block_diag_mask(width, width, HG_DK)

    @pl.when(tt == 0)
    def _():
        ext[pl.ds(0, HALO), :] = jnp.zeros((HALO, CONV_CH), F32)

    @pl.when(tt > 0)
    def _():
        ext[pl.ds(0, HALO), :] = ext[pl.ds(rows, HALO), :]

    u = cc_ref[...] * ch_ref[...]
    ext[pl.ds(HALO, rows), :] = u
    oc_ref[...] = cb_ref[...] * _short_conv(ext, rows, cw_ref[...])
    tail_ref[0] = u[rows - 8:rows, :]

    @pl.when(tt == 0)
    def _():
        st[...] = jnp.zeros(st.shape, F32)

    q, k, f = _hgrn_gates(hq_ref[...], hf_ref[...], lb_ref[...])
    logf = jnp.log(f)
    v = hi_ref[...]
    r_i =lax.broadcasted_iota(jnp.int32, (rows, rows), 0)
    c_i = lax.broadcasted_iota(jnp.int32, (rows, rows), 1)
    tri = jnp.where((r_i // HG_CHUNK == c_i // HG_CHUNK) & (c_i <= r_i), 1.0, 0.0)
    a_all = jnp.dot(tri, logf, precision=HIGHEST, preferred_element_type=F32)

    L = HG_CHUNK
    t_i = lax.broadcasted_iota(jnp.int32, (L, width), 0)
    tw = lax.broadcasted_iota(jnp.int32, (L, width), 0)
    sw_col = lax.broadcasted_iota(jnp.int32, (L, width), 1) % L
    head_ones = jnp.where(bd, 1.0, 0.0).astype(BF16)
    outs = []
    for c in range(n_chunks):
        sl = slice(c * L, (c + 1) * L)
        a, qc, kc, vc = a_all[sl], q[sl], k[sl], v[sl]
        a_end = a[L - 1:L, :]
        st_old = st[...]
        o = _dot_nt((qc * jnp.exp(a)).astype(BF16), st_old.astype(BF16))

        sw = jnp.zeros((L, width), F32)
        b = L // 2
        while b >= HG_LEAF:
            nb = L // b
            rq = jnp.concatenate(
                [jnp.broadcast_to(a[j * b - 1:j * b, :], (b, width)) if j % 2 else a[j * b:(j + 1) * b, :]
                 for j in range(nb)], axis=0)
            rk = jnp.concatenate(
                [a[j * b:(j + 1) * b, :] if j % 2 else jnp.broadcast_to(a[(j + 1) * b - 1:(j + 1) * b, :], (b, width))
                 for j in range(nb)], axis=0)
            odd = (t_i // b) % 2 == 1
            ql = jnp.where(odd, qc * jnp.exp(a - rq), 0.0).astype(BF16)
            kl = jnp.where(odd, 0.0, kc * jnp.exp(rk - a))
            k_bd = jnp.where(bd, jnp.concatenate([kl] * HG_HEADS, axis=0), 0.0).astype(BF16)
            sc = _dot_nt(ql, k_bd)
            pair = ((tw // b) % 2 == 1) & (sw_col // b == tw // b - 1)
            sw = sw + jnp.where(pair, sc, 0.0)
            b //= 2
        v_bd = jnp.where(bd, jnp.concatenate([vc] * HG_HEADS, axis=0), 0.0).astype(BF16)
        o = o + _dot(sw.astype(BF16), v_bd)

        prods = []
        vrs = []
        for delta in range(HG_LEAF):
            if delta:
                kr, ar, vr = (pltpu.roll(x, delta, 0) for x in (kc, a, vc))
            else:
                kr, ar, vr = kc, a, vc
            live = (t_i % HG_LEAF) >= delta
            prods.append(jnp.where(live, qc * kr * jnp.exp(jnp.where(live, a - ar, 0.0)), 0.0))
            vrs.append(vr)
        sc = _dot(jnp.concatenate(prods, axis=0).astype(BF16), head_ones)
        for delta in range(HG_LEAF):
            o = o + sc[delta * L:(delta + 1) * L] * vrs[delta]
        outs.append(o)

        ke = (kc * jnp.exp(a_end - a)).astype(BF16)
        upd = _dot_tn(vc.astype(BF16), ke)
        st[...] = st_old * jnp.exp(a_end) + jnp.where(bd, upd, 0.0)

    o_all = jnp.concatenate(outs, axis=0)
    oh_ref[...] = _head_rms(o_all, ng_ref[...]) * _silu(hg_ref[...])
    st_ref[0] = st[...]


def _hgrn_prompt(h, lb, norm_g, cw, batch, seq):
    rows = 256
    nt = seq // rows

    def col(cb):
        return pl.BlockSpec((rows, COL), lambda b, t: (b * nt + t, cb))

    const = lambda shape: pl.BlockSpec(shape, lambda b, t: (0, 0))
    row_out = pl.BlockSpec((rows, COL), lambda b, t: (b * nt + t, 0))
    return pl.pallas_call(
        _hgrn_prompt_kernel,
        grid=(batch, nt),
        in_specs=[col(CB_HQ), col(CB_HF), col(CB_HI), col(CB_HG), col(CB_CB), col(CB_CC), col(CB_CH),
                  const((1, HG_WIDTH)), const((1, HG_WIDTH)), const((3, CONV_CH))],
        out_specs=[row_out, row_out,
                   pl.BlockSpec((1, HG_WIDTH, HG_WIDTH), lambda b, t: (b, 0, 0)),
                   pl.BlockSpec((1, 8, CONV_CH), lambda b, t: (b, 0, 0))],
        out_shape=[jax.ShapeDtypeStruct((batch * seq, HG_WIDTH), F32),
                   jax.ShapeDtypeStruct((batch * seq, CONV_CH), F32),
                   jax.ShapeDtypeStruct((batch, HG_WIDTH, HG_WIDTH), F32),
                   jax.ShapeDtypeStruct((batch, 8, CONV_CH), F32)],
        scratch_shapes=[pltpu.VMEM((HG_WIDTH, HG_WIDTH), F32),
                        pltpu.VMEM((HALO + rows + HALO, CONV_CH), F32)],
        compiler_params=_cparams("parallel", "arbitrary"),
        name="hgrn_prompt",
    )(h, h, h, h, h, h, h, lb.reshape(1, -1), norm_g.reshape(1, -1), cw)


def _hgrn_sample_kernel(hq_ref, hf_ref, hi_ref, hg_ref, cb_ref, cc_ref, ch_ref, s0_ref, cprev_ref,
                        lb_ref, ng_ref, cw_ref, oh_ref, oc_ref, s_ref, cnew_ref, obuf):
    seq = hq_ref.shape[1]
    width = HG_WIDTH
    bd = _block_diag_mask(width, width, HG_DK)
    head_ones = jnp.where(bd, 1.0, 0.0).astype(BF16)

    cw = cw_ref[...]
    u = cc_ref[0] * ch_ref[0]
    cprev = cprev_ref[0]
    full = [cprev[0:1, :], cprev[1:2, :]] + [u[t:t + 1, :] for t in range(seq)]
    cb = cb_ref[0]
    for t in range(seq):
        y = cw[0:1, :] * full[t] + cw[1:2, :] * full[t + 1] + cw[2:3, :] * full[t + 2]
        oc_ref[0, t:t + 1, :] = cb[t:t + 1, :] * y
    cnew_ref[0, 0:1, :] = full[seq]
    cnew_ref[0, 1:2, :] = full[seq + 1]

    q, k, f = _hgrn_gates(hq_ref[0], hf_ref[0], lb_ref[...])
    f_hi = f.astype(BF16).astype(F32)
    f_lo = f - f_hi
    v = hi_ref[0]
    eye = (lax.broadcasted_iota(jnp.int32, (HG_DK, width), 0)
           == lax.broadcasted_iota(jnp.int32, (HG_DK, width), 1) % HG_DK)

    def expand(row):
        return jnp.where(eye, jnp.broadcast_to(row, (HG_DK, width)), 0.0).astype(BF16)

    stacked = []
    for t in range(seq):
        stacked += [expand(f_hi[t:t + 1, :]), expand(f_lo[t:t + 1, :]),
                    expand(k[t:t + 1, :]), expand(q[t:t + 1, :])]
    ex = _dot(jnp.concatenate(stacked, axis=0), head_ones)

    s = s0_ref[0]
    obuf[...] = jnp.zeros(obuf.shape, F32)
    for t in range(seq):
        base = t * 4 * HG_DK
        f_m = ex[base:base + HG_DK] + ex[base + HG_DK:base + 2 * HG_DK]
        k_m = ex[base + 2 * HG_DK:base + 3 * HG_DK]
        q_m = ex[base + 3 * HG_DK:base + 4 * HG_DK]
        s = f_m * s + k_m * v[t:t + 1, :]
        obuf[t:t + 1, :] = jnp.sum(q_m * s, axis=0, keepdims=True)
    s_ref[0] = s
    o = _head_rms(obuf[...], ng_ref[...])
    oh_ref[0] = o[0:seq, :] * _silu(hg_ref[0])


def _hgrn_sample(h3, s0, cprev, lb, norm_g, cw):
    n_req, seq, _ = h3.shape

    def col(cb):
        return pl.BlockSpec((1, seq, COL), lambda r: (r, 0, cb))

    const = lambda shape: pl.BlockSpec(shape, lambda r: (0, 0))
    row_out = pl.BlockSpec((1, seq, COL), lambda r: (r, 0, 0))
    return pl.pallas_call(
        _hgrn_sample_kernel,
        grid=(n_req,),
        in_specs=[col(CB_HQ), col(CB_HF), col(CB_HI), col(CB_HG), col(CB_CB), col(CB_CC), col(CB_CH),
                  pl.BlockSpec((1, HG_DK, HG_WIDTH), lambda r: (r, 0, 0)),
                  pl.BlockSpec((1, 2, CONV_CH), lambda r: (r, 0, 0)),
                  const((1, HG_WIDTH)), const((1, HG_WIDTH)), const((3, CONV_CH))],
        out_specs=[row_out, row_out,
                   pl.BlockSpec((1, HG_DK, HG_WIDTH), lambda r: (r, 0, 0)),
                   pl.BlockSpec((1, 2, CONV_CH), lambda r: (r, 0, 0))],
        out_shape=[jax.ShapeDtypeStruct((n_req, seq, HG_WIDTH), F32),
                   jax.ShapeDtypeStruct((n_req, seq, CONV_CH), F32),
                   jax.ShapeDtypeStruct((n_req, HG_DK, HG_WIDTH), F32),
                   jax.ShapeDtypeStruct((n_req, 2, CONV_CH), F32)],
        scratch_shapes=[pltpu.VMEM((8, HG_WIDTH), F32)],
        compiler_params=_cparams("parallel"),
        name="hgrn_sample",
    )(h3, h3, h3, h3, h3, h3, h3, s0, cprev, lb.reshape(1, -1), norm_g.reshape(1, -1), cw)


def _xattn_prompt_kernel(q_ref, mk_ref, mv_ref, o_ref):
    q = q_ref[...] * (XA_WIDTH // XA_HEADS) ** -0.5
    mk = mk_ref[...].astype(BF16)
    mv = mv_ref[...].astype(BF16)
    head = lax.broadcasted_iota(jnp.int32, q.shape, 1) // (XA_WIDTH // XA_HEADS)
    out = jnp.zeros(q.shape, F32)
    for hh in range(XA_HEADS):
        mine = head == hh
        s = _dot_nt(jnp.where(mine, q, 0.0).astype(BF16), mk)
        m = jnp.max(s, axis=-1, keepdims=True)
        p = jnp.exp(s - m)
        l = jnp.sum(p, axis=-1, keepdims=True)
        out = jnp.where(mine, _dot(p.astype(BF16), mv) / l, out)
    o_ref[...] = out


def _xattn_prompt(q, mkv, batch, seq):
    tm = min(ROW_TILE, seq)
    nt = seq // tm
    return pl.pallas_call(
        _xattn_prompt_kernel,
        grid=(batch, nt),
        in_specs=[pl.BlockSpec((tm, XA_WIDTH), lambda b, t: (b * nt + t, 0)),
                  pl.BlockSpec((N_MEM, XA_WIDTH), lambda b, t: (b, 0)),
                  pl.BlockSpec((N_MEM, XA_WIDTH), lambda b, t: (b, 1))],
        out_specs=pl.BlockSpec((tm, XA_WIDTH), lambda b, t: (b * nt + t, 0)),
        out_shape=jax.ShapeDtypeStruct((batch * seq, XA_WIDTH), F32),
        compiler_params=_cparams("parallel", "parallel"),
        name="xattn_prompt",
    )(q, mkv, mkv)


def _xattn_sample_kernel(qbd_ref, mk_ref, mv_ref, o_ref):
    q = (qbd_ref[0] * (XA_WIDTH // XA_HEADS) ** -0.5).astype(BF16)
    s = _dot(q, mk_ref[...].astype(BF16))
    m = jnp.max(s, axis=-1, keepdims=True)
    p = jnp.exp(s - m)
    l = jnp.sum(p, axis=-1, keepdims=True)
    o_ref[0] = _dot_nt(p.astype(BF16), mv_ref[...].astype(BF16)) / l


def _xattn_sample(qbd, mem_kt, mem_vt, layer):
    n_req, n_rows, _ = qbd.shape
    mem_spec = pl.BlockSpec((None, None, XA_WIDTH, N_MEM), lambda r: (layer, r, 0, 0))
    return pl.pallas_call(
        _xattn_sample_kernel,
        grid=(n_req,),
        in_specs=[pl.BlockSpec((1, n_rows, XA_WIDTH), lambda r: (r, 0, 0)), mem_spec, mem_spec],
        out_specs=pl.BlockSpec((1, n_rows, XA_WIDTH), lambda r: (r, 0, 0)),
        out_shape=jax.ShapeDtypeStruct((n_req, n_rows, XA_WIDTH), F32),
        compiler_params=_cparams("parallel"),
        name="xattn_sample",
    )(qbd, mem_kt, mem_vt)


def _ffn_prompt_kernel(x_ref, halo_ref, g_ref, wup_ref, cw_ref, wdn_ref, gf_ref, o_ref, u_ref,
                       xn_sc, ext_a, ext_b, hmid, acc, *, seq, final_norm):
    rows = x_ref.shape[0]
    x = x_ref[...]
    g = g_ref[...]
    xn_sc[pl.ds(0, HALO), :] = _rms(halo_ref[...], g).astype(BF16)
    xn_sc[pl.ds(HALO, rows), :] = _rms(x, g).astype(BF16)
    xn = xn_sc[...]

    t_pos = (pl.program_id(0) * rows + lax.broadcasted_iota(jnp.int32, (rows, 1), 0)) % seq
    keep1 = t_pos >= 1
    keep2 = t_pos >= 2

    n_chunks = D_FF // COL

    def up(j):
        for half, ext in ((0, ext_a), (1, ext_b)):
            c0 = half * D_FF + j * COL
            u = _dot(xn, wup_ref[:, c0:c0 + COL])
            u_ref[0, :, c0:c0 + COL] = u[HALO + rows - 8:, :]
            ext[j % 2] = u

    def conv(ext, j, c0):
        cw = cw_ref[:, c0:c0 + COL]
        s1 = jnp.where(keep1, ext[j % 2, pl.ds(HALO - 1, rows), :], 0.0)
        s2 = jnp.where(keep2, ext[j % 2, pl.ds(HALO - 2, rows), :], 0.0)
        return cw[0:1, :] * s2 + cw[1:2, :] * s1 + cw[2:3, :] * ext[j % 2, pl.ds(HALO, rows), :]

    def gate(j):
        hmid[j % 2] = (_silu(conv(ext_a, j, j * COL)) * conv(ext_b, j, D_FF + j * COL)).astype(BF16)

    def down(j):
        part = _dot(hmid[j % 2], wdn_ref[j * COL:(j + 1) * COL, :])
        acc[...] = part if j == 0 else acc[...] + part

    up(0)
    for j in range(n_chunks):
        if j + 1 < n_chunks:
            up(j + 1)
        gate(j)
        if j >= 1:
            down(j - 1)
    down(n_chunks - 1)
    y = x + acc[...]
    if final_norm:
        y = _rms(y, gf_ref[...])
    o_ref[...] = y


def _ffn_sample_kernel(x_ref, p0_ref, p1_ref, g_ref, wup_ref, cw_ref, wdn_ref, gf_ref, o_ref, u2_ref, u3_ref,
                       acc, *, final_norm):
    seq, n_req, d = x_ref.shape
    x = x_ref[...].reshape(seq * n_req, d)
    xn = _rms(x, g_ref[...]).astype(BF16)
    acc[...] = jnp.zeros(acc.shape, F32)

    def conv(u, c0):
        cw = cw_ref[:, pl.ds(c0, COL)]
        full = [p0_ref[:, pl.ds(c0, COL)], p1_ref[:, pl.ds(c0, COL)]]
        full += [u[t * n_req:(t + 1) * n_req, :] for t in range(seq)]
        u2_ref[:, pl.ds(c0, COL)] = full[seq]
        u3_ref[:, pl.ds(c0, COL)] = full[seq + 1]
        return jnp.concatenate(
            [cw[0:1, :] * full[t] + cw[1:2, :] * full[t + 1] + cw[2:3, :] * full[t + 2] for t in range(seq)],
            axis=0)

    def body(j, carry):
        ca = pl.multiple_of(j * COL, COL)
        cb = pl.multiple_of(D_FF + j * COL, COL)
        ya = conv(_dot(xn, wup_ref[:, pl.ds(ca, COL)]), ca)
        yb = conv(_dot(xn, wup_ref[:, pl.ds(cb, COL)]), cb)
        acc[...] += _dot((_silu(ya) * yb).astype(BF16), wdn_ref[pl.ds(ca, COL), :])
        return carry

    lax.fori_loop(0, D_FF // COL, body, 0)
    y = x + acc[...]
    if final_norm:
        y = _rms(y, gf_ref[...])
    o_ref[...] = y.reshape(seq, n_req, d)


def _resident(shape):
    return pl.BlockSpec(shape, lambda i: (0,) * len(shape), pipeline_mode=pl.Buffered(1))


def _ffn_prompt(x, g, w_up, cw, w_dn, g_final, seq, final_norm):
    rows, d = x.shape
    tm = min(ROW_TILE, seq)
    n_up = w_up.shape[1]
    tiles_per_seq = seq // tm
    x_spec = pl.BlockSpec((tm, d), lambda i: (i, 0))
    halo_spec = pl.BlockSpec((HALO, d), lambda i: (jnp.maximum(i * (tm // HALO) - 1, 0), 0))
    return pl.pallas_call(
        functools.partial(_ffn_prompt_kernel, seq=seq, final_norm=final_norm),
        grid=(rows // tm,),
        in_specs=[x_spec, halo_spec, _resident((1, d)), _resident((d, n_up)), _resident((3, n_up)),
                  _resident((D_FF, d)), _resident((1, d))],
        out_specs=[x_spec, pl.BlockSpec((1, 8, n_up), lambda i: (i // tiles_per_seq, 0, 0))],
        out_shape=[jax.ShapeDtypeStruct((rows, d), F32),
                   jax.ShapeDtypeStruct((rows // seq, 8, n_up), F32)],
        scratch_shapes=[pltpu.VMEM((HALO + tm, d), BF16),
                        pltpu.VMEM((2, HALO + tm, COL), F32),
                        pltpu.VMEM((2, HALO + tm, COL), F32),
                        pltpu.VMEM((2, tm, COL), BF16),
                        pltpu.VMEM((tm, d), F32)],
        compiler_params=_cparams("arbitrary"),
        name="ffn_prompt",
    )(x, x, g.reshape(1, d), w_up, cw, w_dn, g_final.reshape(1, d))


def _ffn_sample(x, prev, g, w_up, cw, w_dn, g_final, final_norm):
    seq, n_req, d = x.shape
    n_up = w_up.shape[1]
    full = lambda shape: pl.BlockSpec(shape, lambda i: (0,) * len(shape))
    return pl.pallas_call(
        functools.partial(_ffn_sample_kernel, final_norm=final_norm),
        grid=(1,),
        in_specs=[_resident((seq, n_req, d)), _resident((n_req, n_up)), _resident((n_req, n_up)),
                  _resident((1, d)), _resident((d, n_up)), _resident((3, n_up)), _resident((D_FF, d)),
                  _resident((1, d))],
        out_specs=[full((seq, n_req, d)), full((n_req, n_up)), full((n_req, n_up))],
        out_shape=[jax.ShapeDtypeStruct((seq, n_req, d), F32),
                   jax.ShapeDtypeStruct((n_req, n_up), F32),
                   jax.ShapeDtypeStruct((n_req, n_up), F32)],
        scratch_shapes=[pltpu.VMEM((seq * n_req, d), F32)],
        compiler_params=_cparams("arbitrary"),
        name="ffn_sample",
    )(x, prev[:, 0], prev[:, 1], g.reshape(1, d), w_up, cw, w_dn, g_final.reshape(1, d))


def _head_rows(q, n_heads, group):
    n_req, seq, _ = q.shape
    n_kv = n_heads // group
    qh = q.reshape(n_req, seq, n_heads, HEAD_DIM).transpose(0, 2, 1, 3)
    onehot = jax.nn.one_hot(jnp.arange(n_heads) // group, n_kv, dtype=q.dtype)
    out = qh[:, :, :, None, :] * onehot[None, :, None, :, None]
    return out.reshape(n_req, n_heads * seq, n_kv * HEAD_DIM)


def _head_rows_inverse(o, n_heads, group, seq):
    n_req = o.shape[0]
    n_kv = n_heads // group
    o5 = o.reshape(n_req, n_heads, seq, n_kv, HEAD_DIM)
    picked = jnp.stack([o5[:, hh, :, hh // group, :] for hh in range(n_heads)], axis=2)
    return picked.reshape(n_req * seq, n_heads * HEAD_DIM)


def kernel(x_prompt, x_sample, cache_k, cache_v, cache_mem_k, cache_mem_v, state_hgrn, state_conv,
           state_ffn, page_table, mem_prompt, g_mix, w_in, w_out, hg_lb_logits, hg_norm_g, conv_w,
           g_xattn, g_mem, w_xq, w_mk, w_mv, w_xo, g_ffn, w_up, ffn_conv_w, w_down, g_final):
    B, T, D = x_prompt.shape
    DB, S, _ = x_sample.shape
    depth = w_in.shape[0]
    n_pages = page_table.shape[1]
    page = cache_k.shape[2]
    past = n_pages * page

    lb_p = jax.nn.softmax(hg_lb_logits.astype(F32), axis=0)
    lower = jnp.cumsum(lb_p, axis=0) - lb_p[0:1]

    rope_p = _rope_tables(jnp.arange(T, dtype=jnp.int32))
    rope_s = _rope_tables(jnp.tile(past + jnp.arange(S, dtype=jnp.int32), DB))

    w_in_b, w_out_b, w_xq_b, w_xo_b = (w.astype(BF16) for w in (w_in, w_out, w_xq, w_xo))
    w_up_b, w_down_b = w_up.astype(BF16), w_down.astype(BF16)
    w_mkv_b = jnp.concatenate([w_mk, w_mv], axis=-1).astype(BF16)

    def token_minor(c):
        return c.transpose(0, 1, 3, 4, 2).reshape(c.shape[0], c.shape[1], c.shape[3] * c.shape[4], c.shape[2])

    cache_k4, cache_v4 = token_minor(cache_k), token_minor(cache_v)
    mem_k4, mem_v4 = token_minor(cache_mem_k), token_minor(cache_mem_v)
    mem2 = mem_prompt.reshape(B * N_MEM, D)

    xp = x_prompt.reshape(B * T, D)
    xs = x_sample.reshape(DB * S, D)
    outs = [[] for _ in range(12)]
    for l in range(depth):
        last = l == depth - 1
        h = _norm_proj(xp, g_mix[l], w_in_b[l], rope=rope_p, n_rope=(ATT_WIDTH + KV_WIDTH) // COL)
        k1 = h[:, CB_K * COL:(CB_K + 1) * COL].reshape(B, T, KV_HEADS, HEAD_DIM)
        v1 = h[:, CB_V * COL:(CB_V + 1) * COL].reshape(B, T, KV_HEADS, HEAD_DIM)
        a = _moba_prompt(h, B, T)
        o_h, o_c, st_t, c_tail = _hgrn_prompt(h, lower[l], hg_norm_g[l], conv_w[l], B, T)
        st5 = st_t.reshape(B, HG_HEADS, HG_DK, HG_HEADS, HG_DK)
        h1 = jnp.stack([st5[:, hh, :, hh, :] for hh in range(HG_HEADS)], axis=1).swapaxes(-1, -2)
        c1 = c_tail[:, 6:8, :]
        xp = _resid_proj(xp, [a, o_h, o_c],
                         [w_out_b[l, :ATT_WIDTH], w_out_b[l, ATT_WIDTH:ATT_WIDTH + HG_WIDTH],
                          w_out_b[l, ATT_WIDTH + HG_WIDTH:]])
        mkv = _norm_proj(mem2, g_mem[l], w_mkv_b[l])
        mk = mkv[:, :XA_WIDTH].reshape(B, N_MEM, XA_HEADS, XA_WIDTH // XA_HEADS)
        mv = mkv[:, XA_WIDTH:].reshape(B, N_MEM, XA_HEADS, XA_WIDTH // XA_HEADS)
        qx = _norm_proj(xp, g_xattn[l], w_xq_b[l])
        ox = _xattn_prompt(qx, mkv, B, T)
        xp = _resid_proj(xp, [ox], [w_xo_b[l]])
        xp, u_tail = _ffn_prompt(xp, g_ffn[l], w_up_b[l], ffn_conv_w[l], w_down_b[l], g_final, T, last)
        fs1 = u_tail[:, 6:8, :]

        hs = _norm_proj(xs, g_mix[l], w_in_b[l], rope=rope_s, n_rope=(ATT_WIDTH + KV_WIDTH) // COL)
        hs3 = hs.reshape(DB, S, IN_WIDTH)
        k_new = hs3[:, :, CB_K * COL:(CB_K + 1) * COL]
        v_new = hs3[:, :, CB_V * COL:(CB_V + 1) * COL]
        qbd = _head_rows(hs3[:, :, :ATT_WIDTH], ATT_HEADS, ATT_HEADS // KV_HEADS)
        o_att = _moba_sample(qbd, k_new, v_new, cache_k4, cache_v4, page_table, l)
        a_s = _head_rows_inverse(o_att, ATT_HEADS, ATT_HEADS // KV_HEADS, S)
        s0 = state_hgrn[l].transpose(0, 2, 1, 3).reshape(DB, HG_DK, HG_WIDTH)
        oh_s, oc_s, s_new, c2 = _hgrn_sample(hs3, s0, state_conv[l], lower[l], hg_norm_g[l], conv_w[l])
        h2 = s_new.reshape(DB, HG_DK, HG_HEADS, HG_DK).transpose(0, 2, 1, 3)
        xs = _resid_proj(xs, [a_s, oh_s.reshape(DB * S, HG_WIDTH), oc_s.reshape(DB * S, CONV_CH)],
                         [w_out_b[l, :ATT_WIDTH], w_out_b[l, ATT_WIDTH:ATT_WIDTH + HG_WIDTH],
                          w_out_b[l, ATT_WIDTH + HG_WIDTH:]])
        qxs = _norm_proj(xs, g_xattn[l], w_xq_b[l])
        qxbd = _head_rows(qxs.reshape(DB, S, XA_WIDTH), XA_HEADS, 1)
        oxs = _xattn_sample(qxbd, mem_k4, mem_v4, l)
        xs = _resid_proj(xs, [_head_rows_inverse(oxs, XA_HEADS, 1, S)], [w_xo_b[l]])
        xs_t, u_a, u_b = _ffn_sample(xs.reshape(DB, S, D).swapaxes(0, 1), state_ffn[l], g_ffn[l], w_up_b[l],
                                     ffn_conv_w[l], w_down_b[l], g_final, last)
        xs = xs_t.swapaxes(0, 1).reshape(DB * S, D)
        fs2 = jnp.stack([u_a, u_b], axis=1)

        for lst, val in zip(outs, (k1, v1, k_new.reshape(DB, S, KV_HEADS, HEAD_DIM),
                                   v_new.reshape(DB, S, KV_HEADS, HEAD_DIM), h1, h2, c1, c2, fs1, fs2, mk, mv)):
            lst.append(val)

    return (xp.reshape(B, T, D), xs.reshape(DB, S, D)) + tuple(jnp.stack(o) for o in outs)
```

```python
import functools

import numpy as np
import jax
import jax.numpy as jnp
from jax import lax
from jax.experimental import pallas as pl
from jax.experimental.pallas import tpu as pltpu

F32 = jnp.float32
BF16 = jnp.bfloat16
HIGHEST = lax.Precision.HIGHEST
NEG_INF = float("-inf")
LOG2_E = 1.4426950408889634

D_MODEL = 1024
HEAD_DIM = 64
ATT_HEADS = 8
KV_HEADS = 4
ATT_WIDTH = ATT_HEADS * HEAD_DIM
KV_WIDTH = KV_HEADS * HEAD_DIM
MOBA_BLOCK = 256
MOBA_TOPK = 3
ROPE_THETA = 10000.0
HG_HEADS = 4
HG_DK = 64
HG_WIDTH = HG_HEADS * HG_DK
HG_CHUNK = 64
HG_LEAF = 8
CONV_CH = 256
N_MEM = 256
XA_HEADS = 4
XA_WIDTH = 256
D_FF = 2816
EPS = 1e-6
IN_WIDTH = 2816
COL = 256
LANES = 128
ROW_TILE = 512
HALO = 16
VMEM_LIMIT = 56 * 1024 * 1024

CB_K, CB_V, CB_HQ, CB_HF, CB_HI, CB_HG, CB_CB, CB_CC, CB_CH = 2, 3, 4, 5, 6, 7, 8, 9, 10


def _cparams(*sem):
    return pltpu.CompilerParams(dimension_semantics=sem, vmem_limit_bytes=VMEM_LIMIT)


def _rms(x, g):
    ms = jnp.mean(x * x, axis=-1, keepdims=True)
    return x * lax.rsqrt(ms + EPS) * g


def _sigmoid(z):
    return 1.0 / (1.0 + jnp.exp(-z))


def _silu(z):
    return z * _sigmoid(z)


def _dot(a, b):
    return jnp.dot(a, b, preferred_element_type=F32)


def _dot_nt(a, b, precision=None):
    return lax.dot_general(a, b, (((1,), (1,)), ((), ())), precision=precision,
                           preferred_element_type=F32)


def _dot_tn(a, b):
    return lax.dot_general(a, b, (((0,), (0,)), ((), ())), preferred_element_type=F32)


def _block_diag_mask(rows, cols, blk):
    r = lax.broadcasted_iota(jnp.int32, (rows, cols), 0) // blk
    c = lax.broadcasted_iota(jnp.int32, (rows, cols), 1) // blk
    return r == c


def _norm_proj_kernel(x_ref, g_ref, w_ref, cos_ref, sin_ref, o_ref, *, n_rope):
    xn = _rms(x_ref[...], g_ref[...]).astype(BF16)
    n_chunks = o_ref.shape[1] // COL
    if n_rope:
        cos = jnp.concatenate([cos_ref[...]] * (COL // LANES), axis=1)
        sin = jnp.concatenate([sin_ref[...]] * (COL // LANES), axis=1)
        lane = lax.broadcasted_iota(jnp.int32, cos.shape, 1)
        first_half = (lane % HEAD_DIM) < HEAD_DIM // 2
    for c in range(n_chunks):
        y = _dot(xn, w_ref[:, c * COL:(c + 1) * COL])
        if c < n_rope:
            partner = jnp.where(first_half, pltpu.roll(y, COL - HEAD_DIM // 2, 1),
                                pltpu.roll(y, HEAD_DIM // 2, 1))
            y = y * cos + partner * sin
        o_ref[:, c * COL:(c + 1) * COL] = y


def _norm_proj(x, g, w, rope=None, n_rope=0):
    rows, d = x.shape
    n = w.shape[1]
    tm = min(ROW_TILE, rows)
    if rope is None:
        cos = sin = jnp.zeros((tm, LANES), F32)
    else:
        cos, sin = rope
    tab_blocks = cos.shape[0] // tm
    return pl.pallas_call(
        functools.partial(_norm_proj_kernel, n_rope=n_rope),
        grid=(rows // tm,),
        in_specs=[
            pl.BlockSpec((tm, d), lambda i: (i, 0)),
            pl.BlockSpec((1, d), lambda i: (0, 0)),
            pl.BlockSpec((d, n), lambda i: (0, 0)),
            pl.BlockSpec((tm, LANES), lambda i: (i % tab_blocks, 0)),
            pl.BlockSpec((tm, LANES), lambda i: (i % tab_blocks, 0)),
        ],
        out_specs=pl.BlockSpec((tm, n), lambda i: (i, 0)),
        out_shape=jax.ShapeDtypeStruct((rows, n), F32),
        compiler_params=_cparams("parallel"),
        name="norm_proj",
    )(x, g.reshape(1, d), w, cos, sin)


def _rope_tables(pos):
    half = HEAD_DIM // 2
    inv = ROPE_THETA ** (-jnp.arange(half, dtype=F32) / half)
    ang = pos.astype(F32)[:, None] * inv[None, :]
    cos = jnp.cos(ang)
    sin = jnp.sin(ang)
    reps = LANES // HEAD_DIM
    cos_t = jnp.concatenate([cos, cos] * reps, axis=1)
    sin_t = jnp.concatenate([-sin, sin] * reps, axis=1)
    return cos_t, sin_t


def _resid_proj_kernel(*refs, n_in):
    x_ref = refs[0]
    a_refs = refs[1:1 + n_in]
    w_refs = refs[1 + n_in:1 + 2 * n_in]
    o_ref = refs[1 + 2 * n_in]
    acc = x_ref[...]
    for a_ref, w_ref in zip(a_refs, w_refs):
        acc = acc + _dot(a_ref[...].astype(BF16), w_ref[...])
    o_ref[...] = acc


def _resid_proj(x, acts, weights):
    rows, d = x.shape
    tm = min(ROW_TILE, rows)
    n_in = len(acts)
    in_specs = [pl.BlockSpec((tm, d), lambda i: (i, 0))]
    in_specs += [pl.BlockSpec((tm, a.shape[1]), lambda i: (i, 0)) for a in acts]
    in_specs += [pl.BlockSpec(w.shape, lambda i: (0, 0)) for w in weights]
    return pl.pallas_call(
        functools.partial(_resid_proj_kernel, n_in=n_in),
        grid=(rows // tm,),
        in_specs=in_specs,
        out_specs=pl.BlockSpec((tm, d), lambda i: (i, 0)),
        out_shape=jax.ShapeDtypeStruct((rows, d), F32),
        compiler_params=_cparams("parallel"),
        name="resid_proj",
    )(x, *acts, *weights)


def _topk_rank(gate, n_cand, n_valid, axis):
    idx = lax.broadcasted_iota(jnp.int32, gate.shape, axis)
    cnt = jnp.zeros(gate.shape, F32)
    for m in range(n_cand):
        gm = gate[m:m + 1, :] if axis == 0 else gate[:, m:m + 1]
        beats = jnp.where(gm > gate, 1.0, jnp.where(gm == gate, jnp.where(m < idx, 1.0, 0.0), 0.0))
        cnt = cnt + beats * jnp.where(m < n_valid, 1.0, 0.0)
    return cnt, idx


def _moba_prompt_kernel(q_ref, k_ref, v_ref, o_ref, kbf, vt, kmean, sbuf, pbuf, acc):
    i = pl.program_id(2)
    n_blk = k_ref.shape[0] // MOBA_BLOCK
    nq = 4 * MOBA_BLOCK

    @pl.when(i == 0)
    def _():
        k = k_ref[...]
        kbf[...] = k.astype(BF16)
        vt[...] = v_ref[...].T.astype(BF16)
        kmean[...] = jnp.mean(k.reshape(n_blk, MOBA_BLOCK, LANES), axis=1)

    qb = q_ref[...]
    lane = lax.broadcasted_iota(jnp.int32, (MOBA_BLOCK, LANES), 1)
    lo = lane < HEAD_DIM
    t0, t1 = qb[:, :LANES], qb[:, LANES:]
    r0, r1 = pltpu.roll(t0, HEAD_DIM, 1), pltpu.roll(t1, HEAD_DIM, 1)
    zero = jnp.zeros_like(t0)
    qs = jnp.concatenate([jnp.where(lo, t0, zero), jnp.where(lo, r0, zero),
                          jnp.where(lo, zero, r1), jnp.where(lo, zero, t1)], axis=0)
    qst = qs.T

    gate = jnp.dot(kmean[...], qst, precision=HIGHEST, preferred_element_type=F32)
    cnt, bidx = _topk_rank(gate, n_blk, i, 0)
    bias = jnp.where((bidx < i) & (cnt < MOBA_TOPK), 0.0, NEG_INF)

    qsc = (qst * (HEAD_DIM ** -0.5 * LOG2_E)).astype(BF16)
    causal = (lax.broadcasted_iota(jnp.int32, (MOBA_BLOCK, nq), 0)
              <= lax.broadcasted_iota(jnp.int32, (MOBA_BLOCK, nq), 1) % MOBA_BLOCK)

    def attend(own):
        blocks = [slice(n * MOBA_BLOCK, (n + 1) * MOBA_BLOCK) for n in range(own + 1)]
        m = None
        for n, rows in enumerate(blocks):
            s = _dot(kbf[rows, :], qsc)
            s = jnp.where(causal, s, NEG_INF) if n == own else s + bias[n:n + 1, :]
            sbuf[rows, :] = s
            bm = jnp.max(s, axis=0, keepdims=True)
            m = bm if m is None else jnp.maximum(m, bm)
        l = jnp.zeros((1, nq), F32)
        out = jnp.zeros((LANES, nq), F32)
        for rows in blocks:
            p = jnp.exp2(sbuf[rows, :] - m)
            l = l + jnp.sum(p, axis=0, keepdims=True)
            pbuf[rows, :] = p.astype(BF16)
            out = out + _dot(vt[:, rows], pbuf[rows, :])
        acc[...] = out / l

    for own in range(n_blk):
        pl.when(i == own)(functools.partial(attend, own))

    o = acc[...].T
    b = MOBA_BLOCK
    tile0 = jnp.where(lo, o[0:b], pltpu.roll(o[b:2 * b], HEAD_DIM, 1))
    tile1 = jnp.where(lo, pltpu.roll(o[2 * b:3 * b], HEAD_DIM, 1), o[3 * b:4 * b])
    o_ref[...] = jnp.concatenate([tile0, tile1], axis=1)


def _moba_prompt(h, batch, seq):
    n_blk = seq // MOBA_BLOCK
    kcol = (CB_K * COL) // LANES
    vcol = (CB_V * COL) // LANES
    return pl.pallas_call(
        _moba_prompt_kernel,
        grid=(batch, KV_HEADS // 2, n_blk),
        in_specs=[
            pl.BlockSpec((MOBA_BLOCK, COL), lambda b, p, i: (b * n_blk + i, p)),
            pl.BlockSpec((seq, LANES), lambda b, p, i: (b, kcol + p)),
            pl.BlockSpec((seq, LANES), lambda b, p, i: (b, vcol + p)),
        ],
        out_specs=pl.BlockSpec((MOBA_BLOCK, COL), lambda b, p, i: (b * n_blk + i, p)),
        out_shape=jax.ShapeDtypeStruct((batch * seq, ATT_WIDTH), F32),
        scratch_shapes=[
            pltpu.VMEM((seq, LANES), BF16),
            pltpu.VMEM((LANES, seq), BF16),
            pltpu.VMEM((n_blk, LANES), F32),
            pltpu.VMEM((seq, 4 * MOBA_BLOCK), F32),
            pltpu.VMEM((seq, 4 * MOBA_BLOCK), BF16),
            pltpu.VMEM((LANES, 4 * MOBA_BLOCK), F32),
        ],
        compiler_params=_cparams("parallel", "parallel", "arbitrary"),
        name="moba_prompt",
    )(h, h, h)


def _moba_sample_kernel(pt_ref, qbd_ref, kn_ref, vn_ref, *rest, n_pages, page):
    del pt_ref
    kp = rest[:n_pages]
    vp = rest[n_pages:2 * n_pages]
    o_ref = rest[2 * n_pages]
    kc, vc = rest[2 * n_pages + 1:]
    n_blk = n_pages * page // MOBA_BLOCK
    per_blk = MOBA_BLOCK // page
    n_rows = qbd_ref.shape[1]
    seq_new = kn_ref.shape[1]

    for pg in range(n_pages):
        kc[:, pg * page:(pg + 1) * page] = kp[pg][...].astype(BF16)
        vc[:, pg * page:(pg + 1) * page] = vp[pg][...].astype(BF16)

    qbd = qbd_ref[0]
    qs = qbd * HEAD_DIM ** -0.5
    s_all = _dot(qs.astype(BF16), kc[...])

    lane = lax.broadcasted_iota(jnp.int32, (n_rows, LANES), 1)
    gate = jnp.zeros((n_rows, LANES), F32)
    for n in range(n_blk):
        gate = jnp.where(lane == n, jnp.mean(s_all[:, n * MOBA_BLOCK:(n + 1) * MOBA_BLOCK], axis=-1, keepdims=True),
                         gate)
    cnt, bidx = _topk_rank(gate, n_blk, n_blk, 1)
    bias = jnp.where((bidx < n_blk) & (cnt < MOBA_TOPK), 0.0, NEG_INF)
    s_blk = [s_all[:, n * MOBA_BLOCK:(n + 1) * MOBA_BLOCK] + bias[:, n:n + 1] for n in range(n_blk)]

    kn = kn_ref[0]
    vn = vn_ref[0]
    tok = lax.broadcasted_iota(jnp.int32, (n_rows, 1), 0) % seq_new
    s_own = []
    for t in range(seq_new):
        st = jnp.sum(qs * kn[t:t + 1, :], axis=-1, keepdims=True)
        s_own.append(jnp.where(t <= tok, st, NEG_INF))

    m = s_own[0]
    for st in s_own[1:]:
        m = jnp.maximum(m, st)
    for sb in s_blk:
        m = jnp.maximum(m, jnp.max(sb, axis=-1, keepdims=True))

    l = jnp.zeros((n_rows, 1), F32)
    out = jnp.zeros((n_rows, KV_WIDTH), F32)
    for t in range(seq_new):
        pt = jnp.exp(s_own[t] - m)
        l = l + pt
        out = out + pt * vn[t:t + 1, :]
    for n in range(n_blk):
        pn = jnp.exp(s_blk[n] - m)
        l = l + jnp.sum(pn, axis=-1, keepdims=True)
        out = out + _dot_nt(pn.astype(BF16), vc[:, n * MOBA_BLOCK:(n + 1) * MOBA_BLOCK])
    o_ref[0] = out / l


def _moba_sample(qbd, k_new, v_new, cache_kt, cache_vt, page_table, layer):
    n_req, n_pages = page_table.shape
    page = cache_kt.shape[3]
    n_rows = qbd.shape[1]
    seq_new = k_new.shape[1]

    def page_spec(p):
        return pl.BlockSpec((None, None, KV_WIDTH, page), lambda r, pt: (layer, pt[r * n_pages + p], 0, 0))

    in_specs = [
        pl.BlockSpec((1, n_rows, KV_WIDTH), lambda r, pt: (r, 0, 0)),
        pl.BlockSpec((1, seq_new, KV_WIDTH), lambda r, pt: (r, 0, 0)),
        pl.BlockSpec((1, seq_new, KV_WIDTH), lambda r, pt: (r, 0, 0)),
    ]
    in_specs += [page_spec(p) for p in range(n_pages)]
    in_specs += [page_spec(p) for p in range(n_pages)]
    grid_spec = pltpu.PrefetchScalarGridSpec(
        num_scalar_prefetch=1,
        grid=(n_req,),
        in_specs=in_specs,
        out_specs=pl.BlockSpec((1, n_rows, KV_WIDTH), lambda r, pt: (r, 0, 0)),
        scratch_shapes=[
            pltpu.VMEM((KV_WIDTH, n_pages * page), BF16),
            pltpu.VMEM((KV_WIDTH, n_pages * page), BF16),
        ],
    )
    return pl.pallas_call(
        functools.partial(_moba_sample_kernel, n_pages=n_pages, page=page),
        grid_spec=grid_spec,
        out_shape=jax.ShapeDtypeStruct((n_req, n_rows, KV_WIDTH), F32),
        compiler_params=_cparams("arbitrary"),
        name="moba_sample",
    )(page_table.reshape(-1), qbd, k_new, v_new, *([cache_kt] * n_pages), *([cache_vt] * n_pages))


def _hgrn_gates(hq, hf, lb):
    q = _silu(hq)
    f = lb + (1.0 - lb) * _sigmoid(hf)
    k = (1.0 - lb) * _sigmoid(-hf)
    return q, k, f


def _head_rms(o, norm_g):
    head_mean = jnp.where(_block_diag_mask(HG_WIDTH, HG_WIDTH, HG_DK), 1.0 / HG_DK, 0.0)
    ms = jnp.dot(o * o, head_mean, precision=HIGHEST, preferred_element_type=F32)
    return o * lax.rsqrt(ms + EPS) * norm_g


def _short_conv(ext_ref, rows, cw):
    return (cw[0:1, :] * ext_ref[pl.ds(HALO - 2, rows), :]
            + cw[1:2, :] * ext_ref[pl.ds(HALO - 1, rows), :]
            + cw[2:3, :] * ext_ref[pl.ds(HALO, rows), :])


def _hgrn_prompt_kernel(hq_ref, hf_ref, hi_ref, hg_ref, cb_ref, cc_ref, ch_ref, lb_ref, ng_ref, cw_ref,
                        oh_ref, oc_ref, st_ref, tail_ref, st, ext):
    tt = pl.program_id(1)
    rows = hq_ref.shape[0]
    n_chunks = rows // HG_CHUNK
    width = HG_WIDTH
    bd = _block_diag_mask(width, width, HG_DK)

    @pl.when(tt == 0)
    def _():
        ext[pl.ds(0, HALO), :] = jnp.zeros((HALO, CONV_CH), F32)

    @pl.when(tt > 0)
    def _():
        ext[pl.ds(0, HALO), :] = ext[pl.ds(rows, HALO), :]

    u = cc_ref[...] * ch_ref[...]
    ext[pl.ds(HALO, rows), :] = u
    oc_ref[...] = cb_ref[...] * _short_conv(ext, rows, cw_ref[...])
    tail_ref[0] = u[rows - 8:rows, :]

    @pl.when(tt == 0)
    def _():
        st[...] = jnp.zeros(st.shape, F32)

    q, k, f = _hgrn_gates(hq_ref[...], hf_ref[...], lb_ref[...])
    logf = jnp.log(f)
    v = hi_ref[...]
    r_i =lax.broadcasted_iota(jnp.int32, (rows, rows), 0)
    c_i = lax.broadcasted_iota(jnp.int32, (rows, rows), 1)
    tri = jnp.where((r_i // HG_CHUNK == c_i // HG_CHUNK) & (c_i <= r_i), 1.0, 0.0)
    a_all = jnp.dot(tri, logf, precision=HIGHEST, preferred_element_type=F32)

    L = HG_CHUNK
    t_i = lax.broadcasted_iota(jnp.int32, (L, width), 0)
    tw = lax.broadcasted_iota(jnp.int32, (L, width), 0)
    sw_col = lax.broadcasted_iota(jnp.int32, (L, width), 1) % L
    head_ones = jnp.where(bd, 1.0, 0.0).astype(BF16)
    outs = []
    for c in range(n_chunks):
        sl = slice(c * L, (c + 1) * L)
        a, qc, kc, vc = a_all[sl], q[sl], k[sl], v[sl]
        a_end = a[L - 1:L, :]
        st_old = st[...]
        o = _dot_nt((qc * jnp.exp(a)).astype(BF16), st_old.astype(BF16))

        sw = jnp.zeros((L, width), F32)
        b = L // 2
        while b >= HG_LEAF:
            nb = L // b
            rq = jnp.concatenate(
                [jnp.broadcast_to(a[j * b - 1:j * b, :], (b, width)) if j % 2 else a[j * b:(j + 1) * b, :]
                 for j in range(nb)], axis=0)
            rk = jnp.concatenate(
                [a[j * b:(j + 1) * b, :] if j % 2 else jnp.broadcast_to(a[(j + 1) * b - 1:(j + 1) * b, :], (b, width))
                 for j in range(nb)], axis=0)
            odd = (t_i // b) % 2 == 1
            ql = jnp.where(odd, qc * jnp.exp(a - rq), 0.0).astype(BF16)
            kl = jnp.where(odd, 0.0, kc * jnp.exp(rk - a))
            k_bd = jnp.where(bd, jnp.concatenate([kl] * HG_HEADS, axis=0), 0.0).astype(BF16)
            sc = _dot_nt(ql, k_bd)
            pair = ((tw // b) % 2 == 1) & (sw_col // b == tw // b - 1)
            sw = sw + jnp.where(pair, sc, 0.0)
            b //= 2
        v_bd = jnp.where(bd, jnp.concatenate([vc] * HG_HEADS, axis=0), 0.0).astype(BF16)
        o = o + _dot(sw.astype(BF16), v_bd)

        prods = []
        vrs = []
        for delta in range(HG_LEAF):
            if delta:
                kr, ar, vr = (pltpu.roll(x, delta, 0) for x in (kc, a, vc))
            else:
                kr, ar, vr = kc, a, vc
            live = (t_i % HG_LEAF) >= delta
            prods.append(jnp.where(live, qc * kr * jnp.exp(jnp.where(live, a - ar, 0.0)), 0.0))
            vrs.append(vr)
        sc = _dot(jnp.concatenate(prods, axis=0).astype(BF16), head_ones)
        for delta in range(HG_LEAF):
            o = o + sc[delta * L:(delta + 1) * L] * vrs[delta]
        outs.append(o)

        ke = (kc * jnp.exp(a_end - a)).astype(BF16)
        upd = _dot_tn(vc.astype(BF16), ke)
        st[...] = st_old * jnp.exp(a_end) + jnp.where(bd, upd, 0.0)

    o_all = jnp.concatenate(outs, axis=0)
    oh_ref[...] = _head_rms(o_all, ng_ref[...]) * _silu(hg_ref[...])
    st_ref[0] = st[...]


def _hgrn_prompt(h, lb, norm_g, cw, batch, seq):
    rows = 256
    nt = seq // rows

    def col(cb):
        return pl.BlockSpec((rows, COL), lambda b, t: (b * nt + t, cb))

    const = lambda shape: pl.BlockSpec(shape, lambda b, t: (0, 0))
    row_out = pl.BlockSpec((rows, COL), lambda b, t: (b * nt + t, 0))
    return pl.pallas_call(
        _hgrn_prompt_kernel,
        grid=(batch, nt),
        in_specs=[col(CB_HQ), col(CB_HF), col(CB_HI), col(CB_HG), col(CB_CB), col(CB_CC), col(CB_CH),
                  const((1, HG_WIDTH)), const((1, HG_WIDTH)), const((3, CONV_CH))],
        out_specs=[row_out, row_out,
                   pl.BlockSpec((1, HG_WIDTH, HG_WIDTH), lambda b, t: (b, 0, 0)),
                   pl.BlockSpec((1, 8, CONV_CH), lambda b, t: (b, 0, 0))],
        out_shape=[jax.ShapeDtypeStruct((batch * seq, HG_WIDTH), F32),
                   jax.ShapeDtypeStruct((batch * seq, CONV_CH), F32),
                   jax.ShapeDtypeStruct((batch, HG_WIDTH, HG_WIDTH), F32),
                   jax.ShapeDtypeStruct((batch, 8, CONV_CH), F32)],
        scratch_shapes=[pltpu.VMEM((HG_WIDTH, HG_WIDTH), F32),
                        pltpu.VMEM((HALO + rows + HALO, CONV_CH), F32)],
        compiler_params=_cparams("parallel", "arbitrary"),
        name="hgrn_prompt",
    )(h, h, h, h, h, h, h, lb.reshape(1, -1), norm_g.reshape(1, -1), cw)


def _mix_sample_kernel(hq_ref, hf_ref, hi_ref, hg_ref, cb_ref, cc_ref, ch_ref, s0_ref, cprev_ref,
                       lb_ref, ng_ref, cw_ref, oh_ref, oc_ref, s_ref, cnew_ref, fg, kg, qg):
    seq = hq_ref.shape[0]

    full = [cprev_ref[0], cprev_ref[1]] + [cc_ref[t] * ch_ref[t] for t in range(seq)]
    for t in range(seq):
        oc_ref[t] = cb_ref[t] * (cw_ref[0] * full[t] + cw_ref[1] * full[t + 1] + cw_ref[2] * full[t + 2])
    cnew_ref[0] = full[seq]
    cnew_ref[1] = full[seq + 1]

    lb = lb_ref[...]
    for t in range(seq):
        q, k, f = _hgrn_gates(hq_ref[t], hf_ref[t], lb)
        fg[t], kg[t], qg[t] = f, k, q
    v = [hi_ref[t] for t in range(seq)]

    def body(dk, o):
        s = s0_ref[dk]
        row = pl.ds(dk, 1)
        new_o = []
        for t in range(seq):
            s = fg[t, row, :] * s + kg[t, row, :] * v[t]
            new_o.append(o[t] + qg[t, row, :] * s)
        s_ref[dk] = s
        return tuple(new_o)

    o = lax.fori_loop(0, HG_DK, body, tuple(jnp.zeros(v[0].shape, F32) for _ in range(seq)))
    for t in range(seq):
        ms = jnp.mean(o[t] * o[t], axis=0, keepdims=True)
        oh_ref[t] = o[t] * lax.rsqrt(ms + EPS) * ng_ref[...] * _silu(hg_ref[t])


def _mix_sample(act_t, state_t, layer, cprev_t, lb, norm_g, cw):
    seq, _, n_req = act_t.shape
    hd = HG_DK

    def act(k):
        return pl.BlockSpec((seq, hd, n_req), lambda h: (0, k * HG_HEADS + h, 0))

    per_head = lambda lead: pl.BlockSpec((lead, hd, n_req), lambda h: (0, h, 0))
    chan = pl.BlockSpec((hd, n_req), lambda h: (h, 0))
    state_in = pl.BlockSpec((None, None, hd, hd, n_req), lambda h: (layer, h, 0, 0, 0))
    state_out = pl.BlockSpec((None, hd, hd, n_req), lambda h: (h, 0, 0, 0))
    spread = lambda a: jnp.broadcast_to(a[..., None], a.shape + (n_req,))
    return pl.pallas_call(
        _mix_sample_kernel,
        grid=(HG_HEADS,),
        in_specs=[act(k) for k in range(7)] + [state_in, per_head(2), chan, chan, per_head(3)],
        out_specs=[per_head(seq), per_head(seq), state_out, per_head(2)],
        out_shape=[jax.ShapeDtypeStruct((seq, HG_WIDTH, n_req), F32),
                   jax.ShapeDtypeStruct((seq, CONV_CH, n_req), F32),
                   jax.ShapeDtypeStruct((HG_HEADS, hd, hd, n_req), F32),
                   jax.ShapeDtypeStruct((2, CONV_CH, n_req), F32)],
        scratch_shapes=[pltpu.VMEM((seq, hd, n_req), F32)] * 3,
        compiler_params=_cparams("parallel"),
        name="mix_sample",
    )(*([act_t] * 7), state_t, cprev_t, spread(lb), spread(norm_g), spread(cw))


def _xattn_prompt_kernel(q_ref, mk_ref, mv_ref, o_ref):
    q = q_ref[...] * (XA_WIDTH // XA_HEADS) ** -0.5
    mk = mk_ref[...].astype(BF16)
    mv = mv_ref[...].astype(BF16)
    head = lax.broadcasted_iota(jnp.int32, q.shape, 1) // (XA_WIDTH // XA_HEADS)
    out = jnp.zeros(q.shape, F32)
    for hh in range(XA_HEADS):
        mine = head == hh
        s = _dot_nt(jnp.where(mine, q, 0.0).astype(BF16), mk)
        m = jnp.max(s, axis=-1, keepdims=True)
        p = jnp.exp(s - m)
        l = jnp.sum(p, axis=-1, keepdims=True)
        out = jnp.where(mine, _dot(p.astype(BF16), mv) / l, out)
    o_ref[...] = out


def _xattn_prompt(q, mkv, batch, seq):
    tm = min(ROW_TILE, seq)
    nt = seq // tm
    return pl.pallas_call(
        _xattn_prompt_kernel,
        grid=(batch, nt),
        in_specs=[pl.BlockSpec((tm, XA_WIDTH), lambda b, t: (b * nt + t, 0)),
                  pl.BlockSpec((N_MEM, XA_WIDTH), lambda b, t: (b, 0)),
                  pl.BlockSpec((N_MEM, XA_WIDTH), lambda b, t: (b, 1))],
        out_specs=pl.BlockSpec((tm, XA_WIDTH), lambda b, t: (b * nt + t, 0)),
        out_shape=jax.ShapeDtypeStruct((batch * seq, XA_WIDTH), F32),
        compiler_params=_cparams("parallel", "parallel"),
        name="xattn_prompt",
    )(q, mkv, mkv)


def _xattn_sample_kernel(qbd_ref, mk_ref, mv_ref, o_ref):
    q = (qbd_ref[0] * (XA_WIDTH // XA_HEADS) ** -0.5).astype(BF16)
    s = _dot(q, mk_ref[...].astype(BF16))
    m = jnp.max(s, axis=-1, keepdims=True)
    p = jnp.exp(s - m)
    l = jnp.sum(p, axis=-1, keepdims=True)
    o_ref[0] = _dot_nt(p.astype(BF16), mv_ref[...].astype(BF16)) / l


def _xattn_sample(qbd, mem_kt, mem_vt, layer):
    n_req, n_rows, _ = qbd.shape
    mem_spec = pl.BlockSpec((None, None, XA_WIDTH, N_MEM), lambda r: (layer, r, 0, 0))
    return pl.pallas_call(
        _xattn_sample_kernel,
        grid=(n_req,),
        in_specs=[pl.BlockSpec((1, n_rows, XA_WIDTH), lambda r: (r, 0, 0)), mem_spec, mem_spec],
        out_specs=pl.BlockSpec((1, n_rows, XA_WIDTH), lambda r: (r, 0, 0)),
        out_shape=jax.ShapeDtypeStruct((n_req, n_rows, XA_WIDTH), F32),
        compiler_params=_cparams("parallel"),
        name="xattn_sample",
    )(qbd, mem_kt, mem_vt)


def _ffn_prompt_kernel(x_ref, halo_ref, g_ref, wup_ref, cw_ref, wdn_ref, gf_ref, o_ref, u_ref,
                       xn_sc, ext_a, ext_b, hmid, acc, *, seq, final_norm):
    rows = x_ref.shape[0]
    x = x_ref[...]
    g = g_ref[...]
    xn_sc[pl.ds(0, HALO), :] = _rms(halo_ref[...], g).astype(BF16)
    xn_sc[pl.ds(HALO, rows), :] = _rms(x, g).astype(BF16)
    xn = xn_sc[...]

    t_pos = (pl.program_id(0) * rows + lax.broadcasted_iota(jnp.int32, (rows, 1), 0)) % seq
    keep1 = t_pos >= 1
    keep2 = t_pos >= 2

    n_chunks = D_FF // COL

    def up(j):
        for half, ext in ((0, ext_a), (1, ext_b)):
            c0 = half * D_FF + j * COL
            u = _dot(xn, wup_ref[:, c0:c0 + COL])
            u_ref[0, :, c0:c0 + COL] = u[HALO + rows - 8:, :]
            ext[j % 2] = u

    def conv(ext, j, c0):
        cw = cw_ref[:, c0:c0 + COL]
        s1 = jnp.where(keep1, ext[j % 2, pl.ds(HALO - 1, rows), :], 0.0)
        s2 = jnp.where(keep2, ext[j % 2, pl.ds(HALO - 2, rows), :], 0.0)
        return cw[0:1, :] * s2 + cw[1:2, :] * s1 + cw[2:3, :] * ext[j % 2, pl.ds(HALO, rows), :]

    def gate(j):
        hmid[j % 2] = (_silu(conv(ext_a, j, j * COL)) * conv(ext_b, j, D_FF + j * COL)).astype(BF16)

    def down(j):
        part = _dot(hmid[j % 2], wdn_ref[j * COL:(j + 1) * COL, :])
        acc[...] = part if j == 0 else acc[...] + part

    up(0)
    for j in range(n_chunks):
        if j + 1 < n_chunks:
            up(j + 1)
        gate(j)
        if j >= 1:
            down(j - 1)
    down(n_chunks - 1)
    y = x + acc[...]
    if final_norm:
        y = _rms(y, gf_ref[...])
    o_ref[...] = y


def _ffn_sample_kernel(x_ref, p0_ref, p1_ref, g_ref, wup_ref, cw_ref, wdn_ref, gf_ref, o_ref, u2_ref, u3_ref,
                       acc, *, final_norm):
    seq, n_req, d = x_ref.shape
    x = x_ref[...].reshape(seq * n_req, d)
    xn = _rms(x, g_ref[...]).astype(BF16)
    acc[...] = jnp.zeros(acc.shape, F32)

    def conv(u, c0):
        cw = cw_ref[:, pl.ds(c0, COL)]
        full = [p0_ref[:, pl.ds(c0, COL)], p1_ref[:, pl.ds(c0, COL)]]
        full += [u[t * n_req:(t + 1) * n_req, :] for t in range(seq)]
        u2_ref[:, pl.ds(c0, COL)] = full[seq]
        u3_ref[:, pl.ds(c0, COL)] = full[seq + 1]
        return jnp.concatenate(
            [cw[0:1, :] * full[t] + cw[1:2, :] * full[t + 1] + cw[2:3, :] * full[t + 2] for t in range(seq)],
            axis=0)

    def body(j, carry):
        ca = pl.multiple_of(j * COL, COL)
        cb = pl.multiple_of(D_FF + j * COL, COL)
        ya = conv(_dot(xn, wup_ref[:, pl.ds(ca, COL)]), ca)
        yb = conv(_dot(xn, wup_ref[:, pl.ds(cb, COL)]), cb)
        acc[...] += _dot((_silu(ya) * yb).astype(BF16), wdn_ref[pl.ds(ca, COL), :])
        return carry

    lax.fori_loop(0, D_FF // COL, body, 0)
    y = x + acc[...]
    if final_norm:
        y = _rms(y, gf_ref[...])
    o_ref[...] = y.reshape(seq, n_req, d)


def _resident(shape):
    return pl.BlockSpec(shape, lambda i: (0,) * len(shape), pipeline_mode=pl.Buffered(1))


def _ffn_prompt(x, g, w_up, cw, w_dn, g_final, seq, final_norm):
    rows, d = x.shape
    tm = min(ROW_TILE, seq)
    n_up = w_up.shape[1]
    tiles_per_seq = seq // tm
    x_spec = pl.BlockSpec((tm, d), lambda i: (i, 0))
    halo_spec = pl.BlockSpec((HALO, d), lambda i: (jnp.maximum(i * (tm // HALO) - 1, 0), 0))
    return pl.pallas_call(
        functools.partial(_ffn_prompt_kernel, seq=seq, final_norm=final_norm),
        grid=(rows // tm,),
        in_specs=[x_spec, halo_spec, _resident((1, d)), _resident((d, n_up)), _resident((3, n_up)),
                  _resident((D_FF, d)), _resident((1, d))],
        out_specs=[x_spec, pl.BlockSpec((1, 8, n_up), lambda i: (i // tiles_per_seq, 0, 0))],
        out_shape=[jax.ShapeDtypeStruct((rows, d), F32),
                   jax.ShapeDtypeStruct((rows // seq, 8, n_up), F32)],
        scratch_shapes=[pltpu.VMEM((HALO + tm, d), BF16),
                        pltpu.VMEM((2, HALO + tm, COL), F32),
                        pltpu.VMEM((2, HALO + tm, COL), F32),
                        pltpu.VMEM((2, tm, COL), BF16),
                        pltpu.VMEM((tm, d), F32)],
        compiler_params=_cparams("arbitrary"),
        name="ffn_prompt",
    )(x, x, g.reshape(1, d), w_up, cw, w_dn, g_final.reshape(1, d))


def _ffn_sample(x, prev, g, w_up, cw, w_dn, g_final, final_norm):
    seq, n_req, d = x.shape
    n_up = w_up.shape[1]
    full = lambda shape: pl.BlockSpec(shape, lambda i: (0,) * len(shape))
    return pl.pallas_call(
        functools.partial(_ffn_sample_kernel, final_norm=final_norm),
        grid=(1,),
        in_specs=[_resident((seq, n_req, d)), _resident((n_req, n_up)), _resident((n_req, n_up)),
                  _resident((1, d)), _resident((d, n_up)), _resident((3, n_up)), _resident((D_FF, d)),
                  _resident((1, d))],
        out_specs=[full((seq, n_req, d)), full((n_req, n_up)), full((n_req, n_up))],
        out_shape=[jax.ShapeDtypeStruct((seq, n_req, d), F32),
                   jax.ShapeDtypeStruct((n_req, n_up), F32),
                   jax.ShapeDtypeStruct((n_req, n_up), F32)],
        scratch_shapes=[pltpu.VMEM((seq * n_req, d), F32)],
        compiler_params=_cparams("arbitrary"),
        name="ffn_sample",
    )(x, prev[:, 0], prev[:, 1], g.reshape(1, d), w_up, cw, w_dn, g_final.reshape(1, d))


def _head_rows(q, n_heads, group):
    n_req, seq, _ = q.shape
    n_kv = n_heads // group
    qh = q.reshape(n_req, seq, n_heads, HEAD_DIM).transpose(0, 2, 1, 3)
    onehot = jax.nn.one_hot(jnp.arange(n_heads) // group, n_kv, dtype=q.dtype)
    out = qh[:, :, :, None, :] * onehot[None, :, None, :, None]
    return out.reshape(n_req, n_heads * seq, n_kv * HEAD_DIM)


def _head_rows_inverse(o, n_heads, group, seq):
    n_req = o.shape[0]
    n_kv = n_heads // group
    o5 = o.reshape(n_req, n_heads, seq, n_kv, HEAD_DIM)
    picked = jnp.stack([o5[:, hh, :, hh // group, :] for hh in range(n_heads)], axis=2)
    return picked.reshape(n_req * seq, n_heads * HEAD_DIM)


def kernel(x_prompt, x_sample, cache_k, cache_v, cache_mem_k, cache_mem_v, state_hgrn, state_conv,
           state_ffn, page_table, mem_prompt, g_mix, w_in, w_out, hg_lb_logits, hg_norm_g, conv_w,
           g_xattn, g_mem, w_xq, w_mk, w_mv, w_xo, g_ffn, w_up, ffn_conv_w, w_down, g_final):
    B, T, D = x_prompt.shape
    DB, S, _ = x_sample.shape
    depth = w_in.shape[0]
    n_pages = page_table.shape[1]
    page = cache_k.shape[2]
    past = n_pages * page

    lb_p = jax.nn.softmax(hg_lb_logits.astype(F32), axis=0)
    lower = jnp.cumsum(lb_p, axis=0) - lb_p[0:1]

    rope_p = _rope_tables(jnp.arange(T, dtype=jnp.int32))
    rope_s = _rope_tables(jnp.tile(past + jnp.arange(S, dtype=jnp.int32), DB))

    w_in_b, w_out_b, w_xq_b, w_xo_b = (w.astype(BF16) for w in (w_in, w_out, w_xq, w_xo))
    w_up_b, w_down_b = w_up.astype(BF16), w_down.astype(BF16)
    w_mkv_b = jnp.concatenate([w_mk, w_mv], axis=-1).astype(BF16)

    def token_minor(c):
        return c.transpose(0, 1, 3, 4, 2).reshape(c.shape[0], c.shape[1], c.shape[3] * c.shape[4], c.shape[2])

    cache_k4, cache_v4 = token_minor(cache_k), token_minor(cache_v)
    mem_k4, mem_v4 = token_minor(cache_mem_k), token_minor(cache_mem_v)
    state_t = state_hgrn.transpose(0, 2, 3, 4, 1)
    mem2 = mem_prompt.reshape(B * N_MEM, D)

    xp = x_prompt.reshape(B * T, D)
    xs = x_sample.reshape(DB * S, D)
    outs = [[] for _ in range(12)]
    for l in range(depth):
        last = l == depth - 1
        h = _norm_proj(xp, g_mix[l], w_in_b[l], rope=rope_p, n_rope=(ATT_WIDTH + KV_WIDTH) // COL)
        k1 = h[:, CB_K * COL:(CB_K + 1) * COL].reshape(B, T, KV_HEADS, HEAD_DIM)
        v1 = h[:, CB_V * COL:(CB_V + 1) * COL].reshape(B, T, KV_HEADS, HEAD_DIM)
        a = _moba_prompt(h, B, T)
        o_h, o_c, st_t, c_tail = _hgrn_prompt(h, lower[l], hg_norm_g[l], conv_w[l], B, T)
        st5 = st_t.reshape(B, HG_HEADS, HG_DK, HG_HEADS, HG_DK)
        h1 = jnp.stack([st5[:, hh, :, hh, :] for hh in range(HG_HEADS)], axis=1).swapaxes(-1, -2)
        c1 = c_tail[:, 6:8, :]
        xp = _resid_proj(xp, [a, o_h, o_c],
                         [w_out_b[l, :ATT_WIDTH], w_out_b[l, ATT_WIDTH:ATT_WIDTH + HG_WIDTH],
                          w_out_b[l, ATT_WIDTH + HG_WIDTH:]])
        mkv = _norm_proj(mem2, g_mem[l], w_mkv_b[l])
        mk = mkv[:, :XA_WIDTH].reshape(B, N_MEM, XA_HEADS, XA_WIDTH // XA_HEADS)
        mv = mkv[:, XA_WIDTH:].reshape(B, N_MEM, XA_HEADS, XA_WIDTH // XA_HEADS)
        qx = _norm_proj(xp, g_xattn[l], w_xq_b[l])
        ox = _xattn_prompt(qx, mkv, B, T)
        xp = _resid_proj(xp, [ox], [w_xo_b[l]])
        xp, u_tail = _ffn_prompt(xp, g_ffn[l], w_up_b[l], ffn_conv_w[l], w_down_b[l], g_final, T, last)
        fs1 = u_tail[:, 6:8, :]

        hs = _norm_proj(xs, g_mix[l], w_in_b[l], rope=rope_s, n_rope=(ATT_WIDTH + KV_WIDTH) // COL)
        hs3 = hs.reshape(DB, S, IN_WIDTH)
        k_new = hs3[:, :, CB_K * COL:(CB_K + 1) * COL]
        v_new = hs3[:, :, CB_V * COL:(CB_V + 1) * COL]
        qbd = _head_rows(hs3[:, :, :ATT_WIDTH], ATT_HEADS, ATT_HEADS // KV_HEADS)
        o_att = _moba_sample(qbd, k_new, v_new, cache_k4, cache_v4, page_table, l)
        a_s = _head_rows_inverse(o_att, ATT_HEADS, ATT_HEADS // KV_HEADS, S)
        act_t = hs3[:, :, CB_HQ * COL:].transpose(1, 2, 0)
        oh_t, oc_t, h2, c2_t = _mix_sample(act_t, state_t, l, state_conv[l].transpose(1, 2, 0),
                                           lower[l], hg_norm_g[l], conv_w[l])
        oh_s, oc_s, c2 = (a.transpose(2, 0, 1) for a in (oh_t, oc_t, c2_t))
        xs = _resid_proj(xs, [a_s, oh_s.reshape(DB * S, HG_WIDTH), oc_s.reshape(DB * S, CONV_CH)],
                         [w_out_b[l, :ATT_WIDTH], w_out_b[l, ATT_WIDTH:ATT_WIDTH + HG_WIDTH],
                          w_out_b[l, ATT_WIDTH + HG_WIDTH:]])
        qxs = _norm_proj(xs, g_xattn[l], w_xq_b[l])
        qxbd = _head_rows(qxs.reshape(DB, S, XA_WIDTH), XA_HEADS, 1)
        oxs = _xattn_sample(qxbd, mem_k4, mem_v4, l)
        xs = _resid_proj(xs, [_head_rows_inverse(oxs, XA_HEADS, 1, S)], [w_xo_b[l]])
        xs_t, u_a, u_b = _ffn_sample(xs.reshape(DB, S, D).swapaxes(0, 1), state_ffn[l], g_ffn[l], w_up_b[l],
                                     ffn_conv_w[l], w_down_b[l], g_final, last)
        xs = xs_t.swapaxes(0, 1).reshape(DB * S, D)
        fs2 = jnp.stack([u_a, u_b], axis=1)

        for lst, val in zip(outs, (k1, v1, k_new.reshape(DB, S, KV_HEADS, HEAD_DIM),
                                   v_new.reshape(DB, S, KV_HEADS, HEAD_DIM), h1, h2, c1, c2, fs1, fs2, mk, mv)):
            lst.append(val)

    res = [jnp.stack(o) for o in outs]
    res[5] = res[5].transpose(0, 4, 1, 2, 3)
    return (xp.reshape(B, T, D), xs.reshape(DB, S, D)) + tuple(res)
```

```python
import functools
import math

import numpy as np
import jax
import jax.numpy as jnp
from jax import lax
from jax.experimental import pallas as pl
from jax.experimental.pallas import tpu as pltpu

F32 = jnp.float32
BF16 = jnp.bfloat16
HIGHEST = lax.Precision.HIGHEST
NEG_INF = float("-inf")
LOG2_E = 1.4426950408889634

D_MODEL = 1024
HEAD_DIM = 64
ATT_HEADS = 8
KV_HEADS = 4
ATT_WIDTH = ATT_HEADS * HEAD_DIM
KV_WIDTH = KV_HEADS * HEAD_DIM
MOBA_BLOCK = 256
MOBA_TOPK = 3
ROPE_THETA = 10000.0
HG_HEADS = 4
HG_DK = 64
HG_WIDTH = HG_HEADS * HG_DK
HG_CHUNK = 64
HG_LEAF = 8
CONV_CH = 256
N_MEM = 256
XA_HEADS = 4
XA_WIDTH = 256
D_FF = 2816
EPS = 1e-6
IN_WIDTH = 2816
COL = 256
LANES = 128
ROW_TILE = 512
HALO = 16
MOBA_REQ_PER_STEP = 2
XATTN_REQ_PER_STEP = 8
VMEM_LIMIT = 56 * 1024 * 1024

CB_K, CB_V, CB_HQ, CB_HF, CB_HI, CB_HG, CB_CB, CB_CC, CB_CH = 2, 3, 4, 5, 6, 7, 8, 9, 10


def _cparams(*sem):
    return pltpu.CompilerParams(dimension_semantics=sem, vmem_limit_bytes=VMEM_LIMIT)


def _rms(x, g):
    ms = jnp.mean(x * x, axis=-1, keepdims=True)
    return x * lax.rsqrt(ms + EPS) * g


def _sigmoid(z):
    return 1.0 / (1.0 + jnp.exp(-z))


def _silu(z):
    return z * _sigmoid(z)


def _dot(a, b):
    return jnp.dot(a, b, preferred_element_type=F32)


def _dot_nt(a, b, precision=None):
    return lax.dot_general(a, b, (((1,), (1,)), ((), ())), precision=precision,
                           preferred_element_type=F32)


def _dot_tn(a, b):
    return lax.dot_general(a, b, (((0,), (0,)), ((), ())), preferred_element_type=F32)


def _block_diag_mask(rows, cols, blk):
    r = lax.broadcasted_iota(jnp.int32, (rows, cols), 0) // blk
    c = lax.broadcasted_iota(jnp.int32, (rows, cols), 1) // blk
    return r == c


def _norm_proj_kernel(x_ref, g_ref, w_ref, cos_ref, sin_ref, o_ref, *, n_rope):
    xn = _rms(x_ref[...], g_ref[...]).astype(BF16)
    n_chunks = o_ref.shape[1] // COL
    if n_rope:
        cos = jnp.concatenate([cos_ref[...]] * (COL // LANES), axis=1)
        sin = jnp.concatenate([sin_ref[...]] * (COL // LANES), axis=1)
        lane = lax.broadcasted_iota(jnp.int32, cos.shape, 1)
        first_half = (lane % HEAD_DIM) < HEAD_DIM // 2
    for c in range(n_chunks):
        y = _dot(xn, w_ref[:, c * COL:(c + 1) * COL])
        if c < n_rope:
            partner = jnp.where(first_half, pltpu.roll(y, COL - HEAD_DIM // 2, 1),
                                pltpu.roll(y, HEAD_DIM // 2, 1))
            y = y * cos + partner * sin
        o_ref[:, c * COL:(c + 1) * COL] = y


def _norm_proj(x, g, w, rope=None, n_rope=0):
    rows, d = x.shape
    n = w.shape[1]
    tm = min(ROW_TILE, rows)
    if rope is None:
        cos = sin = jnp.zeros((tm, LANES), F32)
    else:
        cos, sin = rope
    tab_blocks = cos.shape[0] // tm
    return pl.pallas_call(
        functools.partial(_norm_proj_kernel, n_rope=n_rope),
        grid=(rows // tm,),
        in_specs=[
            pl.BlockSpec((tm, d), lambda i: (i, 0)),
            pl.BlockSpec((1, d), lambda i: (0, 0)),
            pl.BlockSpec((d, n), lambda i: (0, 0)),
            pl.BlockSpec((tm, LANES), lambda i: (i % tab_blocks, 0)),
            pl.BlockSpec((tm, LANES), lambda i: (i % tab_blocks, 0)),
        ],
        out_specs=pl.BlockSpec((tm, n), lambda i: (i, 0)),
        out_shape=jax.ShapeDtypeStruct((rows, n), F32),
        compiler_params=_cparams("parallel"),
        name="norm_proj",
    )(x, g.reshape(1, d), w, cos, sin)


def _rope_tables(pos):
    half = HEAD_DIM // 2
    inv = ROPE_THETA ** (-jnp.arange(half, dtype=F32) / half)
    ang = pos.astype(F32)[:, None] * inv[None, :]
    cos = jnp.cos(ang)
    sin = jnp.sin(ang)
    reps = LANES // HEAD_DIM
    cos_t = jnp.concatenate([cos, cos] * reps, axis=1)
    sin_t = jnp.concatenate([-sin, sin] * reps, axis=1)
    return cos_t, sin_t


def _resid_proj_kernel(*refs, n_in):
    x_ref = refs[0]
    a_refs = refs[1:1 + n_in]
    w_refs = refs[1 + n_in:1 + 2 * n_in]
    g_ref, wq_ref, o_ref, q_ref = refs[1 + 2 * n_in:]
    acc = x_ref[...]
    for a_ref, w_ref in zip(a_refs, w_refs):
        acc = acc + _dot(a_ref[...].astype(BF16), w_ref[...])
    o_ref[...] = acc
    q_ref[...] = _dot(_rms(acc, g_ref[...]).astype(BF16), wq_ref[...])


def _resid_proj(x, acts, weights, g, wq):
    rows, d = x.shape
    tm = min(ROW_TILE, rows)
    n_in = len(acts)
    nq = wq.shape[1]
    row = lambda n: pl.BlockSpec((tm, n), lambda i: (i, 0))
    in_specs = [row(d)] + [row(a.shape[1]) for a in acts]
    in_specs += [pl.BlockSpec(w.shape, lambda i: (0, 0)) for w in weights]
    in_specs += [pl.BlockSpec((1, d), lambda i: (0, 0)), pl.BlockSpec(wq.shape, lambda i: (0, 0))]
    return pl.pallas_call(
        functools.partial(_resid_proj_kernel, n_in=n_in),
        grid=(rows // tm,),
        in_specs=in_specs,
        out_specs=[row(d), row(nq)],
        out_shape=[jax.ShapeDtypeStruct((rows, d), F32), jax.ShapeDtypeStruct((rows, nq), F32)],
        compiler_params=_cparams("parallel"),
        name="resid_proj",
    )(x, *acts, *weights, g.reshape(1, d), wq)


def _topk_rank(gate, n_cand, n_valid, axis):
    idx = lax.broadcasted_iota(jnp.int32, gate.shape, axis)
    cnt = jnp.zeros(gate.shape, F32)
    for m in range(n_cand):
        gm = gate[m:m + 1, :] if axis == 0 else gate[:, m:m + 1]
        beats = jnp.where(gm > gate, 1.0, jnp.where(gm == gate, jnp.where(m < idx, 1.0, 0.0), 0.0))
        cnt = cnt + beats * jnp.where(m < n_valid, 1.0, 0.0)
    return cnt, idx


def _moba_prompt_kernel(q_ref, k_ref, v_ref, o_ref, kbf, vt, kmean, sbuf, pbuf, acc):
    i = pl.program_id(2)
    n_blk = k_ref.shape[0] // MOBA_BLOCK
    nq = 4 * MOBA_BLOCK

    @pl.when(i == 0)
    def _():
        k = k_ref[...]
        kbf[...] = k.astype(BF16)
        vt[...] = v_ref[...].T.astype(BF16)
        kmean[...] = jnp.mean(k.reshape(n_blk, MOBA_BLOCK, LANES), axis=1)

    qb = q_ref[...]
    lane = lax.broadcasted_iota(jnp.int32, (MOBA_BLOCK, LANES), 1)
    lo = lane < HEAD_DIM
    t0, t1 = qb[:, :LANES], qb[:, LANES:]
    r0, r1 = pltpu.roll(t0, HEAD_DIM, 1), pltpu.roll(t1, HEAD_DIM, 1)
    zero = jnp.zeros_like(t0)
    qs = jnp.concatenate([jnp.where(lo, t0, zero), jnp.where(lo, r0, zero),
                          jnp.where(lo, zero, r1), jnp.where(lo, zero, t1)], axis=0)
    qst = qs.T

    gate = jnp.dot(kmean[...], qst, precision=HIGHEST, preferred_element_type=F32)
    cnt, bidx = _topk_rank(gate, n_blk, i, 0)
    bias = jnp.where((bidx < i) & (cnt < MOBA_TOPK), 0.0, NEG_INF)

    qsc = (qst * (HEAD_DIM ** -0.5 * LOG2_E)).astype(BF16)
    causal = (lax.broadcasted_iota(jnp.int32, (MOBA_BLOCK, nq), 0)
              <= lax.broadcasted_iota(jnp.int32, (MOBA_BLOCK, nq), 1) % MOBA_BLOCK)

    def attend(own):
        blocks = [slice(n * MOBA_BLOCK, (n + 1) * MOBA_BLOCK) for n in range(own + 1)]
        m = None
        for n, rows in enumerate(blocks):
            s = _dot(kbf[rows, :], qsc)
            s = jnp.where(causal, s, NEG_INF) if n == own else s + bias[n:n + 1, :]
            sbuf[rows, :] = s
            bm = jnp.max(s, axis=0, keepdims=True)
            m = bm if m is None else jnp.maximum(m, bm)
        l = jnp.zeros((1, nq), F32)
        out = jnp.zeros((LANES, nq), F32)
        for rows in blocks:
            p = jnp.exp2(sbuf[rows, :] - m)
            l = l + jnp.sum(p, axis=0, keepdims=True)
            pbuf[rows, :] = p.astype(BF16)
            out = out + _dot(vt[:, rows], pbuf[rows, :])
        acc[...] = out / l

    for own in range(n_blk):
        pl.when(i == own)(functools.partial(attend, own))

    o = acc[...].T
    b = MOBA_BLOCK
    tile0 = jnp.where(lo, o[0:b], pltpu.roll(o[b:2 * b], HEAD_DIM, 1))
    tile1 = jnp.where(lo, pltpu.roll(o[2 * b:3 * b], HEAD_DIM, 1), o[3 * b:4 * b])
    o_ref[...] = jnp.concatenate([tile0, tile1], axis=1)


def _moba_prompt(h, batch, seq):
    n_blk = seq // MOBA_BLOCK
    kcol = (CB_K * COL) // LANES
    vcol = (CB_V * COL) // LANES
    return pl.pallas_call(
        _moba_prompt_kernel,
        grid=(batch, KV_HEADS // 2, n_blk),
        in_specs=[
            pl.BlockSpec((MOBA_BLOCK, COL), lambda b, p, i: (b * n_blk + i, p)),
            pl.BlockSpec((seq, LANES), lambda b, p, i: (b, kcol + p)),
            pl.BlockSpec((seq, LANES), lambda b, p, i: (b, vcol + p)),
        ],
        out_specs=pl.BlockSpec((MOBA_BLOCK, COL), lambda b, p, i: (b * n_blk + i, p)),
        out_shape=jax.ShapeDtypeStruct((batch * seq, ATT_WIDTH), F32),
        scratch_shapes=[
            pltpu.VMEM((seq, LANES), BF16),
            pltpu.VMEM((LANES, seq), BF16),
            pltpu.VMEM((n_blk, LANES), F32),
            pltpu.VMEM((seq, 4 * MOBA_BLOCK), F32),
            pltpu.VMEM((seq, 4 * MOBA_BLOCK), BF16),
            pltpu.VMEM((LANES, 4 * MOBA_BLOCK), F32),
        ],
        compiler_params=_cparams("parallel", "parallel", "arbitrary"),
        name="moba_prompt",
    )(h, h, h)


def _moba_sample_kernel(pt_ref, qbd_ref, kn_ref, vn_ref, *rest, n_pages, page):
    del pt_ref
    per_step = qbd_ref.shape[0]
    n_refs = per_step * n_pages
    o_ref = rest[2 * n_refs]
    kc_all, vc_all = rest[2 * n_refs + 1:]
    for g in range(per_step):
        _moba_sample_request(qbd_ref.at[g], kn_ref.at[g], vn_ref.at[g],
                             rest[g * n_pages:(g + 1) * n_pages],
                             rest[n_refs + g * n_pages:n_refs + (g + 1) * n_pages],
                             o_ref.at[g], kc_all.at[g], vc_all.at[g], page)


def _moba_sample_request(qbd_ref, kn_ref, vn_ref, kp, vp, o_ref, kc, vc, page):
    n_pages = len(kp)
    n_blk = n_pages * page // MOBA_BLOCK
    n_rows = qbd_ref.shape[0]
    seq_new = kn_ref.shape[0]

    for pg in range(n_pages):
        kc[:, pg * page:(pg + 1) * page] = kp[pg][...].astype(BF16)
        vc[:, pg * page:(pg + 1) * page] = vp[pg][...].astype(BF16)

    qbd = qbd_ref[...]
    qs = qbd * HEAD_DIM ** -0.5
    s_all = _dot(qs.astype(BF16), kc[...])

    lane = lax.broadcasted_iota(jnp.int32, (n_rows, LANES), 1)
    gate = jnp.zeros((n_rows, LANES), F32)
    for n in range(n_blk):
        gate = jnp.where(lane == n, jnp.mean(s_all[:, n * MOBA_BLOCK:(n + 1) * MOBA_BLOCK], axis=-1, keepdims=True),
                         gate)
    cnt, bidx = _topk_rank(gate, n_blk, n_blk, 1)
    bias = jnp.where((bidx < n_blk) & (cnt < MOBA_TOPK), 0.0, NEG_INF)
    s_blk = [s_all[:, n * MOBA_BLOCK:(n + 1) * MOBA_BLOCK] + bias[:, n:n + 1] for n in range(n_blk)]

    kn = kn_ref[...]
    vn = vn_ref[...]
    tok = lax.broadcasted_iota(jnp.int32, (n_rows, 1), 0) % seq_new
    s_own = []
    for t in range(seq_new):
        st = jnp.sum(qs * kn[t:t + 1, :], axis=-1, keepdims=True)
        s_own.append(jnp.where(t <= tok, st, NEG_INF))

    m = s_own[0]
    for st in s_own[1:]:
        m = jnp.maximum(m, st)
    for sb in s_blk:
        m = jnp.maximum(m, jnp.max(sb, axis=-1, keepdims=True))

    l = jnp.zeros((n_rows, 1), F32)
    out = jnp.zeros((n_rows, KV_WIDTH), F32)
    for t in range(seq_new):
        pt = jnp.exp(s_own[t] - m)
        l = l + pt
        out = out + pt * vn[t:t + 1, :]
    for n in range(n_blk):
        pn = jnp.exp(s_blk[n] - m)
        l = l + jnp.sum(pn, axis=-1, keepdims=True)
        out = out + _dot_nt(pn.astype(BF16), vc[:, n * MOBA_BLOCK:(n + 1) * MOBA_BLOCK])
    o_ref[...] = out / l


def _moba_sample(qbd, k_new, v_new, cache_kt, cache_vt, page_table, layer):
    n_req, n_pages = page_table.shape
    page = cache_kt.shape[3]
    n_rows = qbd.shape[1]
    seq_new = k_new.shape[1]

    per_step = math.gcd(n_req, MOBA_REQ_PER_STEP)
    n_refs = per_step * n_pages

    def page_spec(j):
        return pl.BlockSpec((None, None, KV_WIDTH, page), lambda r, pt: (layer, pt[r * n_refs + j], 0, 0))

    in_specs = [
        pl.BlockSpec((per_step, n_rows, KV_WIDTH), lambda r, pt: (r, 0, 0)),
        pl.BlockSpec((per_step, seq_new, KV_WIDTH), lambda r, pt: (r, 0, 0)),
        pl.BlockSpec((per_step, seq_new, KV_WIDTH), lambda r, pt: (r, 0, 0)),
    ]
    in_specs += [page_spec(j) for j in range(n_refs)]
    in_specs += [page_spec(j) for j in range(n_refs)]
    grid_spec = pltpu.PrefetchScalarGridSpec(
        num_scalar_prefetch=1,
        grid=(n_req // per_step,),
        in_specs=in_specs,
        out_specs=pl.BlockSpec((per_step, n_rows, KV_WIDTH), lambda r, pt: (r, 0, 0)),
        scratch_shapes=[
            pltpu.VMEM((per_step, KV_WIDTH, n_pages * page), BF16),
            pltpu.VMEM((per_step, KV_WIDTH, n_pages * page), BF16),
        ],
    )
    return pl.pallas_call(
        functools.partial(_moba_sample_kernel, n_pages=n_pages, page=page),
        grid_spec=grid_spec,
        out_shape=jax.ShapeDtypeStruct((n_req, n_rows, KV_WIDTH), F32),
        compiler_params=_cparams("arbitrary"),
        name="moba_sample",
    )(page_table.reshape(-1), qbd, k_new, v_new, *([cache_kt] * n_refs), *([cache_vt] * n_refs))


def _hgrn_gates(hq, hf, lb):
    q = _silu(hq)
    f = lb + (1.0 - lb) * _sigmoid(hf)
    k = (1.0 - lb) * _sigmoid(-hf)
    return q, k, f


def _head_rms(o, norm_g):
    head_mean = jnp.where(_block_diag_mask(HG_WIDTH, HG_WIDTH, HG_DK), 1.0 / HG_DK, 0.0)
    ms = jnp.dot(o * o, head_mean, precision=HIGHEST, preferred_element_type=F32)
    return o * lax.rsqrt(ms + EPS) * norm_g


def _short_conv(ext_ref, rows, cw):
    return (cw[0:1, :] * ext_ref[pl.ds(HALO - 2, rows), :]
            + cw[1:2, :] * ext_ref[pl.ds(HALO - 1, rows), :]
            + cw[2:3, :] * ext_ref[pl.ds(HALO, rows), :])


def _hgrn_prompt_kernel(hq_ref, hf_ref, hi_ref, hg_ref, cb_ref, cc_ref, ch_ref, lb_ref, ng_ref, cw_ref,
                        oh_ref, oc_ref, st_ref, tail_ref, st, ext):
    tt = pl.program_id(1)
    rows = hq_ref.shape[0]
    n_chunks = rows // HG_CHUNK
    width = HG_WIDTH
    bd = _block_diag_mask(width, width, HG_DK)

    @pl.when(tt == 0)
    def _():
        ext[pl.ds(0, HALO), :] = jnp.zeros((HALO, CONV_CH), F32)

    @pl.when(tt > 0)
    def _():
        ext[pl.ds(0, HALO), :] = ext[pl.ds(rows, HALO), :]

    u = cc_ref[...] * ch_ref[...]
    ext[pl.ds(HALO, rows), :] = u
    oc_ref[...] = cb_ref[...] * _short_conv(ext, rows, cw_ref[...])
    tail_ref[0] = u[rows - 8:rows, :]

    @pl.when(tt == 0)
    def _():
        st[...] = jnp.zeros(st.shape, F32)

    q, k, f = _hgrn_gates(hq_ref[...], hf_ref[...], lb_ref[...])
    logf = jnp.log(f)
    v = hi_ref[...]
    r_i =lax.broadcasted_iota(jnp.int32, (rows, rows), 0)
    c_i = lax.broadcasted_iota(jnp.int32, (rows, rows), 1)
    tri = jnp.where((r_i // HG_CHUNK == c_i // HG_CHUNK) & (c_i <= r_i), 1.0, 0.0)
    a_all = jnp.dot(tri, logf, precision=HIGHEST, preferred_element_type=F32)

    L = HG_CHUNK
    t_i = lax.broadcasted_iota(jnp.int32, (L, width), 0)
    tw = lax.broadcasted_iota(jnp.int32, (L, width), 0)
    sw_col = lax.broadcasted_iota(jnp.int32, (L, width), 1) % L
    head_ones = jnp.where(bd, 1.0, 0.0).astype(BF16)
    outs = []
    for c in range(n_chunks):
        sl = slice(c * L, (c + 1) * L)
        a, qc, kc, vc = a_all[sl], q[sl], k[sl], v[sl]
        a_end = a[L - 1:L, :]
        st_old = st[...]
        o = _dot_nt((qc * jnp.exp(a)).astype(BF16), st_old.astype(BF16))

        sw = jnp.zeros((L, width), F32)
        b = L // 2
        while b >= HG_LEAF:
            nb = L // b
            rq = jnp.concatenate(
                [jnp.broadcast_to(a[j * b - 1:j * b, :], (b, width)) if j % 2 else a[j * b:(j + 1) * b, :]
                 for j in range(nb)], axis=0)
            rk = jnp.concatenate(
                [a[j * b:(j + 1) * b, :] if j % 2 else jnp.broadcast_to(a[(j + 1) * b - 1:(j + 1) * b, :], (b, width))
                 for j in range(nb)], axis=0)
            odd = (t_i // b) % 2 == 1
            ql = jnp.where(odd, qc * jnp.exp(a - rq), 0.0).astype(BF16)
            kl = jnp.where(odd, 0.0, kc * jnp.exp(rk - a))
            k_bd = jnp.concatenate([kl.astype(BF16)] * HG_HEADS, axis=0) * head_ones
            sc = _dot_nt(ql, k_bd)
            pair = ((tw // b) % 2 == 1) & (sw_col // b == tw // b - 1)
            sw = sw + jnp.where(pair, sc, 0.0)
            b //= 2
        v_bd = jnp.concatenate([vc.astype(BF16)] * HG_HEADS, axis=0) * head_ones
        o = o + _dot(sw.astype(BF16), v_bd)

        prods = []
        vrs = []
        for delta in range(HG_LEAF):
            if delta:
                kr, ar, vr = (pltpu.roll(x.reshape(L // HG_LEAF, HG_LEAF, width), delta, 1).reshape(L, width)
                              for x in (kc, a, vc))
            else:
                kr, ar, vr = kc, a, vc
            live = (t_i % HG_LEAF) >= delta
            prods.append(jnp.where(live, qc * kr * jnp.exp(jnp.where(live, a - ar, 0.0)), 0.0))
            vrs.append(vr)
        sc = _dot(jnp.concatenate(prods, axis=0).astype(BF16), head_ones)
        for delta in range(HG_LEAF):
            o = o + sc[delta * L:(delta + 1) * L] * vrs[delta]
        outs.append(o)

        ke = (kc * jnp.exp(a_end - a)).astype(BF16)
        upd = _dot_tn(vc.astype(BF16), ke)
        st[...] = st_old * jnp.exp(a_end) + jnp.where(bd, upd, 0.0)

    o_all = jnp.concatenate(outs, axis=0)
    oh_ref[...] = _head_rms(o_all, ng_ref[...]) * _silu(hg_ref[...])
    st_ref[0] = st[...]


def _hgrn_prompt(h, lb, norm_g, cw, batch, seq):
    rows = 256
    nt = seq // rows

    def col(cb):
        return pl.BlockSpec((rows, COL), lambda b, t: (b * nt + t, cb))

    const = lambda shape: pl.BlockSpec(shape, lambda b, t: (0, 0))
    row_out = pl.BlockSpec((rows, COL), lambda b, t: (b * nt + t, 0))
    return pl.pallas_call(
        _hgrn_prompt_kernel,
        grid=(batch, nt),
        in_specs=[col(CB_HQ), col(CB_HF), col(CB_HI), col(CB_HG), col(CB_CB), col(CB_CC), col(CB_CH),
                  const((1, HG_WIDTH)), const((1, HG_WIDTH)), const((3, CONV_CH))],
        out_specs=[row_out, row_out,
                   pl.BlockSpec((1, HG_WIDTH, HG_WIDTH), lambda b, t: (b, 0, 0)),
                   pl.BlockSpec((1, 8, CONV_CH), lambda b, t: (b, 0, 0))],
        out_shape=[jax.ShapeDtypeStruct((batch * seq, HG_WIDTH), F32),
                   jax.ShapeDtypeStruct((batch * seq, CONV_CH), F32),
                   jax.ShapeDtypeStruct((batch, HG_WIDTH, HG_WIDTH), F32),
                   jax.ShapeDtypeStruct((batch, 8, CONV_CH), F32)],
        scratch_shapes=[pltpu.VMEM((HG_WIDTH, HG_WIDTH), F32),
                        pltpu.VMEM((HALO + rows + HALO, CONV_CH), F32)],
        compiler_params=_cparams("parallel", "arbitrary"),
        name="hgrn_prompt",
    )(h, h, h, h, h, h, h, lb.reshape(1, -1), norm_g.reshape(1, -1), cw)


def _mix_sample_kernel(hq_ref, hf_ref, hi_ref, hg_ref, cb_ref, cc_ref, ch_ref, s0_ref, cprev_ref,
                       lb_ref, ng_ref, cw_ref, oh_ref, oc_ref, s_ref, cnew_ref, fg, kg, qg):
    seq = hq_ref.shape[0]

    full = [cprev_ref[0], cprev_ref[1]] + [cc_ref[t] * ch_ref[t] for t in range(seq)]
    for t in range(seq):
        oc_ref[t] = cb_ref[t] * (cw_ref[0] * full[t] + cw_ref[1] * full[t + 1] + cw_ref[2] * full[t + 2])
    cnew_ref[0] = full[seq]
    cnew_ref[1] = full[seq + 1]

    lb = lb_ref[...]
    for t in range(seq):
        q, k, f = _hgrn_gates(hq_ref[t], hf_ref[t], lb)
        fg[t], kg[t], qg[t] = f, k, q
    v = [hi_ref[t] for t in range(seq)]

    def body(dk, o):
        s = s0_ref[dk]
        row = pl.ds(dk, 1)
        new_o = []
        for t in range(seq):
            s = fg[t, row, :] * s + kg[t, row, :] * v[t]
            new_o.append(o[t] + qg[t, row, :] * s)
        s_ref[dk] = s
        return tuple(new_o)

    o = lax.fori_loop(0, HG_DK, body, tuple(jnp.zeros(v[0].shape, F32) for _ in range(seq)))
    for t in range(seq):
        ms = jnp.mean(o[t] * o[t], axis=0, keepdims=True)
        oh_ref[t] = o[t] * lax.rsqrt(ms + EPS) * ng_ref[...] * _silu(hg_ref[t])


def _mix_sample(act_t, state_t, layer, cprev_t, lb, norm_g, cw):
    seq, _, n_req = act_t.shape
    hd = HG_DK

    def act(k):
        return pl.BlockSpec((seq, hd, n_req), lambda h: (0, k * HG_HEADS + h, 0))

    per_head = lambda lead: pl.BlockSpec((lead, hd, n_req), lambda h: (0, h, 0))
    chan = pl.BlockSpec((hd, n_req), lambda h: (h, 0))
    state_in = pl.BlockSpec((None, None, hd, hd, n_req), lambda h: (layer, h, 0, 0, 0))
    state_out = pl.BlockSpec((None, hd, hd, n_req), lambda h: (h, 0, 0, 0))
    spread = lambda a: jnp.broadcast_to(a[..., None], a.shape + (n_req,))
    return pl.pallas_call(
        _mix_sample_kernel,
        grid=(HG_HEADS,),
        in_specs=[act(k) for k in range(7)] + [state_in, per_head(2), chan, chan, per_head(3)],
        out_specs=[per_head(seq), per_head(seq), state_out, per_head(2)],
        out_shape=[jax.ShapeDtypeStruct((seq, HG_WIDTH, n_req), F32),
                   jax.ShapeDtypeStruct((seq, CONV_CH, n_req), F32),
                   jax.ShapeDtypeStruct((HG_HEADS, hd, hd, n_req), F32),
                   jax.ShapeDtypeStruct((2, CONV_CH, n_req), F32)],
        scratch_shapes=[pltpu.VMEM((seq, hd, n_req), F32)] * 3,
        compiler_params=_cparams("parallel"),
        name="mix_sample",
    )(*([act_t] * 7), state_t, cprev_t, spread(lb), spread(norm_g), spread(cw))


def _xattn_prompt_kernel(q_ref, mk_ref, mv_ref, o_ref):
    q = q_ref[...] * (XA_WIDTH // XA_HEADS) ** -0.5
    mk = mk_ref[...].astype(BF16)
    mv = mv_ref[...].astype(BF16)
    head = lax.broadcasted_iota(jnp.int32, q.shape, 1) // (XA_WIDTH // XA_HEADS)
    out = jnp.zeros(q.shape, F32)
    for hh in range(XA_HEADS):
        mine = head == hh
        s = _dot_nt(jnp.where(mine, q, 0.0).astype(BF16), mk)
        m = jnp.max(s, axis=-1, keepdims=True)
        p = jnp.exp(s - m)
        l = jnp.sum(p, axis=-1, keepdims=True)
        out = jnp.where(mine, _dot(p.astype(BF16), mv) / l, out)
    o_ref[...] = out


def _xattn_prompt(q, mkv, batch, seq):
    tm = min(ROW_TILE, seq)
    nt = seq // tm
    return pl.pallas_call(
        _xattn_prompt_kernel,
        grid=(batch, nt),
        in_specs=[pl.BlockSpec((tm, XA_WIDTH), lambda b, t: (b * nt + t, 0)),
                  pl.BlockSpec((N_MEM, XA_WIDTH), lambda b, t: (b, 0)),
                  pl.BlockSpec((N_MEM, XA_WIDTH), lambda b, t: (b, 1))],
        out_specs=pl.BlockSpec((tm, XA_WIDTH), lambda b, t: (b * nt + t, 0)),
        out_shape=jax.ShapeDtypeStruct((batch * seq, XA_WIDTH), F32),
        compiler_params=_cparams("parallel", "parallel"),
        name="xattn_prompt",
    )(q, mkv, mkv)


def _xattn_sample_kernel(qbd_ref, mk_ref, mv_ref, o_ref):
    for r in range(qbd_ref.shape[0]):
        q = (qbd_ref[r] * (XA_WIDTH // XA_HEADS) ** -0.5).astype(BF16)
        s = _dot(q, mk_ref[r].astype(BF16))
        m = jnp.max(s, axis=-1, keepdims=True)
        p = jnp.exp(s - m)
        l = jnp.sum(p, axis=-1, keepdims=True)
        o_ref[r] = _dot_nt(p.astype(BF16), mv_ref[r].astype(BF16)) / l


def _xattn_sample(qbd, mem_kt, mem_vt, layer):
    n_req, n_rows, _ = qbd.shape
    per_step = math.gcd(n_req, XATTN_REQ_PER_STEP)
    mem_spec = pl.BlockSpec((None, per_step, XA_WIDTH, N_MEM), lambda r: (layer, r, 0, 0))
    return pl.pallas_call(
        _xattn_sample_kernel,
        grid=(n_req // per_step,),
        in_specs=[pl.BlockSpec((per_step, n_rows, XA_WIDTH), lambda r: (r, 0, 0)), mem_spec, mem_spec],
        out_specs=pl.BlockSpec((per_step, n_rows, XA_WIDTH), lambda r: (r, 0, 0)),
        out_shape=jax.ShapeDtypeStruct((n_req, n_rows, XA_WIDTH), F32),
        compiler_params=_cparams("parallel"),
        name="xattn_sample",
    )(qbd, mem_kt, mem_vt)


def _ffn_prompt_kernel(x_ref, halo_ref, ox_ref, ox_halo_ref, wxo_ref, g_ref, wup_ref, cw_ref, wdn_ref, gf_ref,
                       o_ref, u_ref, xn_sc, ext_a, ext_b, hmid, acc, *, seq, final_norm):
    rows = x_ref.shape[0]
    wxo = wxo_ref[...]
    x = x_ref[...] + _dot(ox_ref[...].astype(BF16), wxo)
    g = g_ref[...]
    x_halo = halo_ref[...] + _dot(ox_halo_ref[...].astype(BF16), wxo)
    xn_sc[pl.ds(0, HALO), :] = _rms(x_halo, g).astype(BF16)
    xn_sc[pl.ds(HALO, rows), :] = _rms(x, g).astype(BF16)
    xn = xn_sc[...]

    t_pos = (pl.program_id(0) * rows + lax.broadcasted_iota(jnp.int32, (rows, 1), 0)) % seq
    keep1 = t_pos >= 1
    keep2 = t_pos >= 2

    n_chunks = D_FF // COL

    def up(j):
        for half, ext in ((0, ext_a), (1, ext_b)):
            c0 = half * D_FF + j * COL
            u = _dot(xn, wup_ref[:, c0:c0 + COL])
            u_ref[0, :, c0:c0 + COL] = u[HALO + rows - 8:, :]
            ext[j % 2] = u

    def conv(ext, j, c0):
        cw = cw_ref[:, c0:c0 + COL]
        s1 = jnp.where(keep1, ext[j % 2, pl.ds(HALO - 1, rows), :], 0.0)
        s2 = jnp.where(keep2, ext[j % 2, pl.ds(HALO - 2, rows), :], 0.0)
        return cw[0:1, :] * s2 + cw[1:2, :] * s1 + cw[2:3, :] * ext[j % 2, pl.ds(HALO, rows), :]

    def gate(j):
        hmid[j % 2] = (_silu(conv(ext_a, j, j * COL)) * conv(ext_b, j, D_FF + j * COL)).astype(BF16)

    def down(j):
        part = _dot(hmid[j % 2], wdn_ref[j * COL:(j + 1) * COL, :])
        acc[...] = part if j == 0 else acc[...] + part

    up(0)
    for j in range(n_chunks):
        if j + 1 < n_chunks:
            up(j + 1)
        gate(j)
        if j >= 1:
            down(j - 1)
    down(n_chunks - 1)
    y = x + acc[...]
    if final_norm:
        y = _rms(y, gf_ref[...])
    o_ref[...] = y


def _ffn_sample_kernel(x_ref, ox_ref, wxo_ref, p0_ref, p1_ref, g_ref, wup_ref, cw_ref, wdn_ref, gf_ref,
                       o_ref, u2_ref, u3_ref, acc, *, final_norm):
    seq, n_req, d = x_ref.shape
    ox = ox_ref[...].reshape(seq * n_req, ox_ref.shape[2])
    x = x_ref[...].reshape(seq * n_req, d) + _dot(ox.astype(BF16), wxo_ref[...])
    xn = _rms(x, g_ref[...]).astype(BF16)
    acc[...] = jnp.zeros(acc.shape, F32)

    def conv(u, c0):
        cw = cw_ref[:, pl.ds(c0, COL)]
        full = [p0_ref[:, pl.ds(c0, COL)], p1_ref[:, pl.ds(c0, COL)]]
        full += [u[t * n_req:(t + 1) * n_req, :] for t in range(seq)]
        u2_ref[:, pl.ds(c0, COL)] = full[seq]
        u3_ref[:, pl.ds(c0, COL)] = full[seq + 1]
        return jnp.concatenate(
            [cw[0:1, :] * full[t] + cw[1:2, :] * full[t + 1] + cw[2:3, :] * full[t + 2] for t in range(seq)],
            axis=0)

    def body(j, carry):
        ca = pl.multiple_of(j * COL, COL)
        cb = pl.multiple_of(D_FF + j * COL, COL)
        ya = conv(_dot(xn, wup_ref[:, pl.ds(ca, COL)]), ca)
        yb = conv(_dot(xn, wup_ref[:, pl.ds(cb, COL)]), cb)
        acc[...] += _dot((_silu(ya) * yb).astype(BF16), wdn_ref[pl.ds(ca, COL), :])
        return carry

    lax.fori_loop(0, D_FF // COL, body, 0)
    y = x + acc[...]
    if final_norm:
        y = _rms(y, gf_ref[...])
    o_ref[...] = y.reshape(seq, n_req, d)


def _resident(shape):
    return pl.BlockSpec(shape, lambda i: (0,) * len(shape), pipeline_mode=pl.Buffered(1))


def _ffn_prompt(x, ox, w_xo, g, w_up, cw, w_dn, g_final, seq, final_norm):
    rows, d = x.shape
    tm = min(ROW_TILE, seq)
    n_up = w_up.shape[1]
    nx = ox.shape[1]
    tiles_per_seq = seq // tm
    x_spec = pl.BlockSpec((tm, d), lambda i: (i, 0))
    halo = lambda n: pl.BlockSpec((HALO, n), lambda i: (jnp.maximum(i * (tm // HALO) - 1, 0), 0))
    return pl.pallas_call(
        functools.partial(_ffn_prompt_kernel, seq=seq, final_norm=final_norm),
        grid=(rows // tm,),
        in_specs=[x_spec, halo(d), pl.BlockSpec((tm, nx), lambda i: (i, 0)), halo(nx), _resident((nx, d)),
                  _resident((1, d)), _resident((d, n_up)), _resident((3, n_up)),
                  _resident((D_FF, d)), _resident((1, d))],
        out_specs=[x_spec, pl.BlockSpec((1, 8, n_up), lambda i: (i // tiles_per_seq, 0, 0))],
        out_shape=[jax.ShapeDtypeStruct((rows, d), F32),
                   jax.ShapeDtypeStruct((rows // seq, 8, n_up), F32)],
        scratch_shapes=[pltpu.VMEM((HALO + tm, d), BF16),
                        pltpu.VMEM((2, HALO + tm, COL), F32),
                        pltpu.VMEM((2, HALO + tm, COL), F32),
                        pltpu.VMEM((2, tm, COL), BF16),
                        pltpu.VMEM((tm, d), F32)],
        compiler_params=_cparams("arbitrary"),
        name="ffn_prompt",
    )(x, x, ox, ox, w_xo, g.reshape(1, d), w_up, cw, w_dn, g_final.reshape(1, d))


def _ffn_sample(x, ox, w_xo, prev, g, w_up, cw, w_dn, g_final, final_norm):
    seq, n_req, d = x.shape
    n_up = w_up.shape[1]
    nx = ox.shape[2]
    full = lambda shape: pl.BlockSpec(shape, lambda i: (0,) * len(shape))
    return pl.pallas_call(
        functools.partial(_ffn_sample_kernel, final_norm=final_norm),
        grid=(1,),
        in_specs=[_resident((seq, n_req, d)), _resident((seq, n_req, nx)), _resident((nx, d)),
                  _resident((n_req, n_up)), _resident((n_req, n_up)),
                  _resident((1, d)), _resident((d, n_up)), _resident((3, n_up)), _resident((D_FF, d)),
                  _resident((1, d))],
        out_specs=[full((seq, n_req, d)), full((n_req, n_up)), full((n_req, n_up))],
        out_shape=[jax.ShapeDtypeStruct((seq, n_req, d), F32),
                   jax.ShapeDtypeStruct((n_req, n_up), F32),
                   jax.ShapeDtypeStruct((n_req, n_up), F32)],
        scratch_shapes=[pltpu.VMEM((seq * n_req, d), F32)],
        compiler_params=_cparams("arbitrary"),
        name="ffn_sample",
    )(x, ox, w_xo, prev[:, 0], prev[:, 1], g.reshape(1, d), w_up, cw, w_dn, g_final.reshape(1, d))


def _head_rows(q, n_heads, group):
    n_req, seq, _ = q.shape
    n_kv = n_heads // group
    qh = q.reshape(n_req, seq, n_heads, HEAD_DIM).transpose(0, 2, 1, 3)
    onehot = jax.nn.one_hot(jnp.arange(n_heads) // group, n_kv, dtype=q.dtype)
    out = qh[:, :, :, None, :] * onehot[None, :, None, :, None]
    return out.reshape(n_req, n_heads * seq, n_kv * HEAD_DIM)


def _head_rows_inverse(o, n_heads, group, seq):
    n_req = o.shape[0]
    n_kv = n_heads // group
    o5 = o.reshape(n_req, n_heads, seq, n_kv, HEAD_DIM)
    picked = jnp.stack([o5[:, hh, :, hh // group, :] for hh in range(n_heads)], axis=2)
    return picked.reshape(n_req * seq, n_heads * HEAD_DIM)


def kernel(x_prompt, x_sample, cache_k, cache_v, cache_mem_k, cache_mem_v, state_hgrn, state_conv,
           state_ffn, page_table, mem_prompt, g_mix, w_in, w_out, hg_lb_logits, hg_norm_g, conv_w,
           g_xattn, g_mem, w_xq, w_mk, w_mv, w_xo, g_ffn, w_up, ffn_conv_w, w_down, g_final):
    B, T, D = x_prompt.shape
    DB, S, _ = x_sample.shape
    depth = w_in.shape[0]
    n_pages = page_table.shape[1]
    page = cache_k.shape[2]
    past = n_pages * page

    lb_p = jax.nn.softmax(hg_lb_logits.astype(F32), axis=0)
    lower = jnp.cumsum(lb_p, axis=0) - lb_p[0:1]

    rope_p = _rope_tables(jnp.arange(T, dtype=jnp.int32))
    rope_s = _rope_tables(jnp.tile(past + jnp.arange(S, dtype=jnp.int32), DB))

    w_in_b, w_out_b, w_xq_b, w_xo_b = (w.astype(BF16) for w in (w_in, w_out, w_xq, w_xo))
    w_up_b, w_down_b = w_up.astype(BF16), w_down.astype(BF16)
    w_mkv_b = jnp.concatenate([w_mk, w_mv], axis=-1).astype(BF16)

    def token_minor(c):
        return c.transpose(0, 1, 3, 4, 2).reshape(c.shape[0], c.shape[1], c.shape[3] * c.shape[4], c.shape[2])

    cache_k4, cache_v4 = token_minor(cache_k), token_minor(cache_v)
    mem_k4, mem_v4 = token_minor(cache_mem_k), token_minor(cache_mem_v)
    state_t = state_hgrn.transpose(0, 2, 3, 4, 1)
    mem2 = mem_prompt.reshape(B * N_MEM, D)

    xp = x_prompt.reshape(B * T, D)
    xs = x_sample.reshape(DB * S, D)
    outs = [[] for _ in range(12)]
    for l in range(depth):
        last = l == depth - 1
        h = _norm_proj(xp, g_mix[l], w_in_b[l], rope=rope_p, n_rope=(ATT_WIDTH + KV_WIDTH) // COL)
        k1 = h[:, CB_K * COL:(CB_K + 1) * COL].reshape(B, T, KV_HEADS, HEAD_DIM)
        v1 = h[:, CB_V * COL:(CB_V + 1) * COL].reshape(B, T, KV_HEADS, HEAD_DIM)
        a = _moba_prompt(h, B, T)
        o_h, o_c, st_t, c_tail = _hgrn_prompt(h, lower[l], hg_norm_g[l], conv_w[l], B, T)
        st5 = st_t.reshape(B, HG_HEADS, HG_DK, HG_HEADS, HG_DK)
        h1 = jnp.stack([st5[:, hh, :, hh, :] for hh in range(HG_HEADS)], axis=1).swapaxes(-1, -2)
        c1 = c_tail[:, 6:8, :]
        w_out_parts = [w_out_b[l, :ATT_WIDTH], w_out_b[l, ATT_WIDTH:ATT_WIDTH + HG_WIDTH],
                       w_out_b[l, ATT_WIDTH + HG_WIDTH:]]
        xp, qx = _resid_proj(xp, [a, o_h, o_c], w_out_parts, g_xattn[l], w_xq_b[l])
        mkv = _norm_proj(mem2, g_mem[l], w_mkv_b[l])
        mk = mkv[:, :XA_WIDTH].reshape(B, N_MEM, XA_HEADS, XA_WIDTH // XA_HEADS)
        mv = mkv[:, XA_WIDTH:].reshape(B, N_MEM, XA_HEADS, XA_WIDTH // XA_HEADS)
        ox = _xattn_prompt(qx, mkv, B, T)
        xp, u_tail = _ffn_prompt(xp, ox, w_xo_b[l], g_ffn[l], w_up_b[l], ffn_conv_w[l], w_down_b[l],
                                 g_final, T, last)
        fs1 = u_tail[:, 6:8, :]

        hs = _norm_proj(xs, g_mix[l], w_in_b[l], rope=rope_s, n_rope=(ATT_WIDTH + KV_WIDTH) // COL)
        hs3 = hs.reshape(DB, S, IN_WIDTH)
        k_new = hs3[:, :, CB_K * COL:(CB_K + 1) * COL]
        v_new = hs3[:, :, CB_V * COL:(CB_V + 1) * COL]
        qbd = _head_rows(hs3[:, :, :ATT_WIDTH], ATT_HEADS, ATT_HEADS // KV_HEADS)
        o_att = _moba_sample(qbd, k_new, v_new, cache_k4, cache_v4, page_table, l)
        a_s = _head_rows_inverse(o_att, ATT_HEADS, ATT_HEADS // KV_HEADS, S)
        act_t = hs3[:, :, CB_HQ * COL:].transpose(1, 2, 0)
        oh_t, oc_t, h2, c2_t = _mix_sample(act_t, state_t, l, state_conv[l].transpose(1, 2, 0),
                                           lower[l], hg_norm_g[l], conv_w[l])
        oh_s, oc_s, c2 = (a.transpose(2, 0, 1) for a in (oh_t, oc_t, c2_t))
        xs, qxs = _resid_proj(xs, [a_s, oh_s.reshape(DB * S, HG_WIDTH), oc_s.reshape(DB * S, CONV_CH)],
                              w_out_parts, g_xattn[l], w_xq_b[l])
        qxbd = _head_rows(qxs.reshape(DB, S, XA_WIDTH), XA_HEADS, 1)
        oxs = _head_rows_inverse(_xattn_sample(qxbd, mem_k4, mem_v4, l), XA_HEADS, 1, S)
        xs_t, u_a, u_b = _ffn_sample(xs.reshape(DB, S, D).swapaxes(0, 1),
                                     oxs.reshape(DB, S, XA_WIDTH).swapaxes(0, 1), w_xo_b[l], state_ffn[l],
                                     g_ffn[l], w_up_b[l], ffn_conv_w[l], w_down_b[l], g_final, last)
        xs = xs_t.swapaxes(0, 1).reshape(DB * S, D)
        fs2 = jnp.stack([u_a, u_b], axis=1)

        for lst, val in zip(outs, (k1, v1, k_new.reshape(DB, S, KV_HEADS, HEAD_DIM),
                                   v_new.reshape(DB, S, KV_HEADS, HEAD_DIM), h1, h2, c1, c2, fs1, fs2, mk, mv)):
            lst.append(val)

    res = [jnp.stack(o) for o in outs]
    res[5] = res[5].transpose(0, 4, 1, 2, 3)
    return (xp.reshape(B, T, D), xs.reshape(DB, S, D)) + tuple(res)
```

```python
import functools
import math

import numpy as np
import jax
import jax.numpy as jnp
from jax import lax
from jax.experimental import pallas as pl
from jax.experimental.pallas import tpu as pltpu

F32 = jnp.float32
BF16 = jnp.bfloat16
HIGHEST = lax.Precision.HIGHEST
NEG_INF = float("-inf")
LOG2_E = 1.4426950408889634

D_MODEL = 1024
HEAD_DIM = 64
ATT_HEADS = 8
KV_HEADS = 4
ATT_WIDTH = ATT_HEADS * HEAD_DIM
KV_WIDTH = KV_HEADS * HEAD_DIM
MOBA_BLOCK = 256
MOBA_TOPK = 3
ROPE_THETA = 10000.0
HG_HEADS = 4
HG_DK = 64
HG_WIDTH = HG_HEADS * HG_DK
HG_CHUNK = 64
HG_LEAF = 8
CONV_CH = 256
N_MEM = 256
XA_HEADS = 4
XA_WIDTH = 256
D_FF = 2816
EPS = 1e-6
IN_WIDTH = 2816
COL = 256
LANES = 128
ROW_TILE = 512
HALO = 16
MOBA_REQ_PER_STEP = 2
XATTN_REQ_PER_STEP = 8
VMEM_LIMIT = 56 * 1024 * 1024

CB_K, CB_V, CB_HQ, CB_HF, CB_HI, CB_HG, CB_CB, CB_CC, CB_CH = 2, 3, 4, 5, 6, 7, 8, 9, 10


def _cparams(*sem):
    return pltpu.CompilerParams(dimension_semantics=sem, vmem_limit_bytes=VMEM_LIMIT)


def _rms(x, g):
    ms = jnp.mean(x * x, axis=-1, keepdims=True)
    return x * lax.rsqrt(ms + EPS) * g


def _sigmoid(z):
    return 1.0 / (1.0 + jnp.exp2(z * -LOG2_E))


def _silu(z):
    return z * _sigmoid(z)


def _dot(a, b):
    return jnp.dot(a, b, preferred_element_type=F32)


def _dot_nt(a, b, precision=None):
    return lax.dot_general(a, b, (((1,), (1,)), ((), ())), precision=precision,
                           preferred_element_type=F32)


def _dot_tn(a, b):
    return lax.dot_general(a, b, (((0,), (0,)), ((), ())), preferred_element_type=F32)


def _block_diag_mask(rows, cols, blk):
    r = lax.broadcasted_iota(jnp.int32, (rows, cols), 0) // blk
    c = lax.broadcasted_iota(jnp.int32, (rows, cols), 1) // blk
    return r == c


def _norm_proj_kernel(x_ref, g_ref, w_ref, cos_ref, sin_ref, o_ref, *, n_rope):
    xn = _rms(x_ref[...], g_ref[...]).astype(BF16)
    n_chunks = o_ref.shape[1] // COL
    if n_rope:
        cos = jnp.concatenate([cos_ref[...]] * (COL // LANES), axis=1)
        sin = jnp.concatenate([sin_ref[...]] * (COL // LANES), axis=1)
        lane = lax.broadcasted_iota(jnp.int32, cos.shape, 1)
        first_half = (lane % HEAD_DIM) < HEAD_DIM // 2
    for c in range(n_chunks):
        y = _dot(xn, w_ref[:, c * COL:(c + 1) * COL])
        if c < n_rope:
            partner = jnp.where(first_half, pltpu.roll(y, COL - HEAD_DIM // 2, 1),
                                pltpu.roll(y, HEAD_DIM // 2, 1))
            y = y * cos + partner * sin
        o_ref[:, c * COL:(c + 1) * COL] = y


def _norm_proj(x, g, w, layer, rope=None, n_rope=0):
    rows, d = x.shape
    n = w.shape[2]
    tm = min(ROW_TILE, rows)
    if rope is None:
        cos = sin = jnp.zeros((tm, LANES), F32)
    else:
        cos, sin = rope
    tab_blocks = cos.shape[0] // tm
    return pl.pallas_call(
        functools.partial(_norm_proj_kernel, n_rope=n_rope),
        grid=(rows // tm,),
        in_specs=[
            pl.BlockSpec((tm, d), lambda i: (i, 0)),
            pl.BlockSpec((1, d), lambda i: (0, 0)),
            pl.BlockSpec((None, d, n), lambda i: (layer, 0, 0)),
            pl.BlockSpec((tm, LANES), lambda i: (i % tab_blocks, 0)),
            pl.BlockSpec((tm, LANES), lambda i: (i % tab_blocks, 0)),
        ],
        out_specs=pl.BlockSpec((tm, n), lambda i: (i, 0)),
        out_shape=jax.ShapeDtypeStruct((rows, n), F32),
        compiler_params=_cparams("parallel"),
        name="norm_proj",
    )(x, g.reshape(1, d), w, cos, sin)


def _rope_tables(pos):
    half = HEAD_DIM // 2
    inv = ROPE_THETA ** (-jnp.arange(half, dtype=F32) / half)
    ang = pos.astype(F32)[:, None] * inv[None, :]
    cos = jnp.cos(ang)
    sin = jnp.sin(ang)
    reps = LANES // HEAD_DIM
    cos_t = jnp.concatenate([cos, cos] * reps, axis=1)
    sin_t = jnp.concatenate([-sin, sin] * reps, axis=1)
    return cos_t, sin_t


def _resid_proj_kernel(*refs, n_in):
    x_ref = refs[0]
    a_refs = refs[1:1 + n_in]
    w_refs = refs[1 + n_in:1 + 2 * n_in]
    g_ref, wq_ref, o_ref, q_ref = refs[1 + 2 * n_in:]
    acc = x_ref[...]
    for a_ref, w_ref in zip(a_refs, w_refs):
        acc = acc + _dot(a_ref[...].astype(BF16), w_ref[...])
    o_ref[...] = acc
    q_ref[...] = _dot(_rms(acc, g_ref[...]).astype(BF16), wq_ref[...])


def _resid_proj(x, acts, weights, g, wq):
    rows, d = x.shape
    tm = min(ROW_TILE, rows)
    n_in = len(acts)
    nq = wq.shape[1]
    row = lambda n: pl.BlockSpec((tm, n), lambda i: (i, 0))
    in_specs = [row(d)] + [row(a.shape[1]) for a in acts]
    in_specs += [pl.BlockSpec(w.shape, lambda i: (0, 0)) for w in weights]
    in_specs += [pl.BlockSpec((1, d), lambda i: (0, 0)), pl.BlockSpec(wq.shape, lambda i: (0, 0))]
    return pl.pallas_call(
        functools.partial(_resid_proj_kernel, n_in=n_in),
        grid=(rows // tm,),
        in_specs=in_specs,
        out_specs=[row(d), row(nq)],
        out_shape=[jax.ShapeDtypeStruct((rows, d), F32), jax.ShapeDtypeStruct((rows, nq), F32)],
        compiler_params=_cparams("parallel"),
        name="resid_proj",
    )(x, *acts, *weights, g.reshape(1, d), wq)


def _topk_rank(gate, n_cand, n_valid, axis):
    idx = lax.broadcasted_iota(jnp.int32, gate.shape, axis)
    cnt = jnp.zeros(gate.shape, F32)
    for m in range(n_cand):
        gm = gate[m:m + 1, :] if axis == 0 else gate[:, m:m + 1]
        beats = jnp.where(gm > gate, 1.0, jnp.where(gm == gate, jnp.where(m < idx, 1.0, 0.0), 0.0))
        cnt = cnt + beats * jnp.where(m < n_valid, 1.0, 0.0)
    return cnt, idx


def _moba_prompt_kernel(q_ref, k_ref, v_ref, o_ref, kbf, vt, sbuf, pbuf, acc):
    i = pl.program_id(2)
    n_blk = k_ref.shape[0] // MOBA_BLOCK
    nq = 4 * MOBA_BLOCK

    @pl.when(i == 0)
    def _():
        kbf[...] = k_ref[...].astype(BF16)
        vt[...] = v_ref[...].T.astype(BF16)

    qb = q_ref[...]
    lane = lax.broadcasted_iota(jnp.int32, (MOBA_BLOCK, LANES), 1)
    lo = lane < HEAD_DIM
    t0, t1 = qb[:, :LANES], qb[:, LANES:]
    r0, r1 = pltpu.roll(t0, HEAD_DIM, 1), pltpu.roll(t1, HEAD_DIM, 1)
    zero = jnp.zeros_like(t0)
    qs = jnp.concatenate([jnp.where(lo, t0, zero), jnp.where(lo, r0, zero),
                          jnp.where(lo, zero, r1), jnp.where(lo, zero, t1)], axis=0)
    qst = qs.T

    qsc = (qst * (HEAD_DIM ** -0.5 * LOG2_E)).astype(BF16)
    causal = (lax.broadcasted_iota(jnp.int32, (MOBA_BLOCK, nq), 0)
              <= lax.broadcasted_iota(jnp.int32, (MOBA_BLOCK, nq), 1) % MOBA_BLOCK)
    bidx = lax.broadcasted_iota(jnp.int32, (n_blk, nq), 0)

    def attend(own):
        blocks = [slice(n * MOBA_BLOCK, (n + 1) * MOBA_BLOCK) for n in range(own + 1)]
        cmax = []
        gate = jnp.zeros((n_blk, nq), F32)
        for n, rows in enumerate(blocks):
            s = _dot(kbf[rows, :], qsc)
            if n == own:
                s = jnp.where(causal, s, NEG_INF)
            else:
                gate = jnp.where(bidx == n, jnp.sum(s, axis=0, keepdims=True), gate)
            sbuf[rows, :] = s
            cmax.append(jnp.max(s, axis=0, keepdims=True))
        if own > MOBA_TOPK:
            cnt, _ = _topk_rank(gate, own, own, 0)
            bias = jnp.where((bidx < own) & (cnt < MOBA_TOPK), 0.0, NEG_INF)
        else:
            bias = jnp.zeros((n_blk, nq), F32)
        m = cmax[own]
        for n in range(own):
            m = jnp.maximum(m, cmax[n] + bias[n:n + 1, :])
        l = jnp.zeros((1, nq), F32)
        out = jnp.zeros((LANES, nq), F32)
        for n, rows in enumerate(blocks):
            shift = m if n == own else m - bias[n:n + 1, :]
            p = jnp.exp2(sbuf[rows, :] - shift)
            l = l + jnp.sum(p, axis=0, keepdims=True)
            pbuf[rows, :] = p.astype(BF16)
            out = out + _dot(vt[:, rows], pbuf[rows, :])
        acc[...] = out / l

    for own in range(n_blk):
        pl.when(i == own)(functools.partial(attend, own))

    o = acc[...].T
    b = MOBA_BLOCK
    tile0 = jnp.where(lo, o[0:b], pltpu.roll(o[b:2 * b], HEAD_DIM, 1))
    tile1 = jnp.where(lo, pltpu.roll(o[2 * b:3 * b], HEAD_DIM, 1), o[3 * b:4 * b])
    o_ref[...] = jnp.concatenate([tile0, tile1], axis=1)


def _moba_prompt(h, batch, seq):
    n_blk = seq // MOBA_BLOCK
    kcol = (CB_K * COL) // LANES
    vcol = (CB_V * COL) // LANES
    return pl.pallas_call(
        _moba_prompt_kernel,
        grid=(batch, KV_HEADS // 2, n_blk),
        in_specs=[
            pl.BlockSpec((MOBA_BLOCK, COL), lambda b, p, i: (b * n_blk + i, p)),
            pl.BlockSpec((seq, LANES), lambda b, p, i: (b, kcol + p)),
            pl.BlockSpec((seq, LANES), lambda b, p, i: (b, vcol + p)),
        ],
        out_specs=pl.BlockSpec((MOBA_BLOCK, COL), lambda b, p, i: (b * n_blk + i, p)),
        out_shape=jax.ShapeDtypeStruct((batch * seq, ATT_WIDTH), F32),
        scratch_shapes=[
            pltpu.VMEM((seq, LANES), BF16),
            pltpu.VMEM((LANES, seq), BF16),
            pltpu.VMEM((seq, 4 * MOBA_BLOCK), F32),
            pltpu.VMEM((seq, 4 * MOBA_BLOCK), BF16),
            pltpu.VMEM((LANES, 4 * MOBA_BLOCK), F32),
        ],
        compiler_params=_cparams("parallel", "parallel", "arbitrary"),
        name="moba_prompt",
    )(h, h, h)


def _moba_sample_kernel(pt_ref, qbd_ref, kn_ref, vn_ref, *rest, n_pages, page):
    del pt_ref
    per_step = qbd_ref.shape[0]
    n_refs = per_step * n_pages
    o_ref = rest[2 * n_refs]
    kc_all, vc_all = rest[2 * n_refs + 1:]
    for g in range(per_step):
        _moba_sample_request(qbd_ref.at[g], kn_ref.at[g], vn_ref.at[g],
                             rest[g * n_pages:(g + 1) * n_pages],
                             rest[n_refs + g * n_pages:n_refs + (g + 1) * n_pages],
                             o_ref.at[g], kc_all.at[g], vc_all.at[g], page)


def _moba_sample_request(qbd_ref, kn_ref, vn_ref, kp, vp, o_ref, kc, vc, page):
    n_pages = len(kp)
    n_blk = n_pages * page // MOBA_BLOCK
    n_rows = qbd_ref.shape[0]
    seq_new = kn_ref.shape[0]

    for pg in range(n_pages):
        kc[:, pg * page:(pg + 1) * page] = kp[pg][...].astype(BF16)
        vc[:, pg * page:(pg + 1) * page] = vp[pg][...].astype(BF16)

    qbd = qbd_ref[...]
    qs = qbd * HEAD_DIM ** -0.5
    s_all = _dot(qs.astype(BF16), kc[...])

    lane = lax.broadcasted_iota(jnp.int32, (n_rows, LANES), 1)
    gate = jnp.zeros((n_rows, LANES), F32)
    for n in range(n_blk):
        gate = jnp.where(lane == n, jnp.mean(s_all[:, n * MOBA_BLOCK:(n + 1) * MOBA_BLOCK], axis=-1, keepdims=True),
                         gate)
    cnt, bidx = _topk_rank(gate, n_blk, n_blk, 1)
    bias = jnp.where((bidx < n_blk) & (cnt < MOBA_TOPK), 0.0, NEG_INF)
    s_blk = [s_all[:, n * MOBA_BLOCK:(n + 1) * MOBA_BLOCK] + bias[:, n:n + 1] for n in range(n_blk)]

    kn = kn_ref[...]
    vn = vn_ref[...]
    tok = lax.broadcasted_iota(jnp.int32, (n_rows, 1), 0) % seq_new
    s_own = []
    for t in range(seq_new):
        st = jnp.sum(qs * kn[t:t + 1, :], axis=-1, keepdims=True)
        s_own.append(jnp.where(t <= tok, st, NEG_INF))

    m = s_own[0]
    for st in s_own[1:]:
        m = jnp.maximum(m, st)
    for sb in s_blk:
        m = jnp.maximum(m, jnp.max(sb, axis=-1, keepdims=True))

    l = jnp.zeros((n_rows, 1), F32)
    out = jnp.zeros((n_rows, KV_WIDTH), F32)
    for t in range(seq_new):
        pt = jnp.exp(s_own[t] - m)
        l = l + pt
        out = out + pt * vn[t:t + 1, :]
    for n in range(n_blk):
        pn = jnp.exp(s_blk[n] - m)
        l = l + jnp.sum(pn, axis=-1, keepdims=True)
        out = out + _dot_nt(pn.astype(BF16), vc[:, n * MOBA_BLOCK:(n + 1) * MOBA_BLOCK])
    o_ref[...] = out / l


def _moba_sample(qbd, k_new, v_new, cache_kt, cache_vt, page_table, layer):
    n_req, n_pages = page_table.shape
    page = cache_kt.shape[3]
    n_rows = qbd.shape[1]
    seq_new = k_new.shape[1]

    per_step = math.gcd(n_req, MOBA_REQ_PER_STEP)
    n_refs = per_step * n_pages

    def page_spec(j):
        return pl.BlockSpec((None, None, KV_WIDTH, page), lambda r, pt: (layer, pt[r * n_refs + j], 0, 0))

    in_specs = [
        pl.BlockSpec((per_step, n_rows, KV_WIDTH), lambda r, pt: (r, 0, 0)),
        pl.BlockSpec((per_step, seq_new, KV_WIDTH), lambda r, pt: (r, 0, 0)),
        pl.BlockSpec((per_step, seq_new, KV_WIDTH), lambda r, pt: (r, 0, 0)),
    ]
    in_specs += [page_spec(j) for j in range(n_refs)]
    in_specs += [page_spec(j) for j in range(n_refs)]
    grid_spec = pltpu.PrefetchScalarGridSpec(
        num_scalar_prefetch=1,
        grid=(n_req // per_step,),
        in_specs=in_specs,
        out_specs=pl.BlockSpec((per_step, n_rows, KV_WIDTH), lambda r, pt: (r, 0, 0)),
        scratch_shapes=[
            pltpu.VMEM((per_step, KV_WIDTH, n_pages * page), BF16),
            pltpu.VMEM((per_step, KV_WIDTH, n_pages * page), BF16),
        ],
    )
    return pl.pallas_call(
        functools.partial(_moba_sample_kernel, n_pages=n_pages, page=page),
        grid_spec=grid_spec,
        out_shape=jax.ShapeDtypeStruct((n_req, n_rows, KV_WIDTH), F32),
        compiler_params=_cparams("arbitrary"),
        name="moba_sample",
    )(page_table.reshape(-1), qbd, k_new, v_new, *([cache_kt] * n_refs), *([cache_vt] * n_refs))


def _hgrn_gates(hq, hf, lb):
    q = _silu(hq)
    f = lb + (1.0 - lb) * _sigmoid(hf)
    k = (1.0 - lb) * _sigmoid(-hf)
    return q, k, f


def _head_rms(o, norm_g):
    head_mean = jnp.where(_block_diag_mask(HG_WIDTH, HG_WIDTH, HG_DK), 1.0 / HG_DK, 0.0)
    ms = jnp.dot(o * o, head_mean, precision=HIGHEST, preferred_element_type=F32)
    return o * lax.rsqrt(ms + EPS) * norm_g


def _short_conv(ext_ref, rows, cw):
    return (cw[0:1, :] * ext_ref[pl.ds(HALO - 2, rows), :]
            + cw[1:2, :] * ext_ref[pl.ds(HALO - 1, rows), :]
            + cw[2:3, :] * ext_ref[pl.ds(HALO, rows), :])


def _hgrn_prompt_kernel(hq_ref, hf_ref, hi_ref, hg_ref, cb_ref, cc_ref, ch_ref, lb_ref, ng_ref, cw_ref,
                        oh_ref, oc_ref, st_ref, tail_ref, st, ext):
    tt = pl.program_id(1)
    rows = hq_ref.shape[0]
    n_chunks = rows // HG_CHUNK
    width = HG_WIDTH
    bd = _block_diag_mask(width, width, HG_DK)

    @pl.when(tt == 0)
    def _():
        ext[pl.ds(0, HALO), :] = jnp.zeros((HALO, CONV_CH), F32)

    @pl.when(tt > 0)
    def _():
        ext[pl.ds(0, HALO), :] = ext[pl.ds(rows, HALO), :]

    u = cc_ref[...] * ch_ref[...]
    ext[pl.ds(HALO, rows), :] = u
    oc_ref[...] = cb_ref[...] * _short_conv(ext, rows, cw_ref[...])
    tail_ref[0] = u[rows - 8:rows, :]

    @pl.when(tt == 0)
    def _():
        st[...] = jnp.zeros(st.shape, F32)

    q, k, f = _hgrn_gates(hq_ref[...], hf_ref[...], lb_ref[...])
    logf = jnp.log(f)
    v = hi_ref[...]
    r_i =lax.broadcasted_iota(jnp.int32, (rows, rows), 0)
    c_i = lax.broadcasted_iota(jnp.int32, (rows, rows), 1)
    tri = jnp.where((r_i // HG_CHUNK == c_i // HG_CHUNK) & (c_i <= r_i), 1.0, 0.0)
    a_all = jnp.dot(tri, logf, precision=HIGHEST, preferred_element_type=F32)

    L = HG_CHUNK
    t_i = lax.broadcasted_iota(jnp.int32, (L, width), 0)
    tw = lax.broadcasted_iota(jnp.int32, (L, width), 0)
    sw_col = lax.broadcasted_iota(jnp.int32, (L, width), 1) % L
    head_ones = jnp.where(bd, 1.0, 0.0).astype(BF16)
    outs = []
    for c in range(n_chunks):
        sl = slice(c * L, (c + 1) * L)
        a, qc, kc, vc = a_all[sl], q[sl], k[sl], v[sl]
        a_end = a[L - 1:L, :]
        st_old = st[...]
        o = _dot_nt((qc * jnp.exp(a)).astype(BF16), st_old.astype(BF16))

        sw = jnp.zeros((L, width), F32)
        b = L // 2
        while b >= HG_LEAF:
            nb = L // b
            rq = jnp.concatenate(
                [jnp.broadcast_to(a[j * b - 1:j * b, :], (b, width)) if j % 2 else a[j * b:(j + 1) * b, :]
                 for j in range(nb)], axis=0)
            rk = jnp.concatenate(
                [a[j * b:(j + 1) * b, :] if j % 2 else jnp.broadcast_to(a[(j + 1) * b - 1:(j + 1) * b, :], (b, width))
                 for j in range(nb)], axis=0)
            odd = (t_i // b) % 2 == 1
            ql = jnp.where(odd, qc * jnp.exp(a - rq), 0.0).astype(BF16)
            kl = jnp.where(odd, 0.0, kc * jnp.exp(rk - a))
            k_bd = jnp.concatenate([kl.astype(BF16)] * HG_HEADS, axis=0) * head_ones
            sc = _dot_nt(ql, k_bd)
            pair = ((tw // b) % 2 == 1) & (sw_col // b == tw // b - 1)
            sw = sw + jnp.where(pair, sc, 0.0)
            b //= 2
        v_bd = jnp.concatenate([vc.astype(BF16)] * HG_HEADS, axis=0) * head_ones
        o = o + _dot(sw.astype(BF16), v_bd)

        prods = []
        vrs = []
        for delta in range(HG_LEAF):
            if delta:
                kr, ar, vr = (pltpu.roll(x.reshape(L // HG_LEAF, HG_LEAF, width), delta, 1).reshape(L, width)
                              for x in (kc, a, vc))
            else:
                kr, ar, vr = kc, a, vc
            live = (t_i % HG_LEAF) >= delta
            prods.append(jnp.where(live, qc * kr * jnp.exp(jnp.where(live, a - ar, 0.0)), 0.0))
            vrs.append(vr)
        sc = _dot(jnp.concatenate(prods, axis=0).astype(BF16), head_ones)
        for delta in range(HG_LEAF):
            o = o + sc[delta * L:(delta + 1) * L] * vrs[delta]
        outs.append(o)

        ke = (kc * jnp.exp(a_end - a)).astype(BF16)
        upd = _dot_tn(vc.astype(BF16), ke)
        st[...] = st_old * jnp.exp(a_end) + jnp.where(bd, upd, 0.0)

    o_all = jnp.concatenate(outs, axis=0)
    oh_ref[...] = _head_rms(o_all, ng_ref[...]) * _silu(hg_ref[...])
    st_ref[0] = st[...]


def _hgrn_prompt(h, lb, norm_g, cw, batch, seq):
    rows = 256
    nt = seq // rows

    def col(cb):
        return pl.BlockSpec((rows, COL), lambda b, t: (b * nt + t, cb))

    const = lambda shape: pl.BlockSpec(shape, lambda b, t: (0, 0))
    row_out = pl.BlockSpec((rows, COL), lambda b, t: (b * nt + t, 0))
    return pl.pallas_call(
        _hgrn_prompt_kernel,
        grid=(batch, nt),
        in_specs=[col(CB_HQ), col(CB_HF), col(CB_HI), col(CB_HG), col(CB_CB), col(CB_CC), col(CB_CH),
                  const((1, HG_WIDTH)), const((1, HG_WIDTH)), const((3, CONV_CH))],
        out_specs=[row_out, row_out,
                   pl.BlockSpec((1, HG_WIDTH, HG_WIDTH), lambda b, t: (b, 0, 0)),
                   pl.BlockSpec((1, 8, CONV_CH), lambda b, t: (b, 0, 0))],
        out_shape=[jax.ShapeDtypeStruct((batch * seq, HG_WIDTH), F32),
                   jax.ShapeDtypeStruct((batch * seq, CONV_CH), F32),
                   jax.ShapeDtypeStruct((batch, HG_WIDTH, HG_WIDTH), F32),
                   jax.ShapeDtypeStruct((batch, 8, CONV_CH), F32)],
        scratch_shapes=[pltpu.VMEM((HG_WIDTH, HG_WIDTH), F32),
                        pltpu.VMEM((HALO + rows + HALO, CONV_CH), F32)],
        compiler_params=_cparams("parallel", "arbitrary"),
        name="hgrn_prompt",
    )(h, h, h, h, h, h, h, lb.reshape(1, -1), norm_g.reshape(1, -1), cw)


def _mix_sample_kernel(hq_ref, hf_ref, hi_ref, hg_ref, cb_ref, cc_ref, ch_ref, s0_ref, cprev_ref,
                       lb_ref, ng_ref, cw_ref, oh_ref, oc_ref, s_ref, cnew_ref, fg, kg, qg):
    seq = hq_ref.shape[0]

    full = [cprev_ref[0], cprev_ref[1]] + [cc_ref[t] * ch_ref[t] for t in range(seq)]
    for t in range(seq):
        oc_ref[t] = cb_ref[t] * (cw_ref[0] * full[t] + cw_ref[1] * full[t + 1] + cw_ref[2] * full[t + 2])
    cnew_ref[0] = full[seq]
    cnew_ref[1] = full[seq + 1]

    lb = lb_ref[...]
    for t in range(seq):
        q, k, f = _hgrn_gates(hq_ref[t], hf_ref[t], lb)
        fg[t], kg[t], qg[t] = f, k, q
    v = [hi_ref[t] for t in range(seq)]

    def body(dk, o):
        s = s0_ref[dk]
        row = pl.ds(dk, 1)
        new_o = []
        for t in range(seq):
            s = fg[t, row, :] * s + kg[t, row, :] * v[t]
            new_o.append(o[t] + qg[t, row, :] * s)
        s_ref[dk] = s
        return tuple(new_o)

    o = lax.fori_loop(0, HG_DK, body, tuple(jnp.zeros(v[0].shape, F32) for _ in range(seq)))
    for t in range(seq):
        ms = jnp.mean(o[t] * o[t], axis=0, keepdims=True)
        oh_ref[t] = o[t] * lax.rsqrt(ms + EPS) * ng_ref[...] * _silu(hg_ref[t])


def _mix_sample(act_t, state_t, layer, cprev_t, lb, norm_g, cw):
    seq, _, n_req = act_t.shape
    hd = HG_DK

    def act(k):
        return pl.BlockSpec((seq, hd, n_req), lambda h: (0, k * HG_HEADS + h, 0))

    per_head = lambda lead: pl.BlockSpec((lead, hd, n_req), lambda h: (0, h, 0))
    chan = pl.BlockSpec((hd, n_req), lambda h: (h, 0))
    state_in = pl.BlockSpec((None, None, hd, hd, n_req), lambda h: (layer, h, 0, 0, 0))
    state_out = pl.BlockSpec((None, hd, hd, n_req), lambda h: (h, 0, 0, 0))
    spread = lambda a: jnp.broadcast_to(a[..., None], a.shape + (n_req,))
    return pl.pallas_call(
        _mix_sample_kernel,
        grid=(HG_HEADS,),
        in_specs=[act(k) for k in range(7)] + [state_in, per_head(2), chan, chan, per_head(3)],
        out_specs=[per_head(seq), per_head(seq), state_out, per_head(2)],
        out_shape=[jax.ShapeDtypeStruct((seq, HG_WIDTH, n_req), F32),
                   jax.ShapeDtypeStruct((seq, CONV_CH, n_req), F32),
                   jax.ShapeDtypeStruct((HG_HEADS, hd, hd, n_req), F32),
                   jax.ShapeDtypeStruct((2, CONV_CH, n_req), F32)],
        scratch_shapes=[pltpu.VMEM((seq, hd, n_req), F32)] * 3,
        compiler_params=_cparams("parallel"),
        name="mix_sample",
    )(*([act_t] * 7), state_t, cprev_t, spread(lb), spread(norm_g), spread(cw))


def _xattn_prompt_kernel(q_ref, mk_ref, mv_ref, o_ref):
    q = q_ref[...] * (XA_WIDTH // XA_HEADS) ** -0.5
    mk = mk_ref[...].astype(BF16)
    mv = mv_ref[...].astype(BF16)
    head = lax.broadcasted_iota(jnp.int32, q.shape, 1) // (XA_WIDTH // XA_HEADS)
    out = jnp.zeros(q.shape, F32)
    for hh in range(XA_HEADS):
        mine = head == hh
        s = _dot_nt(jnp.where(mine, q, 0.0).astype(BF16), mk)
        m = jnp.max(s, axis=-1, keepdims=True)
        p = jnp.exp(s - m)
        l = jnp.sum(p, axis=-1, keepdims=True)
        out = jnp.where(mine, _dot(p.astype(BF16), mv) / l, out)
    o_ref[...] = out


def _xattn_prompt(q, mkv, batch, seq):
    tm = min(ROW_TILE, seq)
    nt = seq // tm
    return pl.pallas_call(
        _xattn_prompt_kernel,
        grid=(batch, nt),
        in_specs=[pl.BlockSpec((tm, XA_WIDTH), lambda b, t: (b * nt + t, 0)),
                  pl.BlockSpec((N_MEM, XA_WIDTH), lambda b, t: (b, 0)),
                  pl.BlockSpec((N_MEM, XA_WIDTH), lambda b, t: (b, 1))],
        out_specs=pl.BlockSpec((tm, XA_WIDTH), lambda b, t: (b * nt + t, 0)),
        out_shape=jax.ShapeDtypeStruct((batch * seq, XA_WIDTH), F32),
        compiler_params=_cparams("parallel", "parallel"),
        name="xattn_prompt",
    )(q, mkv, mkv)


def _xattn_sample_kernel(qbd_ref, mk_ref, mv_ref, o_ref):
    for r in range(qbd_ref.shape[0]):
        q = (qbd_ref[r] * (XA_WIDTH // XA_HEADS) ** -0.5).astype(BF16)
        s = _dot(q, mk_ref[r].astype(BF16))
        m = jnp.max(s, axis=-1, keepdims=True)
        p = jnp.exp(s - m)
        l = jnp.sum(p, axis=-1, keepdims=True)
        o_ref[r] = _dot_nt(p.astype(BF16), mv_ref[r].astype(BF16)) / l


def _xattn_sample(qbd, mem_kt, mem_vt, layer):
    n_req, n_rows, _ = qbd.shape
    per_step = math.gcd(n_req, XATTN_REQ_PER_STEP)
    mem_spec = pl.BlockSpec((None, per_step, XA_WIDTH, N_MEM), lambda r: (layer, r, 0, 0))
    return pl.pallas_call(
        _xattn_sample_kernel,
        grid=(n_req // per_step,),
        in_specs=[pl.BlockSpec((per_step, n_rows, XA_WIDTH), lambda r: (r, 0, 0)), mem_spec, mem_spec],
        out_specs=pl.BlockSpec((per_step, n_rows, XA_WIDTH), lambda r: (r, 0, 0)),
        out_shape=jax.ShapeDtypeStruct((n_req, n_rows, XA_WIDTH), F32),
        compiler_params=_cparams("parallel"),
        name="xattn_sample",
    )(qbd, mem_kt, mem_vt)


def _ffn_prompt_kernel(x_ref, halo_ref, ox_ref, ox_halo_ref, wxo_ref, g_ref, wup_ref, cw_ref, wdn_ref, gf_ref,
                       o_ref, u_ref, xn_sc, ext_a, ext_b, hmid, acc, *, seq, final_norm):
    rows = x_ref.shape[0]
    wxo = wxo_ref[...]
    x = x_ref[...] + _dot(ox_ref[...].astype(BF16), wxo)
    g = g_ref[...]
    x_halo = halo_ref[...] + _dot(ox_halo_ref[...].astype(BF16), wxo)
    first = (pl.program_id(0) * rows) % seq == 0
    xn_sc[pl.ds(0, HALO), :] = jnp.where(first, 0.0, _rms(x_halo, g)).astype(BF16)
    xn_sc[pl.ds(HALO, rows), :] = _rms(x, g).astype(BF16)
    xn = xn_sc[...]


    n_chunks = D_FF // COL

    def up(j):
        for half, ext in ((0, ext_a), (1, ext_b)):
            c0 = half * D_FF + j * COL
            u = _dot(xn, wup_ref[:, c0:c0 + COL])
            u_ref[0, :, c0:c0 + COL] = u[HALO + rows - 8:, :]
            ext[j % 2] = u

    def conv(ext, j, c0):
        cw = cw_ref[:, c0:c0 + COL]
        s1 = ext[j % 2, pl.ds(HALO - 1, rows), :]
        s2 = ext[j % 2, pl.ds(HALO - 2, rows), :]
        return cw[0:1, :] * s2 + cw[1:2, :] * s1 + cw[2:3, :] * ext[j % 2, pl.ds(HALO, rows), :]

    def gate(j):
        hmid[j % 2] = (_silu(conv(ext_a, j, j * COL)) * conv(ext_b, j, D_FF + j * COL)).astype(BF16)

    def down(j):
        part = _dot(hmid[j % 2], wdn_ref[j * COL:(j + 1) * COL, :])
        acc[...] = part if j == 0 else acc[...] + part

    up(0)
    for j in range(n_chunks):
        if j + 1 < n_chunks:
            up(j + 1)
        gate(j)
        if j >= 1:
            down(j - 1)
    down(n_chunks - 1)
    y = x + acc[...]
    if final_norm:
        y = _rms(y, gf_ref[...])
    o_ref[...] = y


def _ffn_sample_kernel(x_ref, ox_ref, wxo_ref, p0_ref, p1_ref, g_ref, wup_ref, cw_ref, wdn_ref, gf_ref,
                       o_ref, u2_ref, u3_ref, acc, *, final_norm):
    seq, n_req, d = x_ref.shape
    ox = ox_ref[...].reshape(seq * n_req, ox_ref.shape[2])
    x = x_ref[...].reshape(seq * n_req, d) + _dot(ox.astype(BF16), wxo_ref[...])
    xn = _rms(x, g_ref[...]).astype(BF16)
    acc[...] = jnp.zeros(acc.shape, F32)

    def conv(u, c0):
        cw = cw_ref[:, pl.ds(c0, COL)]
        full = [p0_ref[:, pl.ds(c0, COL)], p1_ref[:, pl.ds(c0, COL)]]
        full += [u[t * n_req:(t + 1) * n_req, :] for t in range(seq)]
        u2_ref[:, pl.ds(c0, COL)] = full[seq]
        u3_ref[:, pl.ds(c0, COL)] = full[seq + 1]
        return jnp.concatenate(
            [cw[0:1, :] * full[t] + cw[1:2, :] * full[t + 1] + cw[2:3, :] * full[t + 2] for t in range(seq)],
            axis=0)

    def body(j, carry):
        ca = pl.multiple_of(j * COL, COL)
        cb = pl.multiple_of(D_FF + j * COL, COL)
        ya = conv(_dot(xn, wup_ref[:, pl.ds(ca, COL)]), ca)
        yb = conv(_dot(xn, wup_ref[:, pl.ds(cb, COL)]), cb)
        acc[...] += _dot((_silu(ya) * yb).astype(BF16), wdn_ref[pl.ds(ca, COL), :])
        return carry

    lax.fori_loop(0, D_FF // COL, body, 0)
    y = x + acc[...]
    if final_norm:
        y = _rms(y, gf_ref[...])
    o_ref[...] = y.reshape(seq, n_req, d)


def _resident(shape, layer=None):
    if layer is None:
        return pl.BlockSpec(shape, lambda i: (0,) * len(shape), pipeline_mode=pl.Buffered(1))
    return pl.BlockSpec((None,) + shape, lambda i: (layer,) + (0,) * len(shape), pipeline_mode=pl.Buffered(1))


def _ffn_prompt(x, ox, w_xo, g, w_up, cw, w_dn, g_final, seq, final_norm, layer):
    rows, d = x.shape
    tm = min(ROW_TILE, seq)
    n_up = w_up.shape[2]
    nx = ox.shape[1]
    tiles_per_seq = seq // tm
    x_spec = pl.BlockSpec((tm, d), lambda i: (i, 0))
    halo = lambda n: pl.BlockSpec((HALO, n), lambda i: (jnp.maximum(i * (tm // HALO) - 1, 0), 0))
    return pl.pallas_call(
        functools.partial(_ffn_prompt_kernel, seq=seq, final_norm=final_norm),
        grid=(rows // tm,),
        in_specs=[x_spec, halo(d), pl.BlockSpec((tm, nx), lambda i: (i, 0)), halo(nx), _resident((nx, d)),
                  _resident((1, d)), _resident((d, n_up), layer), _resident((3, n_up)),
                  _resident((D_FF, d), layer), _resident((1, d))],
        out_specs=[x_spec, pl.BlockSpec((1, 8, n_up), lambda i: (i // tiles_per_seq, 0, 0))],
        out_shape=[jax.ShapeDtypeStruct((rows, d), F32),
                   jax.ShapeDtypeStruct((rows // seq, 8, n_up), F32)],
        scratch_shapes=[pltpu.VMEM((HALO + tm, d), BF16),
                        pltpu.VMEM((2, HALO + tm, COL), F32),
                        pltpu.VMEM((2, HALO + tm, COL), F32),
                        pltpu.VMEM((2, tm, COL), BF16),
                        pltpu.VMEM((tm, d), F32)],
        compiler_params=_cparams("arbitrary"),
        name="ffn_prompt",
    )(x, x, ox, ox, w_xo, g.reshape(1, d), w_up, cw, w_dn, g_final.reshape(1, d))


def _ffn_sample(x, ox, w_xo, prev, g, w_up, cw, w_dn, g_final, final_norm, layer):
    seq, n_req, d = x.shape
    n_up = w_up.shape[2]
    nx = ox.shape[2]
    full = lambda shape: pl.BlockSpec(shape, lambda i: (0,) * len(shape))
    return pl.pallas_call(
        functools.partial(_ffn_sample_kernel, final_norm=final_norm),
        grid=(1,),
        in_specs=[_resident((seq, n_req, d)), _resident((seq, n_req, nx)), _resident((nx, d)),
                  _resident((n_req, n_up)), _resident((n_req, n_up)),
                  _resident((1, d)), _resident((d, n_up), layer), _resident((3, n_up)),
                  _resident((D_FF, d), layer), _resident((1, d))],
        out_specs=[full((seq, n_req, d)), full((n_req, n_up)), full((n_req, n_up))],
        out_shape=[jax.ShapeDtypeStruct((seq, n_req, d), F32),
                   jax.ShapeDtypeStruct((n_req, n_up), F32),
                   jax.ShapeDtypeStruct((n_req, n_up), F32)],
        scratch_shapes=[pltpu.VMEM((seq * n_req, d), F32)],
        compiler_params=_cparams("arbitrary"),
        name="ffn_sample",
    )(x, ox, w_xo, prev[:, 0], prev[:, 1], g.reshape(1, d), w_up, cw, w_dn, g_final.reshape(1, d))


def _head_rows(q, n_heads, group):
    n_req, seq, _ = q.shape
    n_kv = n_heads // group
    qh = q.reshape(n_req, seq, n_heads, HEAD_DIM).transpose(0, 2, 1, 3)
    onehot = jax.nn.one_hot(jnp.arange(n_heads) // group, n_kv, dtype=q.dtype)
    out = qh[:, :, :, None, :] * onehot[None, :, None, :, None]
    return out.reshape(n_req, n_heads * seq, n_kv * HEAD_DIM)


def _head_rows_inverse(o, n_heads, group, seq):
    n_req = o.shape[0]
    n_kv = n_heads // group
    o5 = o.reshape(n_req, n_heads, seq, n_kv, HEAD_DIM)
    picked = jnp.stack([o5[:, hh, :, hh // group, :] for hh in range(n_heads)], axis=2)
    return picked.reshape(n_req * seq, n_heads * HEAD_DIM)


def kernel(x_prompt, x_sample, cache_k, cache_v, cache_mem_k, cache_mem_v, state_hgrn, state_conv,
           state_ffn, page_table, mem_prompt, g_mix, w_in, w_out, hg_lb_logits, hg_norm_g, conv_w,
           g_xattn, g_mem, w_xq, w_mk, w_mv, w_xo, g_ffn, w_up, ffn_conv_w, w_down, g_final):
    B, T, D = x_prompt.shape
    DB, S, _ = x_sample.shape
    depth = w_in.shape[0]
    n_pages = page_table.shape[1]
    page = cache_k.shape[2]
    past = n_pages * page

    lb_p = jax.nn.softmax(hg_lb_logits.astype(F32), axis=0)
    lower = jnp.cumsum(lb_p, axis=0) - lb_p[0:1]

    rope_p = _rope_tables(jnp.arange(T, dtype=jnp.int32))
    rope_s = _rope_tables(jnp.tile(past + jnp.arange(S, dtype=jnp.int32), DB))

    w_in_b, w_out_b, w_xq_b, w_xo_b = (w.astype(BF16) for w in (w_in, w_out, w_xq, w_xo))
    w_up_b, w_down_b = w_up.astype(BF16), w_down.astype(BF16)
    w_mkv_b = jnp.concatenate([w_mk, w_mv], axis=-1).astype(BF16)

    def token_minor(c):
        return c.transpose(0, 1, 3, 4, 2).reshape(c.shape[0], c.shape[1], c.shape[3] * c.shape[4], c.shape[2])

    cache_k4, cache_v4 = token_minor(cache_k), token_minor(cache_v)
    mem_k4, mem_v4 = token_minor(cache_mem_k), token_minor(cache_mem_v)
    state_t = state_hgrn.transpose(0, 2, 3, 4, 1)
    mem2 = mem_prompt.reshape(B * N_MEM, D)

    xp = x_prompt.reshape(B * T, D)
    xs = x_sample.reshape(DB * S, D)
    outs = [[] for _ in range(12)]
    for l in range(depth):
        last = l == depth - 1
        h = _norm_proj(xp, g_mix[l], w_in_b, l, rope=rope_p, n_rope=(ATT_WIDTH + KV_WIDTH) // COL)
        k1 = h[:, CB_K * COL:(CB_K + 1) * COL].reshape(B, T, KV_HEADS, HEAD_DIM)
        v1 = h[:, CB_V * COL:(CB_V + 1) * COL].reshape(B, T, KV_HEADS, HEAD_DIM)
        a = _moba_prompt(h, B, T)
        o_h, o_c, st_t, c_tail = _hgrn_prompt(h, lower[l], hg_norm_g[l], conv_w[l], B, T)
        st5 = st_t.reshape(B, HG_HEADS, HG_DK, HG_HEADS, HG_DK)
        h1 = jnp.stack([st5[:, hh, :, hh, :] for hh in range(HG_HEADS)], axis=1).swapaxes(-1, -2)
        c1 = c_tail[:, 6:8, :]
        w_out_parts = [w_out_b[l, :ATT_WIDTH], w_out_b[l, ATT_WIDTH:ATT_WIDTH + HG_WIDTH],
                       w_out_b[l, ATT_WIDTH + HG_WIDTH:]]
        xp, qx = _resid_proj(xp, [a, o_h, o_c], w_out_parts, g_xattn[l], w_xq_b[l])
        mkv = _norm_proj(mem2, g_mem[l], w_mkv_b, l)
        mk = mkv[:, :XA_WIDTH].reshape(B, N_MEM, XA_HEADS, XA_WIDTH // XA_HEADS)
        mv = mkv[:, XA_WIDTH:].reshape(B, N_MEM, XA_HEADS, XA_WIDTH // XA_HEADS)
        ox = _xattn_prompt(qx, mkv, B, T)
        xp, u_tail = _ffn_prompt(xp, ox, w_xo_b[l], g_ffn[l], w_up_b, ffn_conv_w[l], w_down_b,
                                 g_final, T, last, l)
        fs1 = u_tail[:, 6:8, :]

        hs = _norm_proj(xs, g_mix[l], w_in_b, l, rope=rope_s, n_rope=(ATT_WIDTH + KV_WIDTH) // COL)
        hs3 = hs.reshape(DB, S, IN_WIDTH)
        k_new = hs3[:, :, CB_K * COL:(CB_K + 1) * COL]
        v_new = hs3[:, :, CB_V * COL:(CB_V + 1) * COL]
        qbd = _head_rows(hs3[:, :, :ATT_WIDTH], ATT_HEADS, ATT_HEADS // KV_HEADS)
        o_att = _moba_sample(qbd, k_new, v_new, cache_k4, cache_v4, page_table, l)
        a_s = _head_rows_inverse(o_att, ATT_HEADS, ATT_HEADS // KV_HEADS, S)
        act_t = hs3[:, :, CB_HQ * COL:].transpose(1, 2, 0)
        oh_t, oc_t, h2, c2_t = _mix_sample(act_t, state_t, l, state_conv[l].transpose(1, 2, 0),
                                           lower[l], hg_norm_g[l], conv_w[l])
        oh_s, oc_s, c2 = (a.transpose(2, 0, 1) for a in (oh_t, oc_t, c2_t))
        xs, qxs = _resid_proj(xs, [a_s, oh_s.reshape(DB * S, HG_WIDTH), oc_s.reshape(DB * S, CONV_CH)],
                              w_out_parts, g_xattn[l], w_xq_b[l])
        qxbd = _head_rows(qxs.reshape(DB, S, XA_WIDTH), XA_HEADS, 1)
        oxs = _head_rows_inverse(_xattn_sample(qxbd, mem_k4, mem_v4, l), XA_HEADS, 1, S)
        xs_t, u_a, u_b = _ffn_sample(xs.reshape(DB, S, D).swapaxes(0, 1),
                                     oxs.reshape(DB, S, XA_WIDTH).swapaxes(0, 1), w_xo_b[l], state_ffn[l],
                                     g_ffn[l], w_up_b, ffn_conv_w[l], w_down_b, g_final, last, l)
        xs = xs_t.swapaxes(0, 1).reshape(DB * S, D)
        fs2 = jnp.stack([u_a, u_b], axis=1)

        for lst, val in zip(outs, (k1, v1, k_new.reshape(DB, S, KV_HEADS, HEAD_DIM),
                                   v_new.reshape(DB, S, KV_HEADS, HEAD_DIM), h1, h2, c1, c2, fs1, fs2, mk, mv)):
            lst.append(val)

    res = [jnp.stack(o) for o in outs]
    res[5] = res[5].transpose(0, 4, 1, 2, 3)
    return (xp.reshape(B, T, D), xs.reshape(DB, S, D)) + tuple(res)
```

```python
import functools
import math

import numpy as np
import jax
import jax.numpy as jnp
from jax import lax
from jax.experimental import pallas as pl
from jax.experimental.pallas import tpu as pltpu

F32 = jnp.float32
BF16 = jnp.bfloat16
HIGHEST = lax.Precision.HIGHEST
NEG_INF = float("-inf")
LOG2_E = 1.4426950408889634

D_MODEL = 1024
HEAD_DIM = 64
ATT_HEADS = 8
KV_HEADS = 4
ATT_WIDTH = ATT_HEADS * HEAD_DIM
KV_WIDTH = KV_HEADS * HEAD_DIM
MOBA_BLOCK = 256
MOBA_TOPK = 3
ROPE_THETA = 10000.0
HG_HEADS = 4
HG_DK = 64
HG_WIDTH = HG_HEADS * HG_DK
HG_CHUNK = 64
HG_LEAF = 8
CONV_CH = 256
N_MEM = 256
XA_HEADS = 4
XA_WIDTH = 256
D_FF = 2816
EPS = 1e-6
IN_WIDTH = 2816
COL = 256
LANES = 128
ROW_TILE = 512
HALO = 16
MOBA_REQ_PER_STEP = 2
XATTN_REQ_PER_STEP = 8
VMEM_LIMIT = 56 * 1024 * 1024

CB_K, CB_V, CB_HQ, CB_HF, CB_HI, CB_HG, CB_CB, CB_CC, CB_CH = 2, 3, 4, 5, 6, 7, 8, 9, 10


def _cparams(*sem):
    return pltpu.CompilerParams(dimension_semantics=sem, vmem_limit_bytes=VMEM_LIMIT)


def _rms(x, g):
    ms = jnp.mean(x * x, axis=-1, keepdims=True)
    return x * lax.rsqrt(ms + EPS) * g


def _sigmoid(z):
    return 1.0 / (1.0 + jnp.exp2(z * -LOG2_E))


def _silu(z):
    return z * _sigmoid(z)


def _dot(a, b):
    return jnp.dot(a, b, preferred_element_type=F32)


def _dot_nt(a, b, precision=None):
    return lax.dot_general(a, b, (((1,), (1,)), ((), ())), precision=precision,
                           preferred_element_type=F32)


def _dot_tn(a, b):
    return lax.dot_general(a, b, (((0,), (0,)), ((), ())), preferred_element_type=F32)


def _block_diag_mask(rows, cols, blk):
    r = lax.broadcasted_iota(jnp.int32, (rows, cols), 0) // blk
    c = lax.broadcasted_iota(jnp.int32, (rows, cols), 1) // blk
    return r == c


def _norm_proj_kernel(x_ref, g_ref, w_ref, cos_ref, sin_ref, o_ref, *t_refs, n_rope):
    xn = _rms(x_ref[...], g_ref[...]).astype(BF16)
    n_chunks = o_ref.shape[1] // COL
    if n_rope:
        cos = jnp.concatenate([cos_ref[...]] * (COL // LANES), axis=1)
        sin = jnp.concatenate([sin_ref[...]] * (COL // LANES), axis=1)
        lane = lax.broadcasted_iota(jnp.int32, cos.shape, 1)
        first_half = (lane % HEAD_DIM) < HEAD_DIM // 2
    for c in range(n_chunks):
        y = _dot(xn, w_ref[:, c * COL:(c + 1) * COL])
        if c < n_rope:
            partner = jnp.where(first_half, pltpu.roll(y, COL - HEAD_DIM // 2, 1),
                                pltpu.roll(y, HEAD_DIM // 2, 1))
            y = y * cos + partner * sin
        o_ref[:, c * COL:(c + 1) * COL] = y
        if t_refs and c < CB_HQ:
            qt_ref, kt_ref, vt_ref = t_refs
            if c < CB_K:
                qt_ref[c * COL:(c + 1) * COL, :] = y.T
            else:
                (kt_ref if c == CB_K else vt_ref)[...] = y.T


def _norm_proj(x, g, w, layer, rope=None, n_rope=0, seq=None):
    rows, d = x.shape
    n = w.shape[2]
    tm = min(ROW_TILE, rows)
    out_specs = [pl.BlockSpec((tm, n), lambda i: (i, 0))]
    out_shape = [jax.ShapeDtypeStruct((rows, n), F32)]
    if seq is not None:
        tiles = seq // tm
        kv_t = pl.BlockSpec((None, KV_WIDTH, tm), lambda i: (i // tiles, 0, i % tiles))
        out_specs += [pl.BlockSpec((ATT_WIDTH, tm), lambda i: (0, i)), kv_t, kv_t]
        out_shape += [jax.ShapeDtypeStruct((ATT_WIDTH, rows), F32),
                      jax.ShapeDtypeStruct((rows // seq, KV_WIDTH, seq), F32),
                      jax.ShapeDtypeStruct((rows // seq, KV_WIDTH, seq), F32)]
    if rope is None:
        cos = sin = jnp.zeros((tm, LANES), F32)
    else:
        cos, sin = rope
    tab_blocks = cos.shape[0] // tm
    res = pl.pallas_call(
        functools.partial(_norm_proj_kernel, n_rope=n_rope),
        grid=(rows // tm,),
        in_specs=[
            pl.BlockSpec((tm, d), lambda i: (i, 0)),
            pl.BlockSpec((1, d), lambda i: (0, 0)),
            pl.BlockSpec((None, d, n), lambda i: (layer, 0, 0)),
            pl.BlockSpec((tm, LANES), lambda i: (i % tab_blocks, 0)),
            pl.BlockSpec((tm, LANES), lambda i: (i % tab_blocks, 0)),
        ],
        out_specs=out_specs,
        out_shape=out_shape,
        compiler_params=_cparams("parallel"),
        name="norm_proj",
    )(x, g.reshape(1, d), w, cos, sin)
    return res[0] if seq is None else res


def _rope_tables(pos):
    half = HEAD_DIM // 2
    inv = ROPE_THETA ** (-jnp.arange(half, dtype=F32) / half)
    ang = pos.astype(F32)[:, None] * inv[None, :]
    cos = jnp.cos(ang)
    sin = jnp.sin(ang)
    reps = LANES // HEAD_DIM
    cos_t = jnp.concatenate([cos, cos] * reps, axis=1)
    sin_t = jnp.concatenate([-sin, sin] * reps, axis=1)
    return cos_t, sin_t


def _resid_proj_kernel(*refs, n_in):
    x_ref = refs[0]
    a_refs = refs[1:1 + n_in]
    w_refs = refs[1 + n_in:1 + 2 * n_in]
    g_ref, wq_ref, o_ref, q_ref = refs[1 + 2 * n_in:]
    acc = x_ref[...]
    for a_ref, w_ref in zip(a_refs, w_refs):
        acc = acc + _dot(a_ref[...].astype(BF16), w_ref[...])
    o_ref[...] = acc
    q_ref[...] = _dot(_rms(acc, g_ref[...]).astype(BF16), wq_ref[...])


def _resid_proj(x, acts, weights, g, wq):
    rows, d = x.shape
    tm = min(ROW_TILE, rows)
    n_in = len(acts)
    nq = wq.shape[1]
    row = lambda n: pl.BlockSpec((tm, n), lambda i: (i, 0))
    in_specs = [row(d)] + [row(a.shape[1]) for a in acts]
    in_specs += [pl.BlockSpec(w.shape, lambda i: (0, 0)) for w in weights]
    in_specs += [pl.BlockSpec((1, d), lambda i: (0, 0)), pl.BlockSpec(wq.shape, lambda i: (0, 0))]
    return pl.pallas_call(
        functools.partial(_resid_proj_kernel, n_in=n_in),
        grid=(rows // tm,),
        in_specs=in_specs,
        out_specs=[row(d), row(nq)],
        out_shape=[jax.ShapeDtypeStruct((rows, d), F32), jax.ShapeDtypeStruct((rows, nq), F32)],
        compiler_params=_cparams("parallel"),
        name="resid_proj",
    )(x, *acts, *weights, g.reshape(1, d), wq)


def _topk_rank(gate, n_cand, n_valid, axis):
    idx = lax.broadcasted_iota(jnp.int32, gate.shape, axis)
    cnt = jnp.zeros(gate.shape, F32)
    for m in range(n_cand):
        gm = gate[m:m + 1, :] if axis == 0 else gate[:, m:m + 1]
        beats = jnp.where(gm > gate, 1.0, jnp.where(gm == gate, jnp.where(m < idx, 1.0, 0.0), 0.0))
        cnt = cnt + beats * jnp.where(m < n_valid, 1.0, 0.0)
    return cnt, idx


def _moba_prompt_kernel(qt_ref, k_ref, vt_ref, o_ref, kbf, vt, sbuf, pbuf, acc):
    i = pl.program_id(2)
    n_blk = k_ref.shape[0] // MOBA_BLOCK
    nq = 4 * MOBA_BLOCK

    @pl.when(i == 0)
    def _():
        kbf[...] = k_ref[...].astype(BF16)
        vt[...] = vt_ref[...].astype(BF16)

    qt = qt_ref[...]
    zero = jnp.zeros((HEAD_DIM, MOBA_BLOCK), F32)
    qst = jnp.concatenate(
        [jnp.concatenate([qt[j * HEAD_DIM:(j + 1) * HEAD_DIM, :], zero] if j < 2
                         else [zero, qt[j * HEAD_DIM:(j + 1) * HEAD_DIM, :]], axis=0) for j in range(4)],
        axis=1)
    lo = lax.broadcasted_iota(jnp.int32, (MOBA_BLOCK, LANES), 1) < HEAD_DIM

    qsc = (qst * (HEAD_DIM ** -0.5 * LOG2_E)).astype(BF16)
    causal = (lax.broadcasted_iota(jnp.int32, (MOBA_BLOCK, nq), 0)
              <= lax.broadcasted_iota(jnp.int32, (MOBA_BLOCK, nq), 1) % MOBA_BLOCK)
    bidx = lax.broadcasted_iota(jnp.int32, (n_blk, nq), 0)

    def attend(own):
        blocks = [slice(n * MOBA_BLOCK, (n + 1) * MOBA_BLOCK) for n in range(own + 1)]
        cmax = []
        gate = jnp.zeros((n_blk, nq), F32)
        for n, rows in enumerate(blocks):
            s = _dot(kbf[rows, :], qsc)
            if n == own:
                s = jnp.where(causal, s, NEG_INF)
            else:
                gate = jnp.where(bidx == n, jnp.sum(s, axis=0, keepdims=True), gate)
            sbuf[rows, :] = s
            cmax.append(jnp.max(s, axis=0, keepdims=True))
        if own > MOBA_TOPK:
            cnt, _ = _topk_rank(gate, own, own, 0)
            bias = jnp.where((bidx < own) & (cnt < MOBA_TOPK), 0.0, NEG_INF)
        else:
            bias = jnp.zeros((n_blk, nq), F32)
        m = cmax[own]
        for n in range(own):
            m = jnp.maximum(m, cmax[n] + bias[n:n + 1, :])
        l = jnp.zeros((1, nq), F32)
        out = jnp.zeros((LANES, nq), F32)
        for n, rows in enumerate(blocks):
            shift = m if n == own else m - bias[n:n + 1, :]
            p = jnp.exp2(sbuf[rows, :] - shift)
            l = l + jnp.sum(p, axis=0, keepdims=True)
            pbuf[rows, :] = p.astype(BF16)
            out = out + _dot(vt[:, rows], pbuf[rows, :])
        acc[...] = out / l

    for own in range(n_blk):
        pl.when(i == own)(functools.partial(attend, own))

    o = acc[...].T
    b = MOBA_BLOCK
    tile0 = jnp.where(lo, o[0:b], pltpu.roll(o[b:2 * b], HEAD_DIM, 1))
    tile1 = jnp.where(lo, pltpu.roll(o[2 * b:3 * b], HEAD_DIM, 1), o[3 * b:4 * b])
    o_ref[...] = jnp.concatenate([tile0, tile1], axis=1)


def _moba_prompt(h, q_t, v_t, batch, seq):
    n_blk = seq // MOBA_BLOCK
    kcol = (CB_K * COL) // LANES
    return pl.pallas_call(
        _moba_prompt_kernel,
        grid=(batch, KV_HEADS // 2, n_blk),
        in_specs=[
            pl.BlockSpec((COL, MOBA_BLOCK), lambda b, p, i: (p, b * n_blk + i)),
            pl.BlockSpec((seq, LANES), lambda b, p, i: (b, kcol + p)),
            pl.BlockSpec((None, LANES, seq), lambda b, p, i: (b, p, 0)),
        ],
        out_specs=pl.BlockSpec((MOBA_BLOCK, COL), lambda b, p, i: (b * n_blk + i, p)),
        out_shape=jax.ShapeDtypeStruct((batch * seq, ATT_WIDTH), F32),
        scratch_shapes=[
            pltpu.VMEM((seq, LANES), BF16),
            pltpu.VMEM((LANES, seq), BF16),
            pltpu.VMEM((seq, 4 * MOBA_BLOCK), F32),
            pltpu.VMEM((seq, 4 * MOBA_BLOCK), BF16),
            pltpu.VMEM((LANES, 4 * MOBA_BLOCK), F32),
        ],
        compiler_params=_cparams("parallel", "parallel", "arbitrary"),
        name="moba_prompt",
    )(q_t, h, v_t)


def _moba_sample_kernel(pt_ref, qbd_ref, kn_ref, vn_ref, *rest, n_pages, page):
    del pt_ref
    per_step = qbd_ref.shape[0]
    n_refs = per_step * n_pages
    o_ref = rest[2 * n_refs]
    kc_all, vc_all = rest[2 * n_refs + 1:]
    for g in range(per_step):
        _moba_sample_request(qbd_ref.at[g], kn_ref.at[g], vn_ref.at[g],
                             rest[g * n_pages:(g + 1) * n_pages],
                             rest[n_refs + g * n_pages:n_refs + (g + 1) * n_pages],
                             o_ref.at[g], kc_all.at[g], vc_all.at[g], page)


def _moba_sample_request(qbd_ref, kn_ref, vn_ref, kp, vp, o_ref, kc, vc, page):
    n_pages = len(kp)
    n_blk = n_pages * page // MOBA_BLOCK
    n_rows = qbd_ref.shape[0]
    seq_new = kn_ref.shape[0]

    for pg in range(n_pages):
        kc[:, pg * page:(pg + 1) * page] = kp[pg][...].astype(BF16)
        vc[:, pg * page:(pg + 1) * page] = vp[pg][...].astype(BF16)

    qbd = qbd_ref[...]
    qs = qbd * HEAD_DIM ** -0.5
    s_all = _dot(qs.astype(BF16), kc[...])

    lane = lax.broadcasted_iota(jnp.int32, (n_rows, LANES), 1)
    gate = jnp.zeros((n_rows, LANES), F32)
    for n in range(n_blk):
        gate = jnp.where(lane == n, jnp.mean(s_all[:, n * MOBA_BLOCK:(n + 1) * MOBA_BLOCK], axis=-1, keepdims=True),
                         gate)
    cnt, bidx = _topk_rank(gate, n_blk, n_blk, 1)
    bias = jnp.where((bidx < n_blk) & (cnt < MOBA_TOPK), 0.0, NEG_INF)
    s_blk = [s_all[:, n * MOBA_BLOCK:(n + 1) * MOBA_BLOCK] + bias[:, n:n + 1] for n in range(n_blk)]

    kn = kn_ref[...]
    vn = vn_ref[...]
    tok = lax.broadcasted_iota(jnp.int32, (n_rows, 1), 0) % seq_new
    s_own = []
    for t in range(seq_new):
        st = jnp.sum(qs * kn[t:t + 1, :], axis=-1, keepdims=True)
        s_own.append(jnp.where(t <= tok, st, NEG_INF))

    m = s_own[0]
    for st in s_own[1:]:
        m = jnp.maximum(m, st)
    for sb in s_blk:
        m = jnp.maximum(m, jnp.max(sb, axis=-1, keepdims=True))

    l = jnp.zeros((n_rows, 1), F32)
    out = jnp.zeros((n_rows, KV_WIDTH), F32)
    for t in range(seq_new):
        pt = jnp.exp(s_own[t] - m)
        l = l + pt
        out = out + pt * vn[t:t + 1, :]
    for n in range(n_blk):
        pn = jnp.exp(s_blk[n] - m)
        l = l + jnp.sum(pn, axis=-1, keepdims=True)
        out = out + _dot_nt(pn.astype(BF16), vc[:, n * MOBA_BLOCK:(n + 1) * MOBA_BLOCK])
    o_ref[...] = out / l


def _moba_sample(qbd, k_new, v_new, cache_kt, cache_vt, page_table, layer):
    n_req, n_pages = page_table.shape
    page = cache_kt.shape[3]
    n_rows = qbd.shape[1]
    seq_new = k_new.shape[1]

    per_step = math.gcd(n_req, MOBA_REQ_PER_STEP)
    n_refs = per_step * n_pages

    def page_spec(j):
        return pl.BlockSpec((None, None, KV_WIDTH, page), lambda r, pt: (layer, pt[r * n_refs + j], 0, 0))

    in_specs = [
        pl.BlockSpec((per_step, n_rows, KV_WIDTH), lambda r, pt: (r, 0, 0)),
        pl.BlockSpec((per_step, seq_new, KV_WIDTH), lambda r, pt: (r, 0, 0)),
        pl.BlockSpec((per_step, seq_new, KV_WIDTH), lambda r, pt: (r, 0, 0)),
    ]
    in_specs += [page_spec(j) for j in range(n_refs)]
    in_specs += [page_spec(j) for j in range(n_refs)]
    grid_spec = pltpu.PrefetchScalarGridSpec(
        num_scalar_prefetch=1,
        grid=(n_req // per_step,),
        in_specs=in_specs,
        out_specs=pl.BlockSpec((per_step, n_rows, KV_WIDTH), lambda r, pt: (r, 0, 0)),
        scratch_shapes=[
            pltpu.VMEM((per_step, KV_WIDTH, n_pages * page), BF16),
            pltpu.VMEM((per_step, KV_WIDTH, n_pages * page), BF16),
        ],
    )
    return pl.pallas_call(
        functools.partial(_moba_sample_kernel, n_pages=n_pages, page=page),
        grid_spec=grid_spec,
        out_shape=jax.ShapeDtypeStruct((n_req, n_rows, KV_WIDTH), F32),
        compiler_params=_cparams("arbitrary"),
        name="moba_sample",
    )(page_table.reshape(-1), qbd, k_new, v_new, *([cache_kt] * n_refs), *([cache_vt] * n_refs))


def _hgrn_gates(hq, hf, lb):
    q = _silu(hq)
    f = lb + (1.0 - lb) * _sigmoid(hf)
    k = (1.0 - lb) * _sigmoid(-hf)
    return q, k, f


def _head_rms(o, norm_g):
    head_mean = jnp.where(_block_diag_mask(HG_WIDTH, HG_WIDTH, HG_DK), 1.0 / HG_DK, 0.0)
    ms = jnp.dot(o * o, head_mean, precision=HIGHEST, preferred_element_type=F32)
    return o * lax.rsqrt(ms + EPS) * norm_g


def _short_conv(ext_ref, rows, cw):
    return (cw[0:1, :] * ext_ref[pl.ds(HALO - 2, rows), :]
            + cw[1:2, :] * ext_ref[pl.ds(HALO - 1, rows), :]
            + cw[2:3, :] * ext_ref[pl.ds(HALO, rows), :])


def _hgrn_prompt_kernel(hq_ref, hf_ref, hi_ref, hg_ref, cb_ref, cc_ref, ch_ref, lb_ref, ng_ref, cw_ref,
                        oh_ref, oc_ref, st_ref, tail_ref, st, ext):
    tt = pl.program_id(1)
    rows = hq_ref.shape[0]
    n_chunks = rows // HG_CHUNK
    width = HG_WIDTH
    bd = _block_diag_mask(width, width, HG_DK)

    @pl.when(tt == 0)
    def _():
        ext[pl.ds(0, HALO), :] = jnp.zeros((HALO, CONV_CH), F32)

    @pl.when(tt > 0)
    def _():
        ext[pl.ds(0, HALO), :] = ext[pl.ds(rows, HALO), :]

    u = cc_ref[...] * ch_ref[...]
    ext[pl.ds(HALO, rows), :] = u
    oc_ref[...] = cb_ref[...] * _short_conv(ext, rows, cw_ref[...])
    tail_ref[0] = u[rows - 8:rows, :]

    @pl.when(tt == 0)
    def _():
        st[...] = jnp.zeros(st.shape, F32)

    q, k, f = _hgrn_gates(hq_ref[...], hf_ref[...], lb_ref[...])
    logf = jnp.log(f)
    v = hi_ref[...]
    r_i =lax.broadcasted_iota(jnp.int32, (rows, rows), 0)
    c_i = lax.broadcasted_iota(jnp.int32, (rows, rows), 1)
    tri = jnp.where((r_i // HG_CHUNK == c_i // HG_CHUNK) & (c_i <= r_i), 1.0, 0.0)
    a_all = jnp.dot(tri, logf, precision=HIGHEST, preferred_element_type=F32)

    L = HG_CHUNK
    t_i = lax.broadcasted_iota(jnp.int32, (L, width), 0)
    tw = lax.broadcasted_iota(jnp.int32, (L, width), 0)
    sw_col = lax.broadcasted_iota(jnp.int32, (L, width), 1) % L
    head_ones = jnp.where(bd, 1.0, 0.0).astype(BF16)
    outs = []
    for c in range(n_chunks):
        sl = slice(c * L, (c + 1) * L)
        a, qc, kc, vc = a_all[sl], q[sl], k[sl], v[sl]
        a_end = a[L - 1:L, :]
        st_old = st[...]
        o = _dot_nt((qc * jnp.exp(a)).astype(BF16), st_old.astype(BF16))

        sw = jnp.zeros((L, width), F32)
        b = L // 2
        while b >= HG_LEAF:
            nb = L // b
            rq = jnp.concatenate(
                [jnp.broadcast_to(a[j * b - 1:j * b, :], (b, width)) if j % 2 else a[j * b:(j + 1) * b, :]
                 for j in range(nb)], axis=0)
            rk = jnp.concatenate(
                [a[j * b:(j + 1) * b, :] if j % 2 else jnp.broadcast_to(a[(j + 1) * b - 1:(j + 1) * b, :], (b, width))
                 for j in range(nb)], axis=0)
            odd = (t_i // b) % 2 == 1
            ql = jnp.where(odd, qc * jnp.exp(a - rq), 0.0).astype(BF16)
            kl = jnp.where(odd, 0.0, kc * jnp.exp(rk - a))
            k_bd = jnp.concatenate([kl.astype(BF16)] * HG_HEADS, axis=0) * head_ones
            sc = _dot_nt(ql, k_bd)
            pair = ((tw // b) % 2 == 1) & (sw_col // b == tw // b - 1)
            sw = sw + jnp.where(pair, sc, 0.0)
            b //= 2
        v_bd = jnp.concatenate([vc.astype(BF16)] * HG_HEADS, axis=0) * head_ones
        o = o + _dot(sw.astype(BF16), v_bd)

        prods = []
        vrs = []
        for delta in range(HG_LEAF):
            if delta:
                kr, ar, vr = (pltpu.roll(x.reshape(L // HG_LEAF, HG_LEAF, width), delta, 1).reshape(L, width)
                              for x in (kc, a, vc))
            else:
                kr, ar, vr = kc, a, vc
            live = (t_i % HG_LEAF) >= delta
            prods.append(jnp.where(live, qc * kr * jnp.exp(jnp.where(live, a - ar, 0.0)), 0.0))
            vrs.append(vr)
        sc = _dot(jnp.concatenate(prods, axis=0).astype(BF16), head_ones)
        for delta in range(HG_LEAF):
            o = o + sc[delta * L:(delta + 1) * L] * vrs[delta]
        outs.append(o)

        ke = (kc * jnp.exp(a_end - a)).astype(BF16)
        upd = _dot_tn(vc.astype(BF16), ke)
        st[...] = st_old * jnp.exp(a_end) + jnp.where(bd, upd, 0.0)

    o_all = jnp.concatenate(outs, axis=0)
    oh_ref[...] = _head_rms(o_all, ng_ref[...]) * _silu(hg_ref[...])
    st_ref[0] = st[...]


def _hgrn_prompt(h, lb, norm_g, cw, batch, seq):
    rows = 256
    nt = seq // rows

    def col(cb):
        return pl.BlockSpec((rows, COL), lambda b, t: (b * nt + t, cb))

    const = lambda shape: pl.BlockSpec(shape, lambda b, t: (0, 0))
    row_out = pl.BlockSpec((rows, COL), lambda b, t: (b * nt + t, 0))
    return pl.pallas_call(
        _hgrn_prompt_kernel,
        grid=(batch, nt),
        in_specs=[col(CB_HQ), col(CB_HF), col(CB_HI), col(CB_HG), col(CB_CB), col(CB_CC), col(CB_CH),
                  const((1, HG_WIDTH)), const((1, HG_WIDTH)), const((3, CONV_CH))],
        out_specs=[row_out, row_out,
                   pl.BlockSpec((1, HG_WIDTH, HG_WIDTH), lambda b, t: (b, 0, 0)),
                   pl.BlockSpec((1, 8, CONV_CH), lambda b, t: (b, 0, 0))],
        out_shape=[jax.ShapeDtypeStruct((batch * seq, HG_WIDTH), F32),
                   jax.ShapeDtypeStruct((batch * seq, CONV_CH), F32),
                   jax.ShapeDtypeStruct((batch, HG_WIDTH, HG_WIDTH), F32),
                   jax.ShapeDtypeStruct((batch, 8, CONV_CH), F32)],
        scratch_shapes=[pltpu.VMEM((HG_WIDTH, HG_WIDTH), F32),
                        pltpu.VMEM((HALO + rows + HALO, CONV_CH), F32)],
        compiler_params=_cparams("parallel", "arbitrary"),
        name="hgrn_prompt",
    )(h, h, h, h, h, h, h, lb.reshape(1, -1), norm_g.reshape(1, -1), cw)


def _mix_sample_kernel(hq_ref, hf_ref, hi_ref, hg_ref, cb_ref, cc_ref, ch_ref, s0_ref, cprev_ref,
                       lb_ref, ng_ref, cw_ref, oh_ref, oc_ref, s_ref, cnew_ref, fg, kg, qg):
    seq = hq_ref.shape[0]

    full = [cprev_ref[0], cprev_ref[1]] + [cc_ref[t] * ch_ref[t] for t in range(seq)]
    for t in range(seq):
        oc_ref[t] = cb_ref[t] * (cw_ref[0] * full[t] + cw_ref[1] * full[t + 1] + cw_ref[2] * full[t + 2])
    cnew_ref[0] = full[seq]
    cnew_ref[1] = full[seq + 1]

    lb = lb_ref[...]
    for t in range(seq):
        q, k, f = _hgrn_gates(hq_ref[t], hf_ref[t], lb)
        fg[t], kg[t], qg[t] = f, k, q
    v = [hi_ref[t] for t in range(seq)]

    def body(dk, o):
        s = s0_ref[dk]
        row = pl.ds(dk, 1)
        new_o = []
        for t in range(seq):
            s = fg[t, row, :] * s + kg[t, row, :] * v[t]
            new_o.append(o[t] + qg[t, row, :] * s)
        s_ref[dk] = s
        return tuple(new_o)

    o = lax.fori_loop(0, HG_DK, body, tuple(jnp.zeros(v[0].shape, F32) for _ in range(seq)))
    for t in range(seq):
        ms = jnp.mean(o[t] * o[t], axis=0, keepdims=True)
        oh_ref[t] = o[t] * lax.rsqrt(ms + EPS) * ng_ref[...] * _silu(hg_ref[t])


def _mix_sample(act_t, state_t, layer, cprev_t, lb, norm_g, cw):
    seq, _, n_req = act_t.shape
    hd = HG_DK

    def act(k):
        return pl.BlockSpec((seq, hd, n_req), lambda h: (0, k * HG_HEADS + h, 0))

    per_head = lambda lead: pl.BlockSpec((lead, hd, n_req), lambda h: (0, h, 0))
    chan = pl.BlockSpec((hd, n_req), lambda h: (h, 0))
    state_in = pl.BlockSpec((None, None, hd, hd, n_req), lambda h: (layer, h, 0, 0, 0))
    state_out = pl.BlockSpec((None, hd, hd, n_req), lambda h: (h, 0, 0, 0))
    spread = lambda a: jnp.broadcast_to(a[..., None], a.shape + (n_req,))
    return pl.pallas_call(
        _mix_sample_kernel,
        grid=(HG_HEADS,),
        in_specs=[act(k) for k in range(7)] + [state_in, per_head(2), chan, chan, per_head(3)],
        out_specs=[per_head(seq), per_head(seq), state_out, per_head(2)],
        out_shape=[jax.ShapeDtypeStruct((seq, HG_WIDTH, n_req), F32),
                   jax.ShapeDtypeStruct((seq, CONV_CH, n_req), F32),
                   jax.ShapeDtypeStruct((HG_HEADS, hd, hd, n_req), F32),
                   jax.ShapeDtypeStruct((2, CONV_CH, n_req), F32)],
        scratch_shapes=[pltpu.VMEM((seq, hd, n_req), F32)] * 3,
        compiler_params=_cparams("parallel"),
        name="mix_sample",
    )(*([act_t] * 7), state_t, cprev_t, spread(lb), spread(norm_g), spread(cw))


def _xattn_prompt_kernel(q_ref, mk_ref, mv_ref, o_ref):
    q = q_ref[...] * (XA_WIDTH // XA_HEADS) ** -0.5
    mk = mk_ref[...].astype(BF16)
    mv = mv_ref[...].astype(BF16)
    head = lax.broadcasted_iota(jnp.int32, q.shape, 1) // (XA_WIDTH // XA_HEADS)
    out = jnp.zeros(q.shape, F32)
    for hh in range(XA_HEADS):
        mine = head == hh
        s = _dot_nt(jnp.where(mine, q, 0.0).astype(BF16), mk)
        m = jnp.max(s, axis=-1, keepdims=True)
        p = jnp.exp(s - m)
        l = jnp.sum(p, axis=-1, keepdims=True)
        out = jnp.where(mine, _dot(p.astype(BF16), mv) / l, out)
    o_ref[...] = out


def _xattn_prompt(q, mkv, batch, seq):
    tm = min(ROW_TILE, seq)
    nt = seq // tm
    return pl.pallas_call(
        _xattn_prompt_kernel,
        grid=(batch, nt),
        in_specs=[pl.BlockSpec((tm, XA_WIDTH), lambda b, t: (b * nt + t, 0)),
                  pl.BlockSpec((N_MEM, XA_WIDTH), lambda b, t: (b, 0)),
                  pl.BlockSpec((N_MEM, XA_WIDTH), lambda b, t: (b, 1))],
        out_specs=pl.BlockSpec((tm, XA_WIDTH), lambda b, t: (b * nt + t, 0)),
        out_shape=jax.ShapeDtypeStruct((batch * seq, XA_WIDTH), F32),
        compiler_params=_cparams("parallel", "parallel"),
        name="xattn_prompt",
    )(q, mkv, mkv)


def _xattn_sample_kernel(qbd_ref, mk_ref, mv_ref, o_ref):
    for r in range(qbd_ref.shape[0]):
        q = (qbd_ref[r] * (XA_WIDTH // XA_HEADS) ** -0.5).astype(BF16)
        s = _dot(q, mk_ref[r].astype(BF16))
        m = jnp.max(s, axis=-1, keepdims=True)
        p = jnp.exp(s - m)
        l = jnp.sum(p, axis=-1, keepdims=True)
        o_ref[r] = _dot_nt(p.astype(BF16), mv_ref[r].astype(BF16)) / l


def _xattn_sample(qbd, mem_kt, mem_vt, layer):
    n_req, n_rows, _ = qbd.shape
    per_step = math.gcd(n_req, XATTN_REQ_PER_STEP)
    mem_spec = pl.BlockSpec((None, per_step, XA_WIDTH, N_MEM), lambda r: (layer, r, 0, 0))
    return pl.pallas_call(
        _xattn_sample_kernel,
        grid=(n_req // per_step,),
        in_specs=[pl.BlockSpec((per_step, n_rows, XA_WIDTH), lambda r: (r, 0, 0)), mem_spec, mem_spec],
        out_specs=pl.BlockSpec((per_step, n_rows, XA_WIDTH), lambda r: (r, 0, 0)),
        out_shape=jax.ShapeDtypeStruct((n_req, n_rows, XA_WIDTH), F32),
        compiler_params=_cparams("parallel"),
        name="xattn_sample",
    )(qbd, mem_kt, mem_vt)


def _ffn_prompt_kernel(x_ref, halo_ref, ox_ref, ox_halo_ref, wxo_ref, g_ref, wup_ref, cw_ref, wdn_ref, gf_ref,
                       o_ref, u_ref, xn_sc, ext_a, ext_b, hmid, acc, *, seq, final_norm):
    rows = x_ref.shape[0]
    wxo = wxo_ref[...]
    x = x_ref[...] + _dot(ox_ref[...].astype(BF16), wxo)
    g = g_ref[...]
    x_halo = halo_ref[...] + _dot(ox_halo_ref[...].astype(BF16), wxo)
    first = (pl.program_id(0) * rows) % seq == 0
    xn_sc[pl.ds(0, HALO), :] = jnp.where(first, 0.0, _rms(x_halo, g)).astype(BF16)
    xn_sc[pl.ds(HALO, rows), :] = _rms(x, g).astype(BF16)
    xn = xn_sc[...]


    n_chunks = D_FF // COL

    def up(j):
        for half, ext in ((0, ext_a), (1, ext_b)):
            c0 = half * D_FF + j * COL
            u = _dot(xn, wup_ref[:, c0:c0 + COL])
            u_ref[0, :, c0:c0 + COL] = u[HALO + rows - 8:, :]
            ext[j % 2] = u

    def conv(ext, j, c0):
        cw = cw_ref[:, c0:c0 + COL]
        s1 = ext[j % 2, pl.ds(HALO - 1, rows), :]
        s2 = ext[j % 2, pl.ds(HALO - 2, rows), :]
        return cw[0:1, :] * s2 + cw[1:2, :] * s1 + cw[2:3, :] * ext[j % 2, pl.ds(HALO, rows), :]

    def gate(j):
        hmid[j % 2] = (_silu(conv(ext_a, j, j * COL)) * conv(ext_b, j, D_FF + j * COL)).astype(BF16)

    def down(j):
        part = _dot(hmid[j % 2], wdn_ref[j * COL:(j + 1) * COL, :])
        acc[...] = part if j == 0 else acc[...] + part

    up(0)
    for j in range(n_chunks):
        if j + 1 < n_chunks:
            up(j + 1)
        gate(j)
        if j >= 1:
            down(j - 1)
    down(n_chunks - 1)
    y = x + acc[...]
    if final_norm:
        y = _rms(y, gf_ref[...])
    o_ref[...] = y


def _ffn_sample_kernel(x_ref, ox_ref, wxo_ref, p0_ref, p1_ref, g_ref, wup_ref, cw_ref, wdn_ref, gf_ref,
                       o_ref, u2_ref, u3_ref, acc, *, final_norm):
    seq, n_req, d = x_ref.shape
    ox = ox_ref[...].reshape(seq * n_req, ox_ref.shape[2])
    x = x_ref[...].reshape(seq * n_req, d) + _dot(ox.astype(BF16), wxo_ref[...])
    xn = _rms(x, g_ref[...]).astype(BF16)
    acc[...] = jnp.zeros(acc.shape, F32)

    def conv(u, c0):
        cw = cw_ref[:, pl.ds(c0, COL)]
        full = [p0_ref[:, pl.ds(c0, COL)], p1_ref[:, pl.ds(c0, COL)]]
        full += [u[t * n_req:(t + 1) * n_req, :] for t in range(seq)]
        u2_ref[:, pl.ds(c0, COL)] = full[seq]
        u3_ref[:, pl.ds(c0, COL)] = full[seq + 1]
        return jnp.concatenate(
            [cw[0:1, :] * full[t] + cw[1:2, :] * full[t + 1] + cw[2:3, :] * full[t + 2] for t in range(seq)],
            axis=0)

    def body(j, carry):
        ca = pl.multiple_of(j * COL, COL)
        cb = pl.multiple_of(D_FF + j * COL, COL)
        ya = conv(_dot(xn, wup_ref[:, pl.ds(ca, COL)]), ca)
        yb = conv(_dot(xn, wup_ref[:, pl.ds(cb, COL)]), cb)
        acc[...] += _dot((_silu(ya) * yb).astype(BF16), wdn_ref[pl.ds(ca, COL), :])
        return carry

    lax.fori_loop(0, D_FF // COL, body, 0)
    y = x + acc[...]
    if final_norm:
        y = _rms(y, gf_ref[...])
    o_ref[...] = y.reshape(seq, n_req, d)


def _resident(shape, layer=None):
    if layer is None:
        return pl.BlockSpec(shape, lambda i: (0,) * len(shape), pipeline_mode=pl.Buffered(1))
    return pl.BlockSpec((None,) + shape, lambda i: (layer,) + (0,) * len(shape), pipeline_mode=pl.Buffered(1))


def _ffn_prompt(x, ox, w_xo, g, w_up, cw, w_dn, g_final, seq, final_norm, layer):
    rows, d = x.shape
    tm = min(ROW_TILE, seq)
    n_up = w_up.shape[2]
    nx = ox.shape[1]
    tiles_per_seq = seq // tm
    x_spec = pl.BlockSpec((tm, d), lambda i: (i, 0))
    halo = lambda n: pl.BlockSpec((HALO, n), lambda i: (jnp.maximum(i * (tm // HALO) - 1, 0), 0))
    return pl.pallas_call(
        functools.partial(_ffn_prompt_kernel, seq=seq, final_norm=final_norm),
        grid=(rows // tm,),
        in_specs=[x_spec, halo(d), pl.BlockSpec((tm, nx), lambda i: (i, 0)), halo(nx), _resident((nx, d)),
                  _resident((1, d)), _resident((d, n_up), layer), _resident((3, n_up)),
                  _resident((D_FF, d), layer), _resident((1, d))],
        out_specs=[x_spec, pl.BlockSpec((1, 8, n_up), lambda i: (i // tiles_per_seq, 0, 0))],
        out_shape=[jax.ShapeDtypeStruct((rows, d), F32),
                   jax.ShapeDtypeStruct((rows // seq, 8, n_up), F32)],
        scratch_shapes=[pltpu.VMEM((HALO + tm, d), BF16),
                        pltpu.VMEM((2, HALO + tm, COL), F32),
                        pltpu.VMEM((2, HALO + tm, COL), F32),
                        pltpu.VMEM((2, tm, COL), BF16),
                        pltpu.VMEM((tm, d), F32)],
        compiler_params=_cparams("arbitrary"),
        name="ffn_prompt",
    )(x, x, ox, ox, w_xo, g.reshape(1, d), w_up, cw, w_dn, g_final.reshape(1, d))


def _ffn_sample(x, ox, w_xo, prev, g, w_up, cw, w_dn, g_final, final_norm, layer):
    seq, n_req, d = x.shape
    n_up = w_up.shape[2]
    nx = ox.shape[2]
    full = lambda shape: pl.BlockSpec(shape, lambda i: (0,) * len(shape))
    return pl.pallas_call(
        functools.partial(_ffn_sample_kernel, final_norm=final_norm),
        grid=(1,),
        in_specs=[_resident((seq, n_req, d)), _resident((seq, n_req, nx)), _resident((nx, d)),
                  _resident((n_req, n_up)), _resident((n_req, n_up)),
                  _resident((1, d)), _resident((d, n_up), layer), _resident((3, n_up)),
                  _resident((D_FF, d), layer), _resident((1, d))],
        out_specs=[full((seq, n_req, d)), full((n_req, n_up)), full((n_req, n_up))],
        out_shape=[jax.ShapeDtypeStruct((seq, n_req, d), F32),
                   jax.ShapeDtypeStruct((n_req, n_up), F32),
                   jax.ShapeDtypeStruct((n_req, n_up), F32)],
        scratch_shapes=[pltpu.VMEM((seq * n_req, d), F32)],
        compiler_params=_cparams("arbitrary"),
        name="ffn_sample",
    )(x, ox, w_xo, prev[:, 0], prev[:, 1], g.reshape(1, d), w_up, cw, w_dn, g_final.reshape(1, d))


def _head_rows(q, n_heads, group):
    n_req, seq, _ = q.shape
    n_kv = n_heads // group
    qh = q.reshape(n_req, seq, n_heads, HEAD_DIM).transpose(0, 2, 1, 3)
    onehot = jax.nn.one_hot(jnp.arange(n_heads) // group, n_kv, dtype=q.dtype)
    out = qh[:, :, :, None, :] * onehot[None, :, None, :, None]
    return out.reshape(n_req, n_heads * seq, n_kv * HEAD_DIM)


def _head_rows_inverse(o, n_heads, group, seq):
    n_req = o.shape[0]
    n_kv = n_heads // group
    o5 = o.reshape(n_req, n_heads, seq, n_kv, HEAD_DIM)
    picked = jnp.stack([o5[:, hh, :, hh // group, :] for hh in range(n_heads)], axis=2)
    return picked.reshape(n_req * seq, n_heads * HEAD_DIM)


def kernel(x_prompt, x_sample, cache_k, cache_v, cache_mem_k, cache_mem_v, state_hgrn, state_conv,
           state_ffn, page_table, mem_prompt, g_mix, w_in, w_out, hg_lb_logits, hg_norm_g, conv_w,
           g_xattn, g_mem, w_xq, w_mk, w_mv, w_xo, g_ffn, w_up, ffn_conv_w, w_down, g_final):
    B, T, D = x_prompt.shape
    DB, S, _ = x_sample.shape
    depth = w_in.shape[0]
    n_pages = page_table.shape[1]
    page = cache_k.shape[2]
    past = n_pages * page

    lb_p = jax.nn.softmax(hg_lb_logits.astype(F32), axis=0)
    lower = jnp.cumsum(lb_p, axis=0) - lb_p[0:1]

    rope_p = _rope_tables(jnp.arange(T, dtype=jnp.int32))
    rope_s = _rope_tables(jnp.tile(past + jnp.arange(S, dtype=jnp.int32), DB))

    w_in_b, w_out_b, w_xq_b, w_xo_b = (w.astype(BF16) for w in (w_in, w_out, w_xq, w_xo))
    w_up_b, w_down_b = w_up.astype(BF16), w_down.astype(BF16)
    w_mkv_b = jnp.concatenate([w_mk, w_mv], axis=-1).astype(BF16)

    def token_minor(c):
        return c.transpose(0, 1, 3, 4, 2).reshape(c.shape[0], c.shape[1], c.shape[3] * c.shape[4], c.shape[2])

    cache_k4, cache_v4 = token_minor(cache_k), token_minor(cache_v)
    mem_k4, mem_v4 = token_minor(cache_mem_k), token_minor(cache_mem_v)
    state_t = state_hgrn.transpose(0, 2, 3, 4, 1)
    mem2 = mem_prompt.reshape(B * N_MEM, D)

    xp = x_prompt.reshape(B * T, D)
    xs = x_sample.reshape(DB * S, D)
    outs = [[] for _ in range(12)]
    for l in range(depth):
        last = l == depth - 1
        h, q_t, k1, v1 = _norm_proj(xp, g_mix[l], w_in_b, l, rope=rope_p,
                                    n_rope=(ATT_WIDTH + KV_WIDTH) // COL, seq=T)
        a = _moba_prompt(h, q_t, v1, B, T)
        o_h, o_c, st_t, c_tail = _hgrn_prompt(h, lower[l], hg_norm_g[l], conv_w[l], B, T)
        st5 = st_t.reshape(B, HG_HEADS, HG_DK, HG_HEADS, HG_DK)
        h1 = jnp.stack([st5[:, hh, :, hh, :] for hh in range(HG_HEADS)], axis=1).swapaxes(-1, -2)
        c1 = c_tail[:, 6:8, :]
        w_out_parts = [w_out_b[l, :ATT_WIDTH], w_out_b[l, ATT_WIDTH:ATT_WIDTH + HG_WIDTH],
                       w_out_b[l, ATT_WIDTH + HG_WIDTH:]]
        xp, qx = _resid_proj(xp, [a, o_h, o_c], w_out_parts, g_xattn[l], w_xq_b[l])
        mkv = _norm_proj(mem2, g_mem[l], w_mkv_b, l)
        mk = mkv[:, :XA_WIDTH].reshape(B, N_MEM, XA_HEADS, XA_WIDTH // XA_HEADS)
        mv = mkv[:, XA_WIDTH:].reshape(B, N_MEM, XA_HEADS, XA_WIDTH // XA_HEADS)
        ox = _xattn_prompt(qx, mkv, B, T)
        xp, u_tail = _ffn_prompt(xp, ox, w_xo_b[l], g_ffn[l], w_up_b, ffn_conv_w[l], w_down_b,
                                 g_final, T, last, l)
        fs1 = u_tail[:, 6:8, :]

        hs = _norm_proj(xs, g_mix[l], w_in_b, l, rope=rope_s, n_rope=(ATT_WIDTH + KV_WIDTH) // COL)
        hs3 = hs.reshape(DB, S, IN_WIDTH)
        k_new = hs3[:, :, CB_K * COL:(CB_K + 1) * COL]
        v_new = hs3[:, :, CB_V * COL:(CB_V + 1) * COL]
        qbd = _head_rows(hs3[:, :, :ATT_WIDTH], ATT_HEADS, ATT_HEADS // KV_HEADS)
        o_att = _moba_sample(qbd, k_new, v_new, cache_k4, cache_v4, page_table, l)
        a_s = _head_rows_inverse(o_att, ATT_HEADS, ATT_HEADS // KV_HEADS, S)
        act_t = hs3[:, :, CB_HQ * COL:].transpose(1, 2, 0)
        oh_t, oc_t, h2, c2_t = _mix_sample(act_t, state_t, l, state_conv[l].transpose(1, 2, 0),
                                           lower[l], hg_norm_g[l], conv_w[l])
        oh_s, oc_s, c2 = (a.transpose(2, 0, 1) for a in (oh_t, oc_t, c2_t))
        xs, qxs = _resid_proj(xs, [a_s, oh_s.reshape(DB * S, HG_WIDTH), oc_s.reshape(DB * S, CONV_CH)],
                              w_out_parts, g_xattn[l], w_xq_b[l])
        qxbd = _head_rows(qxs.reshape(DB, S, XA_WIDTH), XA_HEADS, 1)
        oxs = _head_rows_inverse(_xattn_sample(qxbd, mem_k4, mem_v4, l), XA_HEADS, 1, S)
        xs_t, u_a, u_b = _ffn_sample(xs.reshape(DB, S, D).swapaxes(0, 1),
                                     oxs.reshape(DB, S, XA_WIDTH).swapaxes(0, 1), w_xo_b[l], state_ffn[l],
                                     g_ffn[l], w_up_b, ffn_conv_w[l], w_down_b, g_final, last, l)
        xs = xs_t.swapaxes(0, 1).reshape(DB * S, D)
        fs2 = jnp.stack([u_a, u_b], axis=1)

        for lst, val in zip(outs, (k1, v1, k_new.reshape(DB, S, KV_HEADS, HEAD_DIM),
                                   v_new.reshape(DB, S, KV_HEADS, HEAD_DIM), h1, h2, c1, c2, fs1, fs2, mk, mv)):
            lst.append(val)

    res = [jnp.stack(o) for o in outs]
    for j in (0, 1):
        res[j] = res[j].reshape(depth, B, KV_HEADS, HEAD_DIM, T).transpose(0, 1, 4, 2, 3)
    res[5] = res[5].transpose(0, 4, 1, 2, 3)
    return (xp.reshape(B, T, D), xs.reshape(DB, S, D)) + tuple(res)
```

```python
import functools
import math

import numpy as np
import jax
import jax.numpy as jnp
from jax import lax
from jax.experimental import pallas as pl
from jax.experimental.pallas import tpu as pltpu

F32 = jnp.float32
BF16 = jnp.bfloat16
HIGHEST = lax.Precision.HIGHEST
NEG_INF = float("-inf")
LOG2_E = 1.4426950408889634

D_MODEL = 1024
HEAD_DIM = 64
ATT_HEADS = 8
KV_HEADS = 4
ATT_WIDTH = ATT_HEADS * HEAD_DIM
KV_WIDTH = KV_HEADS * HEAD_DIM
MOBA_BLOCK = 256
MOBA_TOPK = 3
ROPE_THETA = 10000.0
HG_HEADS = 4
HG_DK = 64
HG_WIDTH = HG_HEADS * HG_DK
HG_CHUNK = 64
SUBLANES = 8
HG_LEAF = 4
CONV_CH = 256
N_MEM = 256
XA_HEADS = 4
XA_WIDTH = 256
D_FF = 2816
EPS = 1e-6
IN_WIDTH = 2816
COL = 256
LANES = 128
ROW_TILE = 512
HALO = 16
MOBA_REQ_PER_STEP = 2
XATTN_REQ_PER_STEP = 8
VMEM_LIMIT = 56 * 1024 * 1024

CB_K, CB_V, CB_HQ, CB_HF, CB_HI, CB_HG, CB_CB, CB_CC, CB_CH = 2, 3, 4, 5, 6, 7, 8, 9, 10


def _cparams(*sem):
    return pltpu.CompilerParams(dimension_semantics=sem, vmem_limit_bytes=VMEM_LIMIT)


def _rms(x, g):
    ms = jnp.mean(x * x, axis=-1, keepdims=True)
    return x * lax.rsqrt(ms + EPS) * g


def _sigmoid(z):
    return 1.0 / (1.0 + jnp.exp2(z * -LOG2_E))


def _silu(z):
    return z * _sigmoid(z)


def _dot(a, b):
    return jnp.dot(a, b, preferred_element_type=F32)


def _dot_nt(a, b, precision=None):
    return lax.dot_general(a, b, (((1,), (1,)), ((), ())), precision=precision,
                           preferred_element_type=F32)


def _dot_tn(a, b):
    return lax.dot_general(a, b, (((0,), (0,)), ((), ())), preferred_element_type=F32)


def _block_diag_mask(rows, cols, blk):
    r = lax.broadcasted_iota(jnp.int32, (rows, cols), 0) // blk
    c = lax.broadcasted_iota(jnp.int32, (rows, cols), 1) // blk
    return r == c


def _norm_proj_kernel(x_ref, g_ref, w_ref, cos_ref, sin_ref, o_ref, *t_refs, n_rope):
    xn = _rms(x_ref[...], g_ref[...]).astype(BF16)
    n_chunks = o_ref.shape[1] // COL
    if n_rope:
        cos = jnp.concatenate([cos_ref[...]] * (COL // LANES), axis=1)
        sin = jnp.concatenate([sin_ref[...]] * (COL // LANES), axis=1)
        lane = lax.broadcasted_iota(jnp.int32, cos.shape, 1)
        first_half = (lane % HEAD_DIM) < HEAD_DIM // 2
    for c in range(n_chunks):
        y = _dot(xn, w_ref[:, c * COL:(c + 1) * COL])
        if c < n_rope:
            partner = jnp.where(first_half, pltpu.roll(y, COL - HEAD_DIM // 2, 1),
                                pltpu.roll(y, HEAD_DIM // 2, 1))
            y = y * cos + partner * sin
        o_ref[:, c * COL:(c + 1) * COL] = y
        if t_refs and c < CB_HQ:
            qt_ref, kt_ref, vt_ref = t_refs
            if c < CB_K:
                qt_ref[c * COL:(c + 1) * COL, :] = y.T
            else:
                (kt_ref if c == CB_K else vt_ref)[...] = y.T


def _norm_proj(x, g, w, layer, rope=None, n_rope=0, seq=None):
    rows, d = x.shape
    n = w.shape[2]
    tm = min(ROW_TILE, rows)
    out_specs = [pl.BlockSpec((tm, n), lambda i: (i, 0))]
    out_shape = [jax.ShapeDtypeStruct((rows, n), F32)]
    if seq is not None:
        tiles = seq // tm
        kv_t = pl.BlockSpec((None, KV_WIDTH, tm), lambda i: (i // tiles, 0, i % tiles))
        out_specs += [pl.BlockSpec((ATT_WIDTH, tm), lambda i: (0, i)), kv_t, kv_t]
        out_shape += [jax.ShapeDtypeStruct((ATT_WIDTH, rows), F32),
                      jax.ShapeDtypeStruct((rows // seq, KV_WIDTH, seq), F32),
                      jax.ShapeDtypeStruct((rows // seq, KV_WIDTH, seq), F32)]
    if rope is None:
        cos = sin = jnp.zeros((tm, LANES), F32)
    else:
        cos, sin = rope
    tab_blocks = cos.shape[0] // tm
    res = pl.pallas_call(
        functools.partial(_norm_proj_kernel, n_rope=n_rope),
        grid=(rows // tm,),
        in_specs=[
            pl.BlockSpec((tm, d), lambda i: (i, 0)),
            pl.BlockSpec((1, d), lambda i: (0, 0)),
            pl.BlockSpec((None, d, n), lambda i: (layer, 0, 0)),
            pl.BlockSpec((tm, LANES), lambda i: (i % tab_blocks, 0)),
            pl.BlockSpec((tm, LANES), lambda i: (i % tab_blocks, 0)),
        ],
        out_specs=out_specs,
        out_shape=out_shape,
        compiler_params=_cparams("parallel"),
        name="norm_proj",
    )(x, g.reshape(1, d), w, cos, sin)
    return res[0] if seq is None else res


def _rope_tables(pos):
    half = HEAD_DIM // 2
    inv = ROPE_THETA ** (-jnp.arange(half, dtype=F32) / half)
    ang = pos.astype(F32)[:, None] * inv[None, :]
    cos = jnp.cos(ang)
    sin = jnp.sin(ang)
    reps = LANES // HEAD_DIM
    cos_t = jnp.concatenate([cos, cos] * reps, axis=1)
    sin_t = jnp.concatenate([-sin, sin] * reps, axis=1)
    return cos_t, sin_t


def _resid_proj_kernel(*refs, n_in, first_transposed):
    x_ref = refs[0]
    a_refs = refs[1:1 + n_in]
    w_refs = refs[1 + n_in:1 + 2 * n_in]
    g_ref, wq_ref, o_ref, q_ref = refs[1 + 2 * n_in:]
    acc = x_ref[...]
    for k, (a_ref, w_ref) in enumerate(zip(a_refs, w_refs)):
        a = a_ref[...].astype(BF16)
        acc = acc + (_dot_tn(a, w_ref[...]) if first_transposed and k == 0 else _dot(a, w_ref[...]))
    o_ref[...] = acc
    q_ref[...] = _dot(_rms(acc, g_ref[...]).astype(BF16), wq_ref[...])


def _resid_proj(x, acts, weights, g, wq, first_transposed=False):
    rows, d = x.shape
    tm = min(ROW_TILE, rows)
    n_in = len(acts)
    nq = wq.shape[1]
    row = lambda n: pl.BlockSpec((tm, n), lambda i: (i, 0))
    in_specs = [row(d)] + [row(a.shape[1]) for a in acts]
    if first_transposed:
        in_specs[1] = pl.BlockSpec((acts[0].shape[0], tm), lambda i: (0, i))
    in_specs += [pl.BlockSpec(w.shape, lambda i: (0, 0)) for w in weights]
    in_specs += [pl.BlockSpec((1, d), lambda i: (0, 0)), pl.BlockSpec(wq.shape, lambda i: (0, 0))]
    return pl.pallas_call(
        functools.partial(_resid_proj_kernel, n_in=n_in, first_transposed=first_transposed),
        grid=(rows // tm,),
        in_specs=in_specs,
        out_specs=[row(d), row(nq)],
        out_shape=[jax.ShapeDtypeStruct((rows, d), F32), jax.ShapeDtypeStruct((rows, nq), F32)],
        compiler_params=_cparams("parallel"),
        name="resid_proj",
    )(x, *acts, *weights, g.reshape(1, d), wq)


def _topk_rank(gate, n_cand, n_valid, axis):
    idx = lax.broadcasted_iota(jnp.int32, gate.shape, axis)
    cnt = jnp.zeros(gate.shape, F32)
    for m in range(n_cand):
        gm = gate[m:m + 1, :] if axis == 0 else gate[:, m:m + 1]
        beats = jnp.where(gm > gate, 1.0, jnp.where(gm == gate, jnp.where(m < idx, 1.0, 0.0), 0.0))
        cnt = cnt + beats * jnp.where(m < n_valid, 1.0, 0.0)
    return cnt, idx


def _moba_prompt_kernel(qt_ref, k_ref, vt_ref, o_ref, kbf, vt, sbuf, pbuf, acc):
    i = pl.program_id(2)
    n_blk = k_ref.shape[0] // MOBA_BLOCK
    nq = 4 * MOBA_BLOCK

    @pl.when(i == 0)
    def _():
        kbf[...] = k_ref[...].astype(BF16)
        vt[...] = vt_ref[...].astype(BF16)

    qt = qt_ref[...]
    zero = jnp.zeros((HEAD_DIM, MOBA_BLOCK), F32)
    qst = jnp.concatenate(
        [jnp.concatenate([qt[j * HEAD_DIM:(j + 1) * HEAD_DIM, :], zero] if j < 2
                         else [zero, qt[j * HEAD_DIM:(j + 1) * HEAD_DIM, :]], axis=0) for j in range(4)],
        axis=1)

    qsc = (qst * (HEAD_DIM ** -0.5 * LOG2_E)).astype(BF16)
    causal = (lax.broadcasted_iota(jnp.int32, (MOBA_BLOCK, nq), 0)
              <= lax.broadcasted_iota(jnp.int32, (MOBA_BLOCK, nq), 1) % MOBA_BLOCK)
    bidx = lax.broadcasted_iota(jnp.int32, (n_blk, nq), 0)

    def attend(own):
        blocks = [slice(n * MOBA_BLOCK, (n + 1) * MOBA_BLOCK) for n in range(own + 1)]
        cmax = []
        gate = jnp.zeros((n_blk, nq), F32)
        for n, rows in enumerate(blocks):
            s = _dot(kbf[rows, :], qsc)
            if n == own:
                s = jnp.where(causal, s, NEG_INF)
            else:
                gate = jnp.where(bidx == n, jnp.sum(s, axis=0, keepdims=True), gate)
            sbuf[rows, :] = s
            cmax.append(jnp.max(s, axis=0, keepdims=True))
        if own > MOBA_TOPK:
            cnt, _ = _topk_rank(gate, own, own, 0)
            bias = jnp.where((bidx < own) & (cnt < MOBA_TOPK), 0.0, NEG_INF)
        else:
            bias = jnp.zeros((n_blk, nq), F32)
        m = cmax[own]
        for n in range(own):
            m = jnp.maximum(m, cmax[n] + bias[n:n + 1, :])
        l = jnp.zeros((1, nq), F32)
        out = jnp.zeros((LANES, nq), F32)
        for n, rows in enumerate(blocks):
            shift = m if n == own else m - bias[n:n + 1, :]
            p = jnp.exp2(sbuf[rows, :] - shift)
            l = l + jnp.sum(p, axis=0, keepdims=True)
            pbuf[rows, :] = p.astype(BF16)
            out = out + _dot(vt[:, rows], pbuf[rows, :])
        acc[...] = out / l

    for own in range(n_blk):
        pl.when(i == own)(functools.partial(attend, own))

    for j in range(4):
        r0 = 0 if j < 2 else HEAD_DIM
        o_ref[j * HEAD_DIM:(j + 1) * HEAD_DIM, :] = acc[r0:r0 + HEAD_DIM, j * MOBA_BLOCK:(j + 1) * MOBA_BLOCK]


def _moba_prompt(h, q_t, v_t, batch, seq):
    n_blk = seq // MOBA_BLOCK
    kcol = (CB_K * COL) // LANES
    return pl.pallas_call(
        _moba_prompt_kernel,
        grid=(batch, KV_HEADS // 2, n_blk),
        in_specs=[
            pl.BlockSpec((COL, MOBA_BLOCK), lambda b, p, i: (p, b * n_blk + i)),
            pl.BlockSpec((seq, LANES), lambda b, p, i: (b, kcol + p)),
            pl.BlockSpec((None, LANES, seq), lambda b, p, i: (b, p, 0)),
        ],
        out_specs=pl.BlockSpec((COL, MOBA_BLOCK), lambda b, p, i: (p, b * n_blk + i)),
        out_shape=jax.ShapeDtypeStruct((ATT_WIDTH, batch * seq), F32),
        scratch_shapes=[
            pltpu.VMEM((seq, LANES), BF16),
            pltpu.VMEM((LANES, seq), BF16),
            pltpu.VMEM((seq, 4 * MOBA_BLOCK), F32),
            pltpu.VMEM((seq, 4 * MOBA_BLOCK), BF16),
            pltpu.VMEM((LANES, 4 * MOBA_BLOCK), F32),
        ],
        compiler_params=_cparams("parallel", "parallel", "arbitrary"),
        name="moba_prompt",
    )(q_t, h, v_t)


def _moba_sample_kernel(pt_ref, qbd_ref, kn_ref, vn_ref, *rest, n_pages, page):
    del pt_ref
    per_step = qbd_ref.shape[0]
    n_refs = per_step * n_pages
    o_ref = rest[2 * n_refs]
    kc_all, vc_all = rest[2 * n_refs + 1:]
    for g in range(per_step):
        _moba_sample_request(qbd_ref.at[g], kn_ref.at[g], vn_ref.at[g],
                             rest[g * n_pages:(g + 1) * n_pages],
                             rest[n_refs + g * n_pages:n_refs + (g + 1) * n_pages],
                             o_ref.at[g], kc_all.at[g], vc_all.at[g], page)


def _moba_sample_request(qbd_ref, kn_ref, vn_ref, kp, vp, o_ref, kc, vc, page):
    n_pages = len(kp)
    n_blk = n_pages * page // MOBA_BLOCK
    n_rows = qbd_ref.shape[0]
    seq_new = kn_ref.shape[0]

    for pg in range(n_pages):
        kc[:, pg * page:(pg + 1) * page] = kp[pg][...].astype(BF16)
        vc[:, pg * page:(pg + 1) * page] = vp[pg][...].astype(BF16)

    qbd = qbd_ref[...]
    qs = qbd * HEAD_DIM ** -0.5
    s_all = _dot(qs.astype(BF16), kc[...])

    lane = lax.broadcasted_iota(jnp.int32, (n_rows, LANES), 1)
    gate = jnp.zeros((n_rows, LANES), F32)
    for n in range(n_blk):
        gate = jnp.where(lane == n, jnp.mean(s_all[:, n * MOBA_BLOCK:(n + 1) * MOBA_BLOCK], axis=-1, keepdims=True),
                         gate)
    cnt, bidx = _topk_rank(gate, n_blk, n_blk, 1)
    bias = jnp.where((bidx < n_blk) & (cnt < MOBA_TOPK), 0.0, NEG_INF)
    s_blk = [s_all[:, n * MOBA_BLOCK:(n + 1) * MOBA_BLOCK] + bias[:, n:n + 1] for n in range(n_blk)]

    kn = kn_ref[...]
    vn = vn_ref[...]
    tok = lax.broadcasted_iota(jnp.int32, (n_rows, 1), 0) % seq_new
    s_own = []
    for t in range(seq_new):
        st = jnp.sum(qs * kn[t:t + 1, :], axis=-1, keepdims=True)
        s_own.append(jnp.where(t <= tok, st, NEG_INF))

    m = s_own[0]
    for st in s_own[1:]:
        m = jnp.maximum(m, st)
    for sb in s_blk:
        m = jnp.maximum(m, jnp.max(sb, axis=-1, keepdims=True))

    l = jnp.zeros((n_rows, 1), F32)
    out = jnp.zeros((n_rows, KV_WIDTH), F32)
    for t in range(seq_new):
        pt = jnp.exp(s_own[t] - m)
        l = l + pt
        out = out + pt * vn[t:t + 1, :]
    for n in range(n_blk):
        pn = jnp.exp(s_blk[n] - m)
        l = l + jnp.sum(pn, axis=-1, keepdims=True)
        out = out + _dot_nt(pn.astype(BF16), vc[:, n * MOBA_BLOCK:(n + 1) * MOBA_BLOCK])
    o_ref[...] = out / l


def _moba_sample(qbd, k_new, v_new, cache_kt, cache_vt, page_table, layer):
    n_req, n_pages = page_table.shape
    page = cache_kt.shape[3]
    n_rows = qbd.shape[1]
    seq_new = k_new.shape[1]

    per_step = math.gcd(n_req, MOBA_REQ_PER_STEP)
    n_refs = per_step * n_pages

    def page_spec(j):
        return pl.BlockSpec((None, None, KV_WIDTH, page), lambda r, pt: (layer, pt[r * n_refs + j], 0, 0))

    in_specs = [
        pl.BlockSpec((per_step, n_rows, KV_WIDTH), lambda r, pt: (r, 0, 0)),
        pl.BlockSpec((per_step, seq_new, KV_WIDTH), lambda r, pt: (r, 0, 0)),
        pl.BlockSpec((per_step, seq_new, KV_WIDTH), lambda r, pt: (r, 0, 0)),
    ]
    in_specs += [page_spec(j) for j in range(n_refs)]
    in_specs += [page_spec(j) for j in range(n_refs)]
    grid_spec = pltpu.PrefetchScalarGridSpec(
        num_scalar_prefetch=1,
        grid=(n_req // per_step,),
        in_specs=in_specs,
        out_specs=pl.BlockSpec((per_step, n_rows, KV_WIDTH), lambda r, pt: (r, 0, 0)),
        scratch_shapes=[
            pltpu.VMEM((per_step, KV_WIDTH, n_pages * page), BF16),
            pltpu.VMEM((per_step, KV_WIDTH, n_pages * page), BF16),
        ],
    )
    return pl.pallas_call(
        functools.partial(_moba_sample_kernel, n_pages=n_pages, page=page),
        grid_spec=grid_spec,
        out_shape=jax.ShapeDtypeStruct((n_req, n_rows, KV_WIDTH), F32),
        compiler_params=_cparams("arbitrary"),
        name="moba_sample",
    )(page_table.reshape(-1), qbd, k_new, v_new, *([cache_kt] * n_refs), *([cache_vt] * n_refs))


def _hgrn_gates(hq, hf, lb):
    q = _silu(hq)
    f = lb + (1.0 - lb) * _sigmoid(hf)
    k = (1.0 - lb) * _sigmoid(-hf)
    return q, k, f


def _head_rms(o, norm_g):
    head_mean = jnp.where(_block_diag_mask(HG_WIDTH, HG_WIDTH, HG_DK), 1.0 / HG_DK, 0.0)
    ms = jnp.dot(o * o, head_mean, precision=HIGHEST, preferred_element_type=F32)
    return o * lax.rsqrt(ms + EPS) * norm_g


def _short_conv(ext_ref, rows, cw):
    return (cw[0:1, :] * ext_ref[pl.ds(HALO - 2, rows), :]
            + cw[1:2, :] * ext_ref[pl.ds(HALO - 1, rows), :]
            + cw[2:3, :] * ext_ref[pl.ds(HALO, rows), :])


def _hgrn_prompt_kernel(hq_ref, hf_ref, hi_ref, hg_ref, cb_ref, cc_ref, ch_ref, lb_ref, ng_ref, cw_ref,
                        oh_ref, oc_ref, st_ref, tail_ref, st, ext):
    tt = pl.program_id(1)
    rows = hq_ref.shape[0]
    n_chunks = rows // HG_CHUNK
    width = HG_WIDTH
    bd = _block_diag_mask(width, width, HG_DK)

    @pl.when(tt == 0)
    def _():
        ext[pl.ds(0, HALO), :] = jnp.zeros((HALO, CONV_CH), F32)

    @pl.when(tt > 0)
    def _():
        ext[pl.ds(0, HALO), :] = ext[pl.ds(rows, HALO), :]

    u = cc_ref[...] * ch_ref[...]
    ext[pl.ds(HALO, rows), :] = u
    oc_ref[...] = cb_ref[...] * _short_conv(ext, rows, cw_ref[...])
    tail_ref[0] = u[rows - 8:rows, :]

    @pl.when(tt == 0)
    def _():
        st[...] = jnp.zeros(st.shape, F32)

    q, k, f = _hgrn_gates(hq_ref[...], hf_ref[...], lb_ref[...])
    logf = jnp.log(f)
    v = hi_ref[...]
    r_i =lax.broadcasted_iota(jnp.int32, (rows, rows), 0)
    c_i = lax.broadcasted_iota(jnp.int32, (rows, rows), 1)
    tri = jnp.where((r_i // HG_CHUNK == c_i // HG_CHUNK) & (c_i <= r_i), 1.0, 0.0)
    a_all = jnp.dot(tri, logf, precision=HIGHEST, preferred_element_type=F32)

    L = HG_CHUNK
    t_i = lax.broadcasted_iota(jnp.int32, (L, width), 0)
    tw = lax.broadcasted_iota(jnp.int32, (L, width), 0)
    sw_col = lax.broadcasted_iota(jnp.int32, (L, width), 1) % L
    head_ones = jnp.where(bd, 1.0, 0.0).astype(BF16)
    outs = []
    for c in range(n_chunks):
        sl = slice(c * L, (c + 1) * L)
        a, qc, kc, vc = a_all[sl], q[sl], k[sl], v[sl]
        a_end = a[L - 1:L, :]
        st_old = st[...]
        o = _dot_nt((qc * jnp.exp(a)).astype(BF16), st_old.astype(BF16))

        sw = jnp.zeros((L, width), F32)
        b = L // 2
        while b >= SUBLANES:
            nb = L // b
            rq = jnp.concatenate(
                [jnp.broadcast_to(a[j * b - 1:j * b, :], (b, width)) if j % 2 else a[j * b:(j + 1) * b, :]
                 for j in range(nb)], axis=0)
            rk = jnp.concatenate(
                [a[j * b:(j + 1) * b, :] if j % 2 else jnp.broadcast_to(a[(j + 1) * b - 1:(j + 1) * b, :], (b, width))
                 for j in range(nb)], axis=0)
            odd = (t_i // b) % 2 == 1
            ql = jnp.where(odd, qc * jnp.exp(a - rq), 0.0).astype(BF16)
            kl = jnp.where(odd, 0.0, kc * jnp.exp(rk - a))
            k_bd = jnp.concatenate([kl.astype(BF16)] * HG_HEADS, axis=0) * head_ones
            sc = _dot_nt(ql, k_bd)
            pair = ((tw // b) % 2 == 1) & (sw_col // b == tw // b - 1)
            sw = sw + jnp.where(pair, sc, 0.0)
            b //= 2
        a_g = a.reshape(L // SUBLANES, SUBLANES, width)
        r4 = jnp.broadcast_to(a_g[:, HG_LEAF - 1:HG_LEAF, :], a_g.shape).reshape(L, width)
        upper = (t_i % SUBLANES) >= HG_LEAF
        ql = jnp.where(upper, qc * jnp.exp(jnp.where(upper, a - r4, 0.0)), 0.0).astype(BF16)
        kl = jnp.where(upper, 0.0, kc * jnp.exp(jnp.where(upper, 0.0, r4 - a)))
        k_bd = jnp.concatenate([kl.astype(BF16)] * HG_HEADS, axis=0) * head_ones
        pair = ((tw % SUBLANES) >= HG_LEAF) & (sw_col // SUBLANES == tw // SUBLANES) & (sw_col % SUBLANES < HG_LEAF)
        sw = sw + jnp.where(pair, _dot_nt(ql, k_bd), 0.0)
        v_bd =jnp.concatenate([vc.astype(BF16)] * HG_HEADS, axis=0) * head_ones
        o = o + _dot(sw.astype(BF16), v_bd)

        prods = []
        vrs = []
        for delta in range(HG_LEAF):
            if delta:
                kr, ar, vr = (pltpu.roll(x.reshape(L // SUBLANES, SUBLANES, width), delta, 1).reshape(L, width)
                              for x in (kc, a, vc))
            else:
                kr, ar, vr = kc, a, vc
            live = (t_i % HG_LEAF) >= delta
            prods.append(jnp.where(live, qc * kr * jnp.exp(jnp.where(live, a - ar, 0.0)), 0.0))
            vrs.append(vr)
        sc = _dot(jnp.concatenate(prods, axis=0).astype(BF16), head_ones)
        for delta in range(HG_LEAF):
            o = o + sc[delta * L:(delta + 1) * L] * vrs[delta]
        outs.append(o)

        ke = (kc * jnp.exp(a_end - a)).astype(BF16)
        upd = _dot_tn(vc.astype(BF16), ke)
        st[...] = st_old * jnp.exp(a_end) + jnp.where(bd, upd, 0.0)

    o_all = jnp.concatenate(outs, axis=0)
    oh_ref[...] = _head_rms(o_all, ng_ref[...]) * _silu(hg_ref[...])
    st_ref[0] = st[...]


def _hgrn_prompt(h, lb, norm_g, cw, batch, seq):
    rows = 256
    nt = seq // rows

    def col(cb):
        return pl.BlockSpec((rows, COL), lambda b, t: (b * nt + t, cb))

    const = lambda shape: pl.BlockSpec(shape, lambda b, t: (0, 0))
    row_out = pl.BlockSpec((rows, COL), lambda b, t: (b * nt + t, 0))
    return pl.pallas_call(
        _hgrn_prompt_kernel,
        grid=(batch, nt),
        in_specs=[col(CB_HQ), col(CB_HF), col(CB_HI), col(CB_HG), col(CB_CB), col(CB_CC), col(CB_CH),
                  const((1, HG_WIDTH)), const((1, HG_WIDTH)), const((3, CONV_CH))],
        out_specs=[row_out, row_out,
                   pl.BlockSpec((1, HG_WIDTH, HG_WIDTH), lambda b, t: (b, 0, 0)),
                   pl.BlockSpec((1, 8, CONV_CH), lambda b, t: (b, 0, 0))],
        out_shape=[jax.ShapeDtypeStruct((batch * seq, HG_WIDTH), F32),
                   jax.ShapeDtypeStruct((batch * seq, CONV_CH), F32),
                   jax.ShapeDtypeStruct((batch, HG_WIDTH, HG_WIDTH), F32),
                   jax.ShapeDtypeStruct((batch, 8, CONV_CH), F32)],
        scratch_shapes=[pltpu.VMEM((HG_WIDTH, HG_WIDTH), F32),
                        pltpu.VMEM((HALO + rows + HALO, CONV_CH), F32)],
        compiler_params=_cparams("parallel", "arbitrary"),
        name="hgrn_prompt",
    )(h, h, h, h, h, h, h, lb.reshape(1, -1), norm_g.reshape(1, -1), cw)


def _mix_sample_kernel(hq_ref, hf_ref, hi_ref, hg_ref, cb_ref, cc_ref, ch_ref, s0_ref, cprev_ref,
                       lb_ref, ng_ref, cw_ref, oh_ref, oc_ref, s_ref, cnew_ref, fg, kg, qg):
    seq = hq_ref.shape[0]

    full = [cprev_ref[0], cprev_ref[1]] + [cc_ref[t] * ch_ref[t] for t in range(seq)]
    for t in range(seq):
        oc_ref[t] = cb_ref[t] * (cw_ref[0] * full[t] + cw_ref[1] * full[t + 1] + cw_ref[2] * full[t + 2])
    cnew_ref[0] = full[seq]
    cnew_ref[1] = full[seq + 1]

    lb = lb_ref[...]
    for t in range(seq):
        q, k, f = _hgrn_gates(hq_ref[t], hf_ref[t], lb)
        fg[t], kg[t], qg[t] = f, k, q
    v = [hi_ref[t] for t in range(seq)]

    def body(dk, o):
        s = s0_ref[dk]
        row = pl.ds(dk, 1)
        new_o = []
        for t in range(seq):
            s = fg[t, row, :] * s + kg[t, row, :] * v[t]
            new_o.append(o[t] + qg[t, row, :] * s)
        s_ref[dk] = s
        return tuple(new_o)

    o = lax.fori_loop(0, HG_DK, body, tuple(jnp.zeros(v[0].shape, F32) for _ in range(seq)))
    for t in range(seq):
        ms = jnp.mean(o[t] * o[t], axis=0, keepdims=True)
        oh_ref[t] = o[t] * lax.rsqrt(ms + EPS) * ng_ref[...] * _silu(hg_ref[t])


def _mix_sample(act_t, state_t, layer, cprev_t, lb, norm_g, cw):
    seq, _, n_req = act_t.shape
    hd = HG_DK

    def act(k):
        return pl.BlockSpec((seq, hd, n_req), lambda h: (0, k * HG_HEADS + h, 0))

    per_head = lambda lead: pl.BlockSpec((lead, hd, n_req), lambda h: (0, h, 0))
    chan = pl.BlockSpec((hd, n_req), lambda h: (h, 0))
    state_in = pl.BlockSpec((None, None, hd, hd, n_req), lambda h: (layer, h, 0, 0, 0))
    state_out = pl.BlockSpec((None, hd, hd, n_req), lambda h: (h, 0, 0, 0))
    spread = lambda a: jnp.broadcast_to(a[..., None], a.shape + (n_req,))
    return pl.pallas_call(
        _mix_sample_kernel,
        grid=(HG_HEADS,),
        in_specs=[act(k) for k in range(7)] + [state_in, per_head(2), chan, chan, per_head(3)],
        out_specs=[per_head(seq), per_head(seq), state_out, per_head(2)],
        out_shape=[jax.ShapeDtypeStruct((seq, HG_WIDTH, n_req), F32),
                   jax.ShapeDtypeStruct((seq, CONV_CH, n_req), F32),
                   jax.ShapeDtypeStruct((HG_HEADS, hd, hd, n_req), F32),
                   jax.ShapeDtypeStruct((2, CONV_CH, n_req), F32)],
        scratch_shapes=[pltpu.VMEM((seq, hd, n_req), F32)] * 3,
        compiler_params=_cparams("parallel"),
        name="mix_sample",
    )(*([act_t] * 7), state_t, cprev_t, spread(lb), spread(norm_g), spread(cw))


def _xattn_prompt_kernel(q_ref, mk_ref, mv_ref, o_ref):
    q = q_ref[...] * (XA_WIDTH // XA_HEADS) ** -0.5
    mk = mk_ref[...].astype(BF16)
    mv = mv_ref[...].astype(BF16)
    head = lax.broadcasted_iota(jnp.int32, q.shape, 1) // (XA_WIDTH // XA_HEADS)
    out = jnp.zeros(q.shape, F32)
    for hh in range(XA_HEADS):
        mine = head == hh
        s = _dot_nt(jnp.where(mine, q, 0.0).astype(BF16), mk)
        m = jnp.max(s, axis=-1, keepdims=True)
        p = jnp.exp(s - m)
        l = jnp.sum(p, axis=-1, keepdims=True)
        out = jnp.where(mine, _dot(p.astype(BF16), mv) / l, out)
    o_ref[...] = out


def _xattn_prompt(q, mkv, batch, seq):
    tm = min(ROW_TILE, seq)
    nt = seq // tm
    return pl.pallas_call(
        _xattn_prompt_kernel,
        grid=(batch, nt),
        in_specs=[pl.BlockSpec((tm, XA_WIDTH), lambda b, t: (b * nt + t, 0)),
                  pl.BlockSpec((N_MEM, XA_WIDTH), lambda b, t: (b, 0)),
                  pl.BlockSpec((N_MEM, XA_WIDTH), lambda b, t: (b, 1))],
        out_specs=pl.BlockSpec((tm, XA_WIDTH), lambda b, t: (b * nt + t, 0)),
        out_shape=jax.ShapeDtypeStruct((batch * seq, XA_WIDTH), F32),
        compiler_params=_cparams("parallel", "parallel"),
        name="xattn_prompt",
    )(q, mkv, mkv)


def _xattn_sample_kernel(qbd_ref, mk_ref, mv_ref, o_ref):
    for r in range(qbd_ref.shape[0]):
        q = (qbd_ref[r] * (XA_WIDTH // XA_HEADS) ** -0.5).astype(BF16)
        s = _dot(q, mk_ref[r].astype(BF16))
        m = jnp.max(s, axis=-1, keepdims=True)
        p = jnp.exp(s - m)
        l = jnp.sum(p, axis=-1, keepdims=True)
        o_ref[r] = _dot_nt(p.astype(BF16), mv_ref[r].astype(BF16)) / l


def _xattn_sample(qbd, mem_kt, mem_vt, layer):
    n_req, n_rows, _ = qbd.shape
    per_step = math.gcd(n_req, XATTN_REQ_PER_STEP)
    mem_spec = pl.BlockSpec((None, per_step, XA_WIDTH, N_MEM), lambda r: (layer, r, 0, 0))
    return pl.pallas_call(
        _xattn_sample_kernel,
        grid=(n_req // per_step,),
        in_specs=[pl.BlockSpec((per_step, n_rows, XA_WIDTH), lambda r: (r, 0, 0)), mem_spec, mem_spec],
        out_specs=pl.BlockSpec((per_step, n_rows, XA_WIDTH), lambda r: (r, 0, 0)),
        out_shape=jax.ShapeDtypeStruct((n_req, n_rows, XA_WIDTH), F32),
        compiler_params=_cparams("parallel"),
        name="xattn_sample",
    )(qbd, mem_kt, mem_vt)


def _ffn_prompt_kernel(x_ref, halo_ref, ox_ref, ox_halo_ref, wxo_ref, g_ref, wup_ref, cw_ref, wdn_ref, gf_ref,
                       o_ref, u_ref, xn_sc, ext_a, ext_b, hmid, acc, *, seq, final_norm):
    rows = x_ref.shape[0]
    wxo = wxo_ref[...]
    x = x_ref[...] + _dot(ox_ref[...].astype(BF16), wxo)
    g = g_ref[...]
    x_halo = halo_ref[...] + _dot(ox_halo_ref[...].astype(BF16), wxo)
    first = (pl.program_id(0) * rows) % seq == 0
    xn_sc[pl.ds(0, HALO), :] = jnp.where(first, 0.0, _rms(x_halo, g)).astype(BF16)
    xn_sc[pl.ds(HALO, rows), :] = _rms(x, g).astype(BF16)
    xn = xn_sc[...]


    n_chunks = D_FF // COL

    def up(j):
        for half, ext in ((0, ext_a), (1, ext_b)):
            c0 = half * D_FF + j * COL
            u = _dot(xn, wup_ref[:, c0:c0 + COL])
            u_ref[0, :, c0:c0 + COL] = u[HALO + rows - 8:, :]
            ext[j % 2] = u

    def conv(ext, j, c0):
        cw = cw_ref[:, c0:c0 + COL]
        s1 = ext[j % 2, pl.ds(HALO - 1, rows), :]
        s2 = ext[j % 2, pl.ds(HALO - 2, rows), :]
        return cw[0:1, :] * s2 + cw[1:2, :] * s1 + cw[2:3, :] * ext[j % 2, pl.ds(HALO, rows), :]

    def gate(j):
        hmid[j % 2] = (_silu(conv(ext_a, j, j * COL)) * conv(ext_b, j, D_FF + j * COL)).astype(BF16)

    def down(j):
        part = _dot(hmid[j % 2], wdn_ref[j * COL:(j + 1) * COL, :])
        acc[...] = part if j == 0 else acc[...] + part

    up(0)
    for j in range(n_chunks):
        if j + 1 < n_chunks:
            up(j + 1)
        gate(j)
        if j >= 1:
            down(j - 1)
    down(n_chunks - 1)
    y = x + acc[...]
    if final_norm:
        y = _rms(y, gf_ref[...])
    o_ref[...] = y


def _ffn_sample_kernel(x_ref, ox_ref, wxo_ref, p0_ref, p1_ref, g_ref, wup_ref, cw_ref, wdn_ref, gf_ref,
                       o_ref, u2_ref, u3_ref, acc, *, final_norm):
    seq, n_req, d = x_ref.shape
    ox = ox_ref[...].reshape(seq * n_req, ox_ref.shape[2])
    x = x_ref[...].reshape(seq * n_req, d) + _dot(ox.astype(BF16), wxo_ref[...])
    xn = _rms(x, g_ref[...]).astype(BF16)
    acc[...] = jnp.zeros(acc.shape, F32)

    def conv(u, c0):
        cw = cw_ref[:, pl.ds(c0, COL)]
        full = [p0_ref[:, pl.ds(c0, COL)], p1_ref[:, pl.ds(c0, COL)]]
        full += [u[t * n_req:(t + 1) * n_req, :] for t in range(seq)]
        u2_ref[:, pl.ds(c0, COL)] = full[seq]
        u3_ref[:, pl.ds(c0, COL)] = full[seq + 1]
        return jnp.concatenate(
            [cw[0:1, :] * full[t] + cw[1:2, :] * full[t + 1] + cw[2:3, :] * full[t + 2] for t in range(seq)],
            axis=0)

    def body(j, carry):
        ca = pl.multiple_of(j * COL, COL)
        cb = pl.multiple_of(D_FF + j * COL, COL)
        ya = conv(_dot(xn, wup_ref[:, pl.ds(ca, COL)]), ca)
        yb = conv(_dot(xn, wup_ref[:, pl.ds(cb, COL)]), cb)
        acc[...] += _dot((_silu(ya) * yb).astype(BF16), wdn_ref[pl.ds(ca, COL), :])
        return carry

    lax.fori_loop(0, D_FF // COL, body, 0)
    y = x + acc[...]
    if final_norm:
        y = _rms(y, gf_ref[...])
    o_ref[...] = y.reshape(seq, n_req, d)


def _resident(shape, layer=None):
    if layer is None:
        return pl.BlockSpec(shape, lambda i: (0,) * len(shape), pipeline_mode=pl.Buffered(1))
    return pl.BlockSpec((None,) + shape, lambda i: (layer,) + (0,) * len(shape), pipeline_mode=pl.Buffered(1))


def _ffn_prompt(x, ox, w_xo, g, w_up, cw, w_dn, g_final, seq, final_norm, layer):
    rows, d = x.shape
    tm = min(ROW_TILE, seq)
    n_up = w_up.shape[2]
    nx = ox.shape[1]
    tiles_per_seq = seq // tm
    x_spec = pl.BlockSpec((tm, d), lambda i: (i, 0))
    halo = lambda n: pl.BlockSpec((HALO, n), lambda i: (jnp.maximum(i * (tm // HALO) - 1, 0), 0))
    return pl.pallas_call(
        functools.partial(_ffn_prompt_kernel, seq=seq, final_norm=final_norm),
        grid=(rows // tm,),
        in_specs=[x_spec, halo(d), pl.BlockSpec((tm, nx), lambda i: (i, 0)), halo(nx), _resident((nx, d)),
                  _resident((1, d)), _resident((d, n_up), layer), _resident((3, n_up)),
                  _resident((D_FF, d), layer), _resident((1, d))],
        out_specs=[x_spec, pl.BlockSpec((1, 8, n_up), lambda i: (i // tiles_per_seq, 0, 0))],
        out_shape=[jax.ShapeDtypeStruct((rows, d), F32),
                   jax.ShapeDtypeStruct((rows // seq, 8, n_up), F32)],
        scratch_shapes=[pltpu.VMEM((HALO + tm, d), BF16),
                        pltpu.VMEM((2, HALO + tm, COL), F32),
                        pltpu.VMEM((2, HALO + tm, COL), F32),
                        pltpu.VMEM((2, tm, COL), BF16),
                        pltpu.VMEM((tm, d), F32)],
        compiler_params=_cparams("arbitrary"),
        name="ffn_prompt",
    )(x, x, ox, ox, w_xo, g.reshape(1, d), w_up, cw, w_dn, g_final.reshape(1, d))


def _ffn_sample(x, ox, w_xo, prev, g, w_up, cw, w_dn, g_final, final_norm, layer):
    seq, n_req, d = x.shape
    n_up = w_up.shape[2]
    nx = ox.shape[2]
    full = lambda shape: pl.BlockSpec(shape, lambda i: (0,) * len(shape))
    return pl.pallas_call(
        functools.partial(_ffn_sample_kernel, final_norm=final_norm),
        grid=(1,),
        in_specs=[_resident((seq, n_req, d)), _resident((seq, n_req, nx)), _resident((nx, d)),
                  _resident((n_req, n_up)), _resident((n_req, n_up)),
                  _resident((1, d)), _resident((d, n_up), layer), _resident((3, n_up)),
                  _resident((D_FF, d), layer), _resident((1, d))],
        out_specs=[full((seq, n_req, d)), full((n_req, n_up)), full((n_req, n_up))],
        out_shape=[jax.ShapeDtypeStruct((seq, n_req, d), F32),
                   jax.ShapeDtypeStruct((n_req, n_up), F32),
                   jax.ShapeDtypeStruct((n_req, n_up), F32)],
        scratch_shapes=[pltpu.VMEM((seq * n_req, d), F32)],
        compiler_params=_cparams("arbitrary"),
        name="ffn_sample",
    )(x, ox, w_xo, prev[:, 0], prev[:, 1], g.reshape(1, d), w_up, cw, w_dn, g_final.reshape(1, d))


def _head_rows(q, n_heads, group):
    n_req, seq, _ = q.shape
    n_kv = n_heads // group
    qh = q.reshape(n_req, seq, n_heads, HEAD_DIM).transpose(0, 2, 1, 3)
    onehot = jax.nn.one_hot(jnp.arange(n_heads) // group, n_kv, dtype=q.dtype)
    out = qh[:, :, :, None, :] * onehot[None, :, None, :, None]
    return out.reshape(n_req, n_heads * seq, n_kv * HEAD_DIM)


def _head_rows_inverse(o, n_heads, group, seq):
    n_req = o.shape[0]
    n_kv = n_heads // group
    o5 = o.reshape(n_req, n_heads, seq, n_kv, HEAD_DIM)
    picked = jnp.stack([o5[:, hh, :, hh // group, :] for hh in range(n_heads)], axis=2)
    return picked.reshape(n_req * seq, n_heads * HEAD_DIM)


def kernel(x_prompt, x_sample, cache_k, cache_v, cache_mem_k, cache_mem_v, state_hgrn, state_conv,
           state_ffn, page_table, mem_prompt, g_mix, w_in, w_out, hg_lb_logits, hg_norm_g, conv_w,
           g_xattn, g_mem, w_xq, w_mk, w_mv, w_xo, g_ffn, w_up, ffn_conv_w, w_down, g_final):
    B, T, D = x_prompt.shape
    DB, S, _ = x_sample.shape
    depth = w_in.shape[0]
    n_pages = page_table.shape[1]
    page = cache_k.shape[2]
    past = n_pages * page

    lb_p = jax.nn.softmax(hg_lb_logits.astype(F32), axis=0)
    lower = jnp.cumsum(lb_p, axis=0) - lb_p[0:1]

    rope_p = _rope_tables(jnp.arange(T, dtype=jnp.int32))
    rope_s = _rope_tables(jnp.tile(past + jnp.arange(S, dtype=jnp.int32), DB))

    w_in_b, w_out_b, w_xq_b, w_xo_b = (w.astype(BF16) for w in (w_in, w_out, w_xq, w_xo))
    w_up_b, w_down_b = w_up.astype(BF16), w_down.astype(BF16)
    w_mkv_b = jnp.concatenate([w_mk, w_mv], axis=-1).astype(BF16)

    def token_minor(c):
        return c.transpose(0, 1, 3, 4, 2).reshape(c.shape[0], c.shape[1], c.shape[3] * c.shape[4], c.shape[2])

    cache_k4, cache_v4 = token_minor(cache_k), token_minor(cache_v)
    mem_k4, mem_v4 = token_minor(cache_mem_k), token_minor(cache_mem_v)
    state_t = state_hgrn.transpose(0, 2, 3, 4, 1)
    mem2 = mem_prompt.reshape(B * N_MEM, D)

    xp = x_prompt.reshape(B * T, D)
    xs = x_sample.reshape(DB * S, D)
    outs = [[] for _ in range(12)]
    for l in range(depth):
        last = l == depth - 1
        h, q_t, k1, v1 = _norm_proj(xp, g_mix[l], w_in_b, l, rope=rope_p,
                                    n_rope=(ATT_WIDTH + KV_WIDTH) // COL, seq=T)
        a = _moba_prompt(h, q_t, v1, B, T)
        o_h, o_c, st_t, c_tail = _hgrn_prompt(h, lower[l], hg_norm_g[l], conv_w[l], B, T)
        st5 = st_t.reshape(B, HG_HEADS, HG_DK, HG_HEADS, HG_DK)
        h1 = jnp.stack([st5[:, hh, :, hh, :] for hh in range(HG_HEADS)], axis=1).swapaxes(-1, -2)
        c1 = c_tail[:, 6:8, :]
        w_out_parts = [w_out_b[l, :ATT_WIDTH], w_out_b[l, ATT_WIDTH:ATT_WIDTH + HG_WIDTH],
                       w_out_b[l, ATT_WIDTH + HG_WIDTH:]]
        xp, qx = _resid_proj(xp, [a, o_h, o_c], w_out_parts, g_xattn[l], w_xq_b[l], first_transposed=True)
        mkv = _norm_proj(mem2, g_mem[l], w_mkv_b, l)
        mk = mkv[:, :XA_WIDTH].reshape(B, N_MEM, XA_HEADS, XA_WIDTH // XA_HEADS)
        mv = mkv[:, XA_WIDTH:].reshape(B, N_MEM, XA_HEADS, XA_WIDTH // XA_HEADS)
        ox = _xattn_prompt(qx, mkv, B, T)
        xp, u_tail = _ffn_prompt(xp, ox, w_xo_b[l], g_ffn[l], w_up_b, ffn_conv_w[l], w_down_b,
                                 g_final, T, last, l)
        fs1 = u_tail[:, 6:8, :]

        hs = _norm_proj(xs, g_mix[l], w_in_b, l, rope=rope_s, n_rope=(ATT_WIDTH + KV_WIDTH) // COL)
        hs3 = hs.reshape(DB, S, IN_WIDTH)
        k_new = hs3[:, :, CB_K * COL:(CB_K + 1) * COL]
        v_new = hs3[:, :, CB_V * COL:(CB_V + 1) * COL]
        qbd = _head_rows(hs3[:, :, :ATT_WIDTH], ATT_HEADS, ATT_HEADS // KV_HEADS)
        o_att = _moba_sample(qbd, k_new, v_new, cache_k4, cache_v4, page_table, l)
        a_s = _head_rows_inverse(o_att, ATT_HEADS, ATT_HEADS // KV_HEADS, S)
        act_t = hs3[:, :, CB_HQ * COL:].transpose(1, 2, 0)
        oh_t, oc_t, h2, c2_t = _mix_sample(act_t, state_t, l, state_conv[l].transpose(1, 2, 0),
                                           lower[l], hg_norm_g[l], conv_w[l])
        oh_s, oc_s, c2 = (a.transpose(2, 0, 1) for a in (oh_t, oc_t, c2_t))
        xs, qxs = _resid_proj(xs, [a_s, oh_s.reshape(DB * S, HG_WIDTH), oc_s.reshape(DB * S, CONV_CH)],
                              w_out_parts, g_xattn[l], w_xq_b[l])
        qxbd = _head_rows(qxs.reshape(DB, S, XA_WIDTH), XA_HEADS, 1)
        oxs = _head_rows_inverse(_xattn_sample(qxbd, mem_k4, mem_v4, l), XA_HEADS, 1, S)
        xs_t, u_a, u_b = _ffn_sample(xs.reshape(DB, S, D).swapaxes(0, 1),
                                     oxs.reshape(DB, S, XA_WIDTH).swapaxes(0, 1), w_xo_b[l], state_ffn[l],
                                     g_ffn[l], w_up_b, ffn_conv_w[l], w_down_b, g_final, last, l)
        xs = xs_t.swapaxes(0, 1).reshape(DB * S, D)
        fs2 = jnp.stack([u_a, u_b], axis=1)

        for lst, val in zip(outs, (k1, v1, k_new.reshape(DB, S, KV_HEADS, HEAD_DIM),
                                   v_new.reshape(DB, S, KV_HEADS, HEAD_DIM), h1, h2, c1, c2, fs1, fs2, mk, mv)):
            lst.append(val)

    res = [jnp.stack(o) for o in outs]
    for j in (0, 1):
        res[j] = res[j].reshape(depth, B, KV_HEADS, HEAD_DIM, T).transpose(0, 1, 4, 2, 3)
    res[5] = res[5].transpose(0, 4, 1, 2, 3)
    return (xp.reshape(B, T, D), xs.reshape(DB, S, D)) + tuple(res)
```

```python
import functools
import math

import numpy as np
import jax
import jax.numpy as jnp
from jax import lax
from jax.experimental import pallas as pl
from jax.experimental.pallas import tpu as pltpu

F32 = jnp.float32
BF16 = jnp.bfloat16
HIGHEST = lax.Precision.HIGHEST
NEG_INF = float("-inf")
LOG2_E = 1.4426950408889634

D_MODEL = 1024
HEAD_DIM = 64
ATT_HEADS = 8
KV_HEADS = 4
ATT_WIDTH = ATT_HEADS * HEAD_DIM
KV_WIDTH = KV_HEADS * HEAD_DIM
MOBA_BLOCK = 256
MOBA_TOPK = 3
ROPE_THETA = 10000.0
HG_HEADS = 4
HG_DK = 64
HG_WIDTH = HG_HEADS * HG_DK
HG_CHUNK = 64
SUBLANES = 8
HG_LEAF = 4
CONV_CH = 256
N_MEM = 256
XA_HEADS = 4
XA_WIDTH = 256
D_FF = 2816
EPS = 1e-6
IN_WIDTH = 2816
COL = 256
LANES = 128
ROW_TILE = 512
HALO = 16
MOBA_REQ_PER_STEP = 4
XATTN_REQ_PER_STEP = 8
VMEM_LIMIT = 56 * 1024 * 1024

CB_K, CB_V, CB_HQ, CB_HF, CB_HI, CB_HG, CB_CB, CB_CC, CB_CH = 2, 3, 4, 5, 6, 7, 8, 9, 10


def _cparams(*sem):
    return pltpu.CompilerParams(dimension_semantics=sem, vmem_limit_bytes=VMEM_LIMIT)


def _rms(x, g):
    ms = jnp.mean(x * x, axis=-1, keepdims=True)
    return x * lax.rsqrt(ms + EPS) * g


def _sigmoid(z):
    return 1.0 / (1.0 + jnp.exp2(z * -LOG2_E))


def _silu(z):
    return z * _sigmoid(z)


def _dot(a, b):
    return jnp.dot(a, b, preferred_element_type=F32)


def _dot_nt(a, b, precision=None):
    return lax.dot_general(a, b, (((1,), (1,)), ((), ())), precision=precision,
                           preferred_element_type=F32)


def _dot_tn(a, b):
    return lax.dot_general(a, b, (((0,), (0,)), ((), ())), preferred_element_type=F32)


def _block_diag_mask(rows, cols, blk):
    r = lax.broadcasted_iota(jnp.int32, (rows, cols), 0) // blk
    c = lax.broadcasted_iota(jnp.int32, (rows, cols), 1) // blk
    return r == c


def _norm_proj_kernel(x_ref, g_ref, w_ref, cos_ref, sin_ref, o_ref, *t_refs, n_rope):
    xn = _rms(x_ref[...], g_ref[...]).astype(BF16)
    n_chunks = o_ref.shape[1] // COL
    if n_rope:
        cos = jnp.concatenate([cos_ref[...]] * (COL // LANES), axis=1)
        sin = jnp.concatenate([sin_ref[...]] * (COL // LANES), axis=1)
        lane = lax.broadcasted_iota(jnp.int32, cos.shape, 1)
        first_half = (lane % HEAD_DIM) < HEAD_DIM // 2
    for c in range(n_chunks):
        y = _dot(xn, w_ref[:, c * COL:(c + 1) * COL])
        if c < n_rope:
            partner = jnp.where(first_half, pltpu.roll(y, COL - HEAD_DIM // 2, 1),
                                pltpu.roll(y, HEAD_DIM // 2, 1))
            y = y * cos + partner * sin
        o_ref[:, c * COL:(c + 1) * COL] = y
        if t_refs and c < CB_HQ:
            qt_ref, kt_ref, vt_ref = t_refs
            if c < CB_K:
                qt_ref[c * COL:(c + 1) * COL, :] = y.T
            else:
                (kt_ref if c == CB_K else vt_ref)[...] = y.T


def _norm_proj(x, g, w, layer, rope=None, n_rope=0, seq=None):
    rows, d = x.shape
    n = w.shape[2]
    tm = min(ROW_TILE, rows)
    out_specs = [pl.BlockSpec((tm, n), lambda i: (i, 0))]
    out_shape = [jax.ShapeDtypeStruct((rows, n), F32)]
    if seq is not None:
        tiles = seq // tm
        kv_t = pl.BlockSpec((None, KV_WIDTH, tm), lambda i: (i // tiles, 0, i % tiles))
        out_specs += [pl.BlockSpec((ATT_WIDTH, tm), lambda i: (0, i)), kv_t, kv_t]
        out_shape += [jax.ShapeDtypeStruct((ATT_WIDTH, rows), F32),
                      jax.ShapeDtypeStruct((rows // seq, KV_WIDTH, seq), F32),
                      jax.ShapeDtypeStruct((rows // seq, KV_WIDTH, seq), F32)]
    if rope is None:
        cos = sin = jnp.zeros((tm, LANES), F32)
    else:
        cos, sin = rope
    tab_blocks = cos.shape[0] // tm
    res = pl.pallas_call(
        functools.partial(_norm_proj_kernel, n_rope=n_rope),
        grid=(rows // tm,),
        in_specs=[
            pl.BlockSpec((tm, d), lambda i: (i, 0)),
            pl.BlockSpec((1, d), lambda i: (0, 0)),
            pl.BlockSpec((None, d, n), lambda i: (layer, 0, 0)),
            pl.BlockSpec((tm, LANES), lambda i: (i % tab_blocks, 0)),
            pl.BlockSpec((tm, LANES), lambda i: (i % tab_blocks, 0)),
        ],
        out_specs=out_specs,
        out_shape=out_shape,
        compiler_params=_cparams("parallel"),
        name="norm_proj",
    )(x, g.reshape(1, d), w, cos, sin)
    return res[0] if seq is None else res


def _rope_tables(pos):
    half = HEAD_DIM // 2
    inv = ROPE_THETA ** (-jnp.arange(half, dtype=F32) / half)
    ang = pos.astype(F32)[:, None] * inv[None, :]
    cos = jnp.cos(ang)
    sin = jnp.sin(ang)
    reps = LANES // HEAD_DIM
    cos_t = jnp.concatenate([cos, cos] * reps, axis=1)
    sin_t = jnp.concatenate([-sin, sin] * reps, axis=1)
    return cos_t, sin_t


def _resid_proj_kernel(*refs, n_in, first_transposed):
    x_ref = refs[0]
    a_refs = refs[1:1 + n_in]
    w_refs = refs[1 + n_in:1 + 2 * n_in]
    g_ref, wq_ref, o_ref, q_ref = refs[1 + 2 * n_in:]
    acc = x_ref[...]
    for k, (a_ref, w_ref) in enumerate(zip(a_refs, w_refs)):
        a = a_ref[...].astype(BF16)
        acc = acc + (_dot_tn(a, w_ref[...]) if first_transposed and k == 0 else _dot(a, w_ref[...]))
    o_ref[...] = acc
    q_ref[...] = _dot(_rms(acc, g_ref[...]).astype(BF16), wq_ref[...])


def _resid_proj(x, acts, weights, g, wq, first_transposed=False):
    rows, d = x.shape
    tm = min(ROW_TILE, rows)
    n_in = len(acts)
    nq = wq.shape[1]
    row = lambda n: pl.BlockSpec((tm, n), lambda i: (i, 0))
    in_specs = [row(d)] + [row(a.shape[1]) for a in acts]
    if first_transposed:
        in_specs[1] = pl.BlockSpec((acts[0].shape[0], tm), lambda i: (0, i))
    in_specs += [pl.BlockSpec(w.shape, lambda i: (0, 0)) for w in weights]
    in_specs += [pl.BlockSpec((1, d), lambda i: (0, 0)), pl.BlockSpec(wq.shape, lambda i: (0, 0))]
    return pl.pallas_call(
        functools.partial(_resid_proj_kernel, n_in=n_in, first_transposed=first_transposed),
        grid=(rows // tm,),
        in_specs=in_specs,
        out_specs=[row(d), row(nq)],
        out_shape=[jax.ShapeDtypeStruct((rows, d), F32), jax.ShapeDtypeStruct((rows, nq), F32)],
        compiler_params=_cparams("parallel"),
        name="resid_proj",
    )(x, *acts, *weights, g.reshape(1, d), wq)


def _topk_rank(gate, n_cand, n_valid, axis):
    idx = lax.broadcasted_iota(jnp.int32, gate.shape, axis)
    cnt = jnp.zeros(gate.shape, F32)
    for m in range(n_cand):
        gm = gate[m:m + 1, :] if axis == 0 else gate[:, m:m + 1]
        beats = jnp.where(gm > gate, 1.0, jnp.where(gm == gate, jnp.where(m < idx, 1.0, 0.0), 0.0))
        cnt = cnt + beats * jnp.where(m < n_valid, 1.0, 0.0)
    return cnt, idx


def _moba_prompt_kernel(qt_ref, k_ref, vt_ref, o_ref, kbf, vt, sbuf, pbuf, acc):
    i = pl.program_id(2)
    n_blk = k_ref.shape[0] // MOBA_BLOCK
    nq = 4 * MOBA_BLOCK

    @pl.when(i == 0)
    def _():
        kbf[...] = k_ref[...].astype(BF16)
        vt[...] = vt_ref[...].astype(BF16)

    qt = qt_ref[...]
    zero = jnp.zeros((HEAD_DIM, MOBA_BLOCK), F32)
    qst = jnp.concatenate(
        [jnp.concatenate([qt[j * HEAD_DIM:(j + 1) * HEAD_DIM, :], zero] if j < 2
                         else [zero, qt[j * HEAD_DIM:(j + 1) * HEAD_DIM, :]], axis=0) for j in range(4)],
        axis=1)

    qsc = (qst * (HEAD_DIM ** -0.5 * LOG2_E)).astype(BF16)
    causal = (lax.broadcasted_iota(jnp.int32, (MOBA_BLOCK, nq), 0)
              <= lax.broadcasted_iota(jnp.int32, (MOBA_BLOCK, nq), 1) % MOBA_BLOCK)
    bidx = lax.broadcasted_iota(jnp.int32, (n_blk, nq), 0)

    def attend(own):
        blocks = [slice(n * MOBA_BLOCK, (n + 1) * MOBA_BLOCK) for n in range(own + 1)]
        cmax = []
        gate = jnp.zeros((n_blk, nq), F32)
        for n, rows in enumerate(blocks):
            s = _dot(kbf[rows, :], qsc)
            if n == own:
                s = jnp.where(causal, s, NEG_INF)
            else:
                gate = jnp.where(bidx == n, jnp.sum(s, axis=0, keepdims=True), gate)
            sbuf[rows, :] = s
            cmax.append(jnp.max(s, axis=0, keepdims=True))
        if own > MOBA_TOPK:
            cnt, _ = _topk_rank(gate, own, own, 0)
            bias = jnp.where((bidx < own) & (cnt < MOBA_TOPK), 0.0, NEG_INF)
        else:
            bias = jnp.zeros((n_blk, nq), F32)
        m = cmax[own]
        for n in range(own):
            m = jnp.maximum(m, cmax[n] + bias[n:n + 1, :])
        l = jnp.zeros((1, nq), F32)
        out = jnp.zeros((LANES, nq), F32)
        for n, rows in enumerate(blocks):
            shift = m if n == own else m - bias[n:n + 1, :]
            p = jnp.exp2(sbuf[rows, :] - shift)
            l = l + jnp.sum(p, axis=0, keepdims=True)
            pbuf[rows, :] = p.astype(BF16)
            out = out + _dot(vt[:, rows], pbuf[rows, :])
        acc[...] = out / l

    for own in range(n_blk):
        pl.when(i == own)(functools.partial(attend, own))

    for j in range(4):
        r0 = 0 if j < 2 else HEAD_DIM
        o_ref[j * HEAD_DIM:(j + 1) * HEAD_DIM, :] = acc[r0:r0 + HEAD_DIM, j * MOBA_BLOCK:(j + 1) * MOBA_BLOCK]


def _moba_prompt(h, q_t, v_t, batch, seq):
    n_blk = seq // MOBA_BLOCK
    kcol = (CB_K * COL) // LANES
    return pl.pallas_call(
        _moba_prompt_kernel,
        grid=(batch, KV_HEADS // 2, n_blk),
        in_specs=[
            pl.BlockSpec((COL, MOBA_BLOCK), lambda b, p, i: (p, b * n_blk + i)),
            pl.BlockSpec((seq, LANES), lambda b, p, i: (b, kcol + p)),
            pl.BlockSpec((None, LANES, seq), lambda b, p, i: (b, p, 0)),
        ],
        out_specs=pl.BlockSpec((COL, MOBA_BLOCK), lambda b, p, i: (p, b * n_blk + i)),
        out_shape=jax.ShapeDtypeStruct((ATT_WIDTH, batch * seq), F32),
        scratch_shapes=[
            pltpu.VMEM((seq, LANES), BF16),
            pltpu.VMEM((LANES, seq), BF16),
            pltpu.VMEM((seq, 4 * MOBA_BLOCK), F32),
            pltpu.VMEM((seq, 4 * MOBA_BLOCK), BF16),
            pltpu.VMEM((LANES, 4 * MOBA_BLOCK), F32),
        ],
        compiler_params=_cparams("parallel", "parallel", "arbitrary"),
        name="moba_prompt",
    )(q_t, h, v_t)


def _moba_sample_kernel(pt_ref, qbd_ref, kn_ref, vn_ref, *rest, n_pages, page):
    del pt_ref
    per_step = qbd_ref.shape[0]
    n_refs = per_step * n_pages
    o_ref = rest[2 * n_refs]
    kc_all, vc_all = rest[2 * n_refs + 1:]
    for g in range(per_step):
        _moba_sample_request(qbd_ref.at[g], kn_ref.at[g], vn_ref.at[g],
                             rest[g * n_pages:(g + 1) * n_pages],
                             rest[n_refs + g * n_pages:n_refs + (g + 1) * n_pages],
                             o_ref.at[g], kc_all.at[g], vc_all.at[g], page)


def _moba_sample_request(qbd_ref, kn_ref, vn_ref, kp, vp, o_ref, kc, vc, page):
    n_pages = len(kp)
    n_blk = n_pages * page // MOBA_BLOCK
    n_rows = qbd_ref.shape[0]
    seq_new = kn_ref.shape[0]

    for pg in range(n_pages):
        kc[:, pg * page:(pg + 1) * page] = kp[pg][...].astype(BF16)
        vc[:, pg * page:(pg + 1) * page] = vp[pg][...].astype(BF16)

    qbd = qbd_ref[...]
    qs = qbd * HEAD_DIM ** -0.5
    s_all = _dot(qs.astype(BF16), kc[...])

    lane = lax.broadcasted_iota(jnp.int32, (n_rows, LANES), 1)
    gate = jnp.zeros((n_rows, LANES), F32)
    for n in range(n_blk):
        gate = jnp.where(lane == n, jnp.mean(s_all[:, n * MOBA_BLOCK:(n + 1) * MOBA_BLOCK], axis=-1, keepdims=True),
                         gate)
    cnt, bidx = _topk_rank(gate, n_blk, n_blk, 1)
    bias = jnp.where((bidx < n_blk) & (cnt < MOBA_TOPK), 0.0, NEG_INF)
    s_blk = [s_all[:, n * MOBA_BLOCK:(n + 1) * MOBA_BLOCK] + bias[:, n:n + 1] for n in range(n_blk)]

    kn = kn_ref[...]
    vn = vn_ref[...]
    tok = lax.broadcasted_iota(jnp.int32, (n_rows, 1), 0) % seq_new
    s_own = []
    for t in range(seq_new):
        st = jnp.sum(qs * kn[t:t + 1, :], axis=-1, keepdims=True)
        s_own.append(jnp.where(t <= tok, st, NEG_INF))

    m = s_own[0]
    for st in s_own[1:]:
        m = jnp.maximum(m, st)
    for sb in s_blk:
        m = jnp.maximum(m, jnp.max(sb, axis=-1, keepdims=True))

    l = jnp.zeros((n_rows, 1), F32)
    out = jnp.zeros((n_rows, KV_WIDTH), F32)
    for t in range(seq_new):
        pt = jnp.exp(s_own[t] - m)
        l = l + pt
        out = out + pt * vn[t:t + 1, :]
    for n in range(n_blk):
        pn = jnp.exp(s_blk[n] - m)
        l = l + jnp.sum(pn, axis=-1, keepdims=True)
        out = out + _dot_nt(pn.astype(BF16), vc[:, n * MOBA_BLOCK:(n + 1) * MOBA_BLOCK])
    o_ref[...] = out / l


def _moba_sample(qbd, k_new, v_new, cache_kt, cache_vt, page_table, layer):
    n_req, n_pages = page_table.shape
    depth, n_pool, _, page = cache_kt.shape
    n_rows = qbd.shape[1]
    seq_new = k_new.shape[1]

    per_step = math.gcd(n_req, MOBA_REQ_PER_STEP)
    n_refs = per_step * n_pages
    cache_kt = cache_kt.reshape(depth * n_pool, KV_WIDTH, page)
    cache_vt = cache_vt.reshape(depth * n_pool, KV_WIDTH, page)
    page_table = page_table + layer * n_pool

    def page_spec(j):
        return pl.BlockSpec((None, KV_WIDTH, page), lambda r, pt: (pt[r * n_refs + j], 0, 0))

    in_specs = [
        pl.BlockSpec((per_step, n_rows, KV_WIDTH), lambda r, pt: (r, 0, 0)),
        pl.BlockSpec((per_step, seq_new, KV_WIDTH), lambda r, pt: (r, 0, 0)),
        pl.BlockSpec((per_step, seq_new, KV_WIDTH), lambda r, pt: (r, 0, 0)),
    ]
    in_specs += [page_spec(j) for j in range(n_refs)]
    in_specs += [page_spec(j) for j in range(n_refs)]
    grid_spec = pltpu.PrefetchScalarGridSpec(
        num_scalar_prefetch=1,
        grid=(n_req // per_step,),
        in_specs=in_specs,
        out_specs=pl.BlockSpec((per_step, n_rows, KV_WIDTH), lambda r, pt: (r, 0, 0)),
        scratch_shapes=[
            pltpu.VMEM((per_step, KV_WIDTH, n_pages * page), BF16),
            pltpu.VMEM((per_step, KV_WIDTH, n_pages * page), BF16),
        ],
    )
    return pl.pallas_call(
        functools.partial(_moba_sample_kernel, n_pages=n_pages, page=page),
        grid_spec=grid_spec,
        out_shape=jax.ShapeDtypeStruct((n_req, n_rows, KV_WIDTH), F32),
        compiler_params=_cparams("arbitrary"),
        name="moba_sample",
    )(page_table.reshape(-1), qbd, k_new, v_new, *([cache_kt] * n_refs), *([cache_vt] * n_refs))


def _hgrn_gates(hq, hf, lb):
    q = _silu(hq)
    f = lb + (1.0 - lb) * _sigmoid(hf)
    k = (1.0 - lb) * _sigmoid(-hf)
    return q, k, f


def _head_rms(o, norm_g):
    head_mean = jnp.where(_block_diag_mask(HG_WIDTH, HG_WIDTH, HG_DK), 1.0 / HG_DK, 0.0)
    ms = jnp.dot(o * o, head_mean, precision=HIGHEST, preferred_element_type=F32)
    return o * lax.rsqrt(ms + EPS) * norm_g


def _short_conv(ext_ref, rows, cw):
    return (cw[0:1, :] * ext_ref[pl.ds(HALO - 2, rows), :]
            + cw[1:2, :] * ext_ref[pl.ds(HALO - 1, rows), :]
            + cw[2:3, :] * ext_ref[pl.ds(HALO, rows), :])


def _hgrn_prompt_kernel(hq_ref, hf_ref, hi_ref, hg_ref, cb_ref, cc_ref, ch_ref, lb_ref, ng_ref, cw_ref,
                        oh_ref, oc_ref, st_ref, tail_ref, st, ext):
    tt = pl.program_id(1)
    rows = hq_ref.shape[0]
    n_chunks = rows // HG_CHUNK
    width = HG_WIDTH
    bd = _block_diag_mask(width, width, HG_DK)

    @pl.when(tt == 0)
    def _():
        ext[pl.ds(0, HALO), :] = jnp.zeros((HALO, CONV_CH), F32)

    @pl.when(tt > 0)
    def _():
        ext[pl.ds(0, HALO), :] = ext[pl.ds(rows, HALO), :]

    u = cc_ref[...] * ch_ref[...]
    ext[pl.ds(HALO, rows), :] = u
    oc_ref[...] = cb_ref[...] * _short_conv(ext, rows, cw_ref[...])
    tail_ref[0] = u[rows - 8:rows, :]

    @pl.when(tt == 0)
    def _():
        st[...] = jnp.zeros(st.shape, F32)

    q, k, f = _hgrn_gates(hq_ref[...], hf_ref[...], lb_ref[...])
    logf = jnp.log(f)
    v = hi_ref[...]
    r_i =lax.broadcasted_iota(jnp.int32, (rows, rows), 0)
    c_i = lax.broadcasted_iota(jnp.int32, (rows, rows), 1)
    tri = jnp.where((r_i // HG_CHUNK == c_i // HG_CHUNK) & (c_i <= r_i), 1.0, 0.0)
    a_all = jnp.dot(tri, logf, precision=HIGHEST, preferred_element_type=F32)

    L = HG_CHUNK
    t_i = lax.broadcasted_iota(jnp.int32, (L, width), 0)
    tw = lax.broadcasted_iota(jnp.int32, (L, width), 0)
    sw_col = lax.broadcasted_iota(jnp.int32, (L, width), 1) % L
    head_ones = jnp.where(bd, 1.0, 0.0).astype(BF16)
    outs = []
    for c in range(n_chunks):
        sl = slice(c * L, (c + 1) * L)
        a, qc, kc, vc = a_all[sl], q[sl], k[sl], v[sl]
        a_end = a[L - 1:L, :]
        st_old = st[...]
        o = _dot_nt((qc * jnp.exp(a)).astype(BF16), st_old.astype(BF16))

        sw = jnp.zeros((L, width), F32)
        b = L // 2
        while b >= SUBLANES:
            nb = L // b
            rq = jnp.concatenate(
                [jnp.broadcast_to(a[j * b - 1:j * b, :], (b, width)) if j % 2 else a[j * b:(j + 1) * b, :]
                 for j in range(nb)], axis=0)
            rk = jnp.concatenate(
                [a[j * b:(j + 1) * b, :] if j % 2 else jnp.broadcast_to(a[(j + 1) * b - 1:(j + 1) * b, :], (b, width))
                 for j in range(nb)], axis=0)
            odd = (t_i // b) % 2 == 1
            ql = jnp.where(odd, qc * jnp.exp(a - rq), 0.0).astype(BF16)
            kl = jnp.where(odd, 0.0, kc * jnp.exp(rk - a))
            k_bd = jnp.concatenate([kl.astype(BF16)] * HG_HEADS, axis=0) * head_ones
            sc = _dot_nt(ql, k_bd)
            pair = ((tw // b) % 2 == 1) & (sw_col // b == tw // b - 1)
            sw = sw + jnp.where(pair, sc, 0.0)
            b //= 2
        a_g = a.reshape(L // SUBLANES, SUBLANES, width)
        r4 = jnp.broadcast_to(a_g[:, HG_LEAF - 1:HG_LEAF, :], a_g.shape).reshape(L, width)
        upper = (t_i % SUBLANES) >= HG_LEAF
        ql = jnp.where(upper, qc * jnp.exp(jnp.where(upper, a - r4, 0.0)), 0.0).astype(BF16)
        kl = jnp.where(upper, 0.0, kc * jnp.exp(jnp.where(upper, 0.0, r4 - a)))
        k_bd = jnp.concatenate([kl.astype(BF16)] * HG_HEADS, axis=0) * head_ones
        pair = ((tw % SUBLANES) >= HG_LEAF) & (sw_col // SUBLANES == tw // SUBLANES) & (sw_col % SUBLANES < HG_LEAF)
        sw = sw + jnp.where(pair, _dot_nt(ql, k_bd), 0.0)
        v_bd =jnp.concatenate([vc.astype(BF16)] * HG_HEADS, axis=0) * head_ones
        o = o + _dot(sw.astype(BF16), v_bd)

        prods = []
        vrs = []
        for delta in range(HG_LEAF):
            if delta:
                kr, ar, vr = (pltpu.roll(x.reshape(L // SUBLANES, SUBLANES, width), delta, 1).reshape(L, width)
                              for x in (kc, a, vc))
            else:
                kr, ar, vr = kc, a, vc
            live = (t_i % HG_LEAF) >= delta
            prods.append(jnp.where(live, qc * kr * jnp.exp(jnp.where(live, a - ar, 0.0)), 0.0))
            vrs.append(vr)
        sc = _dot(jnp.concatenate(prods, axis=0).astype(BF16), head_ones)
        for delta in range(HG_LEAF):
            o = o + sc[delta * L:(delta + 1) * L] * vrs[delta]
        outs.append(o)

        ke = (kc * jnp.exp(a_end - a)).astype(BF16)
        upd = _dot_tn(vc.astype(BF16), ke)
        st[...] = st_old * jnp.exp(a_end) + jnp.where(bd, upd, 0.0)

    o_all = jnp.concatenate(outs, axis=0)
    oh_ref[...] = _head_rms(o_all, ng_ref[...]) * _silu(hg_ref[...])
    st_ref[0] = st[...]


def _hgrn_prompt(h, lb, norm_g, cw, batch, seq):
    rows = 256
    nt = seq // rows

    def col(cb):
        return pl.BlockSpec((rows, COL), lambda b, t: (b * nt + t, cb))

    const = lambda shape: pl.BlockSpec(shape, lambda b, t: (0, 0))
    row_out = pl.BlockSpec((rows, COL), lambda b, t: (b * nt + t, 0))
    return pl.pallas_call(
        _hgrn_prompt_kernel,
        grid=(batch, nt),
        in_specs=[col(CB_HQ), col(CB_HF), col(CB_HI), col(CB_HG), col(CB_CB), col(CB_CC), col(CB_CH),
                  const((1, HG_WIDTH)), const((1, HG_WIDTH)), const((3, CONV_CH))],
        out_specs=[row_out, row_out,
                   pl.BlockSpec((1, HG_WIDTH, HG_WIDTH), lambda b, t: (b, 0, 0)),
                   pl.BlockSpec((1, 8, CONV_CH), lambda b, t: (b, 0, 0))],
        out_shape=[jax.ShapeDtypeStruct((batch * seq, HG_WIDTH), F32),
                   jax.ShapeDtypeStruct((batch * seq, CONV_CH), F32),
                   jax.ShapeDtypeStruct((batch, HG_WIDTH, HG_WIDTH), F32),
                   jax.ShapeDtypeStruct((batch, 8, CONV_CH), F32)],
        scratch_shapes=[pltpu.VMEM((HG_WIDTH, HG_WIDTH), F32),
                        pltpu.VMEM((HALO + rows + HALO, CONV_CH), F32)],
        compiler_params=_cparams("parallel", "arbitrary"),
        name="hgrn_prompt",
    )(h, h, h, h, h, h, h, lb.reshape(1, -1), norm_g.reshape(1, -1), cw)


def _mix_sample_kernel(hq_ref, hf_ref, hi_ref, hg_ref, cb_ref, cc_ref, ch_ref, s0_ref, cprev_ref,
                       lb_ref, ng_ref, cw_ref, oh_ref, oc_ref, s_ref, cnew_ref, fg, kg, qg):
    seq = hq_ref.shape[0]

    full = [cprev_ref[0], cprev_ref[1]] + [cc_ref[t] * ch_ref[t] for t in range(seq)]
    for t in range(seq):
        oc_ref[t] = cb_ref[t] * (cw_ref[0] * full[t] + cw_ref[1] * full[t + 1] + cw_ref[2] * full[t + 2])
    cnew_ref[0] = full[seq]
    cnew_ref[1] = full[seq + 1]

    lb = lb_ref[...]
    for t in range(seq):
        q, k, f = _hgrn_gates(hq_ref[t], hf_ref[t], lb)
        fg[t], kg[t], qg[t] = f, k, q
    v = [hi_ref[t] for t in range(seq)]

    def body(dk, o):
        s = s0_ref[dk]
        row = pl.ds(dk, 1)
        new_o = []
        for t in range(seq):
            s = fg[t, row, :] * s + kg[t, row, :] * v[t]
            new_o.append(o[t] + qg[t, row, :] * s)
        s_ref[dk] = s
        return tuple(new_o)

    o = lax.fori_loop(0, HG_DK, body, tuple(jnp.zeros(v[0].shape, F32) for _ in range(seq)))
    for t in range(seq):
        ms = jnp.mean(o[t] * o[t], axis=0, keepdims=True)
        oh_ref[t] = o[t] * lax.rsqrt(ms + EPS) * ng_ref[...] * _silu(hg_ref[t])


def _mix_sample(act_t, state_t, layer, cprev_t, lb, norm_g, cw):
    seq, _, n_req = act_t.shape
    hd = HG_DK

    def act(k):
        return pl.BlockSpec((seq, hd, n_req), lambda h: (0, k * HG_HEADS + h, 0))

    per_head = lambda lead: pl.BlockSpec((lead, hd, n_req), lambda h: (0, h, 0))
    chan = pl.BlockSpec((hd, n_req), lambda h: (h, 0))
    state_in = pl.BlockSpec((None, None, hd, hd, n_req), lambda h: (layer, h, 0, 0, 0))
    state_out = pl.BlockSpec((None, hd, hd, n_req), lambda h: (h, 0, 0, 0))
    spread = lambda a: jnp.broadcast_to(a[..., None], a.shape + (n_req,))
    return pl.pallas_call(
        _mix_sample_kernel,
        grid=(HG_HEADS,),
        in_specs=[act(k) for k in range(7)] + [state_in, per_head(2), chan, chan, per_head(3)],
        out_specs=[per_head(seq), per_head(seq), state_out, per_head(2)],
        out_shape=[jax.ShapeDtypeStruct((seq, HG_WIDTH, n_req), F32),
                   jax.ShapeDtypeStruct((seq, CONV_CH, n_req), F32),
                   jax.ShapeDtypeStruct((HG_HEADS, hd, hd, n_req), F32),
                   jax.ShapeDtypeStruct((2, CONV_CH, n_req), F32)],
        scratch_shapes=[pltpu.VMEM((seq, hd, n_req), F32)] * 3,
        compiler_params=_cparams("parallel"),
        name="mix_sample",
    )(*([act_t] * 7), state_t, cprev_t, spread(lb), spread(norm_g), spread(cw))


def _xattn_prompt_kernel(q_ref, mk_ref, mv_ref, o_ref):
    q = q_ref[...] * (XA_WIDTH // XA_HEADS) ** -0.5
    mk = mk_ref[...].astype(BF16)
    mv = mv_ref[...].astype(BF16)
    head = lax.broadcasted_iota(jnp.int32, q.shape, 1) // (XA_WIDTH // XA_HEADS)
    out = jnp.zeros(q.shape, F32)
    for hh in range(XA_HEADS):
        mine = head == hh
        s = _dot_nt(jnp.where(mine, q, 0.0).astype(BF16), mk)
        m = jnp.max(s, axis=-1, keepdims=True)
        p = jnp.exp(s - m)
        l = jnp.sum(p, axis=-1, keepdims=True)
        out = jnp.where(mine, _dot(p.astype(BF16), mv) / l, out)
    o_ref[...] = out


def _xattn_prompt(q, mkv, batch, seq):
    tm = min(ROW_TILE, seq)
    nt = seq // tm
    return pl.pallas_call(
        _xattn_prompt_kernel,
        grid=(batch, nt),
        in_specs=[pl.BlockSpec((tm, XA_WIDTH), lambda b, t: (b * nt + t, 0)),
                  pl.BlockSpec((N_MEM, XA_WIDTH), lambda b, t: (b, 0)),
                  pl.BlockSpec((N_MEM, XA_WIDTH), lambda b, t: (b, 1))],
        out_specs=pl.BlockSpec((tm, XA_WIDTH), lambda b, t: (b * nt + t, 0)),
        out_shape=jax.ShapeDtypeStruct((batch * seq, XA_WIDTH), F32),
        compiler_params=_cparams("parallel", "parallel"),
        name="xattn_prompt",
    )(q, mkv, mkv)


def _xattn_sample_kernel(qbd_ref, mk_ref, mv_ref, o_ref):
    for r in range(qbd_ref.shape[0]):
        q = (qbd_ref[r] * (XA_WIDTH // XA_HEADS) ** -0.5).astype(BF16)
        s = _dot(q, mk_ref[r].astype(BF16))
        m = jnp.max(s, axis=-1, keepdims=True)
        p = jnp.exp(s - m)
        l = jnp.sum(p, axis=-1, keepdims=True)
        o_ref[r] = _dot_nt(p.astype(BF16), mv_ref[r].astype(BF16)) / l


def _xattn_sample(qbd, mem_kt, mem_vt, layer):
    n_req, n_rows, _ = qbd.shape
    per_step = math.gcd(n_req, XATTN_REQ_PER_STEP)
    mem_spec = pl.BlockSpec((None, per_step, XA_WIDTH, N_MEM), lambda r: (layer, r, 0, 0))
    return pl.pallas_call(
        _xattn_sample_kernel,
        grid=(n_req // per_step,),
        in_specs=[pl.BlockSpec((per_step, n_rows, XA_WIDTH), lambda r: (r, 0, 0)), mem_spec, mem_spec],
        out_specs=pl.BlockSpec((per_step, n_rows, XA_WIDTH), lambda r: (r, 0, 0)),
        out_shape=jax.ShapeDtypeStruct((n_req, n_rows, XA_WIDTH), F32),
        compiler_params=_cparams("parallel"),
        name="xattn_sample",
    )(qbd, mem_kt, mem_vt)


def _ffn_prompt_kernel(x_ref, halo_ref, ox_ref, ox_halo_ref, wxo_ref, g_ref, wup_ref, cw_ref, wdn_ref, gf_ref,
                       o_ref, u_ref, xn_sc, ext_a, ext_b, hmid, acc, *, seq, final_norm):
    rows = x_ref.shape[0]
    wxo = wxo_ref[...]
    x = x_ref[...] + _dot(ox_ref[...].astype(BF16), wxo)
    g = g_ref[...]
    x_halo = halo_ref[...] + _dot(ox_halo_ref[...].astype(BF16), wxo)
    first = (pl.program_id(0) * rows) % seq == 0
    xn_sc[pl.ds(0, HALO), :] = jnp.where(first, 0.0, _rms(x_halo, g)).astype(BF16)
    xn_sc[pl.ds(HALO, rows), :] = _rms(x, g).astype(BF16)
    xn = xn_sc[...]


    n_chunks = D_FF // COL

    def up(j):
        for half, ext in ((0, ext_a), (1, ext_b)):
            c0 = half * D_FF + j * COL
            u = _dot(xn, wup_ref[:, c0:c0 + COL])
            u_ref[0, :, c0:c0 + COL] = u[HALO + rows - 8:, :]
            ext[j % 2] = u

    def conv(ext, j, c0):
        cw = cw_ref[:, c0:c0 + COL]
        s1 = ext[j % 2, pl.ds(HALO - 1, rows), :]
        s2 = ext[j % 2, pl.ds(HALO - 2, rows), :]
        return cw[0:1, :] * s2 + cw[1:2, :] * s1 + cw[2:3, :] * ext[j % 2, pl.ds(HALO, rows), :]

    def gate(j):
        hmid[j % 2] = (_silu(conv(ext_a, j, j * COL)) * conv(ext_b, j, D_FF + j * COL)).astype(BF16)

    def down(j):
        part = _dot(hmid[j % 2], wdn_ref[j * COL:(j + 1) * COL, :])
        acc[...] = part if j == 0 else acc[...] + part

    up(0)
    for j in range(n_chunks):
        if j + 1 < n_chunks:
            up(j + 1)
        gate(j)
        if j >= 1:
            down(j - 1)
    down(n_chunks - 1)
    y = x + acc[...]
    if final_norm:
        y = _rms(y, gf_ref[...])
    o_ref[...] = y


def _ffn_sample_kernel(x_ref, ox_ref, wxo_ref, p0_ref, p1_ref, g_ref, wup_ref, cw_ref, wdn_ref, gf_ref,
                       o_ref, u2_ref, u3_ref, acc, *, final_norm):
    seq, n_req, d = x_ref.shape
    ox = ox_ref[...].reshape(seq * n_req, ox_ref.shape[2])
    x = x_ref[...].reshape(seq * n_req, d) + _dot(ox.astype(BF16), wxo_ref[...])
    xn = _rms(x, g_ref[...]).astype(BF16)
    acc[...] = jnp.zeros(acc.shape, F32)

    def conv(u, c0):
        cw = cw_ref[:, pl.ds(c0, COL)]
        full = [p0_ref[:, pl.ds(c0, COL)], p1_ref[:, pl.ds(c0, COL)]]
        full += [u[t * n_req:(t + 1) * n_req, :] for t in range(seq)]
        u2_ref[:, pl.ds(c0, COL)] = full[seq]
        u3_ref[:, pl.ds(c0, COL)] = full[seq + 1]
        return jnp.concatenate(
            [cw[0:1, :] * full[t] + cw[1:2, :] * full[t + 1] + cw[2:3, :] * full[t + 2] for t in range(seq)],
            axis=0)

    def body(j, carry):
        ca = pl.multiple_of(j * COL, COL)
        cb = pl.multiple_of(D_FF + j * COL, COL)
        ya = conv(_dot(xn, wup_ref[:, pl.ds(ca, COL)]), ca)
        yb = conv(_dot(xn, wup_ref[:, pl.ds(cb, COL)]), cb)
        acc[...] += _dot((_silu(ya) * yb).astype(BF16), wdn_ref[pl.ds(ca, COL), :])
        return carry

    lax.fori_loop(0, D_FF // COL, body, 0)
    y = x + acc[...]
    if final_norm:
        y = _rms(y, gf_ref[...])
    o_ref[...] = y.reshape(seq, n_req, d)


def _resident(shape, layer=None):
    if layer is None:
        return pl.BlockSpec(shape, lambda i: (0,) * len(shape), pipeline_mode=pl.Buffered(1))
    return pl.BlockSpec((None,) + shape, lambda i: (layer,) + (0,) * len(shape), pipeline_mode=pl.Buffered(1))


def _ffn_prompt(x, ox, w_xo, g, w_up, cw, w_dn, g_final, seq, final_norm, layer):
    rows, d = x.shape
    tm = min(ROW_TILE, seq)
    n_up = w_up.shape[2]
    nx = ox.shape[1]
    tiles_per_seq = seq // tm
    x_spec = pl.BlockSpec((tm, d), lambda i: (i, 0))
    halo = lambda n: pl.BlockSpec((HALO, n), lambda i: (jnp.maximum(i * (tm // HALO) - 1, 0), 0))
    return pl.pallas_call(
        functools.partial(_ffn_prompt_kernel, seq=seq, final_norm=final_norm),
        grid=(rows // tm,),
        in_specs=[x_spec, halo(d), pl.BlockSpec((tm, nx), lambda i: (i, 0)), halo(nx), _resident((nx, d)),
                  _resident((1, d)), _resident((d, n_up), layer), _resident((3, n_up)),
                  _resident((D_FF, d), layer), _resident((1, d))],
        out_specs=[x_spec, pl.BlockSpec((1, 8, n_up), lambda i: (i // tiles_per_seq, 0, 0))],
        out_shape=[jax.ShapeDtypeStruct((rows, d), F32),
                   jax.ShapeDtypeStruct((rows // seq, 8, n_up), F32)],
        scratch_shapes=[pltpu.VMEM((HALO + tm, d), BF16),
                        pltpu.VMEM((2, HALO + tm, COL), F32),
                        pltpu.VMEM((2, HALO + tm, COL), F32),
                        pltpu.VMEM((2, tm, COL), BF16),
                        pltpu.VMEM((tm, d), F32)],
        compiler_params=_cparams("arbitrary"),
        name="ffn_prompt",
    )(x, x, ox, ox, w_xo, g.reshape(1, d), w_up, cw, w_dn, g_final.reshape(1, d))


def _ffn_sample(x, ox, w_xo, prev, g, w_up, cw, w_dn, g_final, final_norm, layer):
    seq, n_req, d = x.shape
    n_up = w_up.shape[2]
    nx = ox.shape[2]
    full = lambda shape: pl.BlockSpec(shape, lambda i: (0,) * len(shape))
    return pl.pallas_call(
        functools.partial(_ffn_sample_kernel, final_norm=final_norm),
        grid=(1,),
        in_specs=[_resident((seq, n_req, d)), _resident((seq, n_req, nx)), _resident((nx, d)),
                  _resident((n_req, n_up)), _resident((n_req, n_up)),
                  _resident((1, d)), _resident((d, n_up), layer), _resident((3, n_up)),
                  _resident((D_FF, d), layer), _resident((1, d))],
        out_specs=[full((seq, n_req, d)), full((n_req, n_up)), full((n_req, n_up))],
        out_shape=[jax.ShapeDtypeStruct((seq, n_req, d), F32),
                   jax.ShapeDtypeStruct((n_req, n_up), F32),
                   jax.ShapeDtypeStruct((n_req, n_up), F32)],
        scratch_shapes=[pltpu.VMEM((seq * n_req, d), F32)],
        compiler_params=_cparams("arbitrary"),
        name="ffn_sample",
    )(x, ox, w_xo, prev[:, 0], prev[:, 1], g.reshape(1, d), w_up, cw, w_dn, g_final.reshape(1, d))


def _head_rows(q, n_heads, group):
    n_req, seq, _ = q.shape
    n_kv = n_heads // group
    qh = q.reshape(n_req, seq, n_heads, HEAD_DIM).transpose(0, 2, 1, 3)
    onehot = jax.nn.one_hot(jnp.arange(n_heads) // group, n_kv, dtype=q.dtype)
    out = qh[:, :, :, None, :] * onehot[None, :, None, :, None]
    return out.reshape(n_req, n_heads * seq, n_kv * HEAD_DIM)


def _head_rows_inverse(o, n_heads, group, seq):
    n_req = o.shape[0]
    n_kv = n_heads // group
    o5 = o.reshape(n_req, n_heads, seq, n_kv, HEAD_DIM)
    onehot = jax.nn.one_hot(jnp.arange(n_heads) // group, n_kv, dtype=o.dtype)
    picked = (o5 * onehot[None, :, None, :, None]).sum(axis=3)
    return picked.transpose(0, 2, 1, 3).reshape(n_req * seq, n_heads * HEAD_DIM)


def kernel(x_prompt, x_sample, cache_k, cache_v, cache_mem_k, cache_mem_v, state_hgrn, state_conv,
           state_ffn, page_table, mem_prompt, g_mix, w_in, w_out, hg_lb_logits, hg_norm_g, conv_w,
           g_xattn, g_mem, w_xq, w_mk, w_mv, w_xo, g_ffn, w_up, ffn_conv_w, w_down, g_final):
    B, T, D = x_prompt.shape
    DB, S, _ = x_sample.shape
    depth = w_in.shape[0]
    n_pages = page_table.shape[1]
    page = cache_k.shape[2]
    past = n_pages * page

    lb_p = jax.nn.softmax(hg_lb_logits.astype(F32), axis=0)
    lower = jnp.cumsum(lb_p, axis=0) - lb_p[0:1]

    rope_p = _rope_tables(jnp.arange(T, dtype=jnp.int32))
    rope_s = _rope_tables(jnp.tile(past + jnp.arange(S, dtype=jnp.int32), DB))

    w_in_b, w_out_b, w_xq_b, w_xo_b = (w.astype(BF16) for w in (w_in, w_out, w_xq, w_xo))
    w_up_b, w_down_b = w_up.astype(BF16), w_down.astype(BF16)
    w_mkv_b = jnp.concatenate([w_mk, w_mv], axis=-1).astype(BF16)

    def token_minor(c):
        return c.transpose(0, 1, 3, 4, 2).reshape(c.shape[0], c.shape[1], c.shape[3] * c.shape[4], c.shape[2])

    cache_k4, cache_v4 = token_minor(cache_k), token_minor(cache_v)
    mem_k4, mem_v4 = token_minor(cache_mem_k), token_minor(cache_mem_v)
    state_t = state_hgrn.transpose(0, 2, 3, 4, 1)
    mem2 = mem_prompt.reshape(B * N_MEM, D)

    xp = x_prompt.reshape(B * T, D)
    xs = x_sample.reshape(DB * S, D)
    outs = [[] for _ in range(12)]
    for l in range(depth):
        last = l == depth - 1
        h, q_t, k1, v1 = _norm_proj(xp, g_mix[l], w_in_b, l, rope=rope_p,
                                    n_rope=(ATT_WIDTH + KV_WIDTH) // COL, seq=T)
        a = _moba_prompt(h, q_t, v1, B, T)
        o_h, o_c, st_t, c_tail = _hgrn_prompt(h, lower[l], hg_norm_g[l], conv_w[l], B, T)
        st5 = st_t.reshape(B, HG_HEADS, HG_DK, HG_HEADS, HG_DK)
        h1 = jnp.stack([st5[:, hh, :, hh, :] for hh in range(HG_HEADS)], axis=1).swapaxes(-1, -2)
        c1 = c_tail[:, 6:8, :]
        w_out_parts = [w_out_b[l, :ATT_WIDTH], w_out_b[l, ATT_WIDTH:ATT_WIDTH + HG_WIDTH],
                       w_out_b[l, ATT_WIDTH + HG_WIDTH:]]
        xp, qx = _resid_proj(xp, [a, o_h, o_c], w_out_parts, g_xattn[l], w_xq_b[l], first_transposed=True)
        mkv = _norm_proj(mem2, g_mem[l], w_mkv_b, l)
        mk = mkv[:, :XA_WIDTH].reshape(B, N_MEM, XA_HEADS, XA_WIDTH // XA_HEADS)
        mv = mkv[:, XA_WIDTH:].reshape(B, N_MEM, XA_HEADS, XA_WIDTH // XA_HEADS)
        ox = _xattn_prompt(qx, mkv, B, T)
        xp, u_tail = _ffn_prompt(xp, ox, w_xo_b[l], g_ffn[l], w_up_b, ffn_conv_w[l], w_down_b,
                                 g_final, T, last, l)
        fs1 = u_tail[:, 6:8, :]

        hs = _norm_proj(xs, g_mix[l], w_in_b, l, rope=rope_s, n_rope=(ATT_WIDTH + KV_WIDTH) // COL)
        hs3 = hs.reshape(DB, S, IN_WIDTH)
        k_new = hs3[:, :, CB_K * COL:(CB_K + 1) * COL]
        v_new = hs3[:, :, CB_V * COL:(CB_V + 1) * COL]
        qbd = _head_rows(hs3[:, :, :ATT_WIDTH], ATT_HEADS, ATT_HEADS // KV_HEADS)
        o_att = _moba_sample(qbd, k_new, v_new, cache_k4, cache_v4, page_table, l)
        a_s = _head_rows_inverse(o_att, ATT_HEADS, ATT_HEADS // KV_HEADS, S)
        act_t = hs3[:, :, CB_HQ * COL:].transpose(1, 2, 0)
        oh_t, oc_t, h2, c2_t = _mix_sample(act_t, state_t, l, state_conv[l].transpose(1, 2, 0),
                                           lower[l], hg_norm_g[l], conv_w[l])
        oh_s, oc_s, c2 = (a.transpose(2, 0, 1) for a in (oh_t, oc_t, c2_t))
        xs, qxs = _resid_proj(xs, [a_s, oh_s.reshape(DB * S, HG_WIDTH), oc_s.reshape(DB * S, CONV_CH)],
                              w_out_parts, g_xattn[l], w_xq_b[l])
        qxbd = _head_rows(qxs.reshape(DB, S, XA_WIDTH), XA_HEADS, 1)
        oxs = _head_rows_inverse(_xattn_sample(qxbd, mem_k4, mem_v4, l), XA_HEADS, 1, S)
        xs_t, u_a, u_b = _ffn_sample(xs.reshape(DB, S, D).swapaxes(0, 1),
                                     oxs.reshape(DB, S, XA_WIDTH).swapaxes(0, 1), w_xo_b[l], state_ffn[l],
                                     g_ffn[l], w_up_b, ffn_conv_w[l], w_down_b, g_final, last, l)
        xs = xs_t.swapaxes(0, 1).reshape(DB * S, D)
        fs2 = jnp.stack([u_a, u_b], axis=1)

        for lst, val in zip(outs, (k1, v1, k_new.reshape(DB, S, KV_HEADS, HEAD_DIM),
                                   v_new.reshape(DB, S, KV_HEADS, HEAD_DIM), h1, h2, c1, c2, fs1, fs2, mk, mv)):
            lst.append(val)

    res = [jnp.stack(o) for o in outs]
    for j in (0, 1):
        res[j] = res[j].reshape(depth, B, KV_HEADS, HEAD_DIM, T).transpose(0, 1, 4, 2, 3)
    res[5] = res[5].transpose(0, 4, 1, 2, 3)
    return (xp.reshape(B, T, D), xs.reshape(DB, S, D)) + tuple(res)
```

```python
import functools
import math

import numpy as np
import jax
import jax.numpy as jnp
from jax import lax
from jax.experimental import pallas as pl
from jax.experimental.pallas import tpu as pltpu

F32 = jnp.float32
BF16 = jnp.bfloat16
HIGHEST = lax.Precision.HIGHEST
NEG_INF = float("-inf")
LOG2_E = 1.4426950408889634

D_MODEL = 1024
HEAD_DIM = 64
ATT_HEADS = 8
KV_HEADS = 4
ATT_WIDTH = ATT_HEADS * HEAD_DIM
KV_WIDTH = KV_HEADS * HEAD_DIM
MOBA_BLOCK = 256
MOBA_TOPK = 3
ROPE_THETA = 10000.0
HG_HEADS = 4
HG_DK = 64
HG_WIDTH = HG_HEADS * HG_DK
HG_CHUNK = 64
SUBLANES = 8
HG_LEAF = 4
CONV_CH = 256
N_MEM = 256
XA_HEADS = 4
XA_WIDTH = 256
D_FF = 2816
EPS = 1e-6
IN_WIDTH = 2816
COL = 256
LANES = 128
ROW_TILE = 512
FFN_ROW_TILE = 512
HALO = 16
MOBA_REQ_PER_STEP = 4
XATTN_REQ_PER_STEP = 16
VMEM_LIMIT = 56 * 1024 * 1024

CB_K, CB_V, CB_HQ, CB_HF, CB_HI, CB_HG, CB_CB, CB_CC, CB_CH = 2, 3, 4, 5, 6, 7, 8, 9, 10


def _cparams(*sem):
    return pltpu.CompilerParams(dimension_semantics=sem, vmem_limit_bytes=VMEM_LIMIT)


def _rms(x, g):
    ms = jnp.mean(x * x, axis=-1, keepdims=True)
    return x * lax.rsqrt(ms + EPS) * g


def _sigmoid(z):
    return 1.0 / (1.0 + jnp.exp2(z * -LOG2_E))


def _silu(z):
    return z * _sigmoid(z)


def _dot(a, b):
    return jnp.dot(a, b, preferred_element_type=F32)


def _dot_nt(a, b, precision=None):
    return lax.dot_general(a, b, (((1,), (1,)), ((), ())), precision=precision,
                           preferred_element_type=F32)


def _dot_tn(a, b):
    return lax.dot_general(a, b, (((0,), (0,)), ((), ())), preferred_element_type=F32)


def _block_diag_mask(rows, cols, blk):
    r = lax.broadcasted_iota(jnp.int32, (rows, cols), 0) // blk
    c = lax.broadcasted_iota(jnp.int32, (rows, cols), 1) // blk
    return r == c


def _norm_proj_kernel(x_ref, g_ref, w_ref, cos_ref, sin_ref, o_ref, *t_refs, n_rope):
    xn = _rms(x_ref[...], g_ref[...]).astype(BF16)
    n_chunks = o_ref.shape[1] // COL
    if n_rope:
        cos = jnp.concatenate([cos_ref[...]] * (COL // LANES), axis=1)
        sin = jnp.concatenate([sin_ref[...]] * (COL // LANES), axis=1)
        lane = lax.broadcasted_iota(jnp.int32, cos.shape, 1)
        first_half = (lane % HEAD_DIM) < HEAD_DIM // 2
    for c in range(n_chunks):
        y = _dot(xn, w_ref[:, c * COL:(c + 1) * COL])
        if c < n_rope:
            partner = jnp.where(first_half, pltpu.roll(y, COL - HEAD_DIM // 2, 1),
                                pltpu.roll(y, HEAD_DIM // 2, 1))
            y = y * cos + partner * sin
        o_ref[:, c * COL:(c + 1) * COL] = y
        if t_refs and c < CB_HQ:
            qt_ref, kt_ref, vt_ref = t_refs
            if c < CB_K:
                qt_ref[c * COL:(c + 1) * COL, :] = y.T
            else:
                (kt_ref if c == CB_K else vt_ref)[...] = y.T


def _norm_proj(x, g, w, layer, rope=None, n_rope=0, seq=None):
    rows, d = x.shape
    n = w.shape[2]
    tm = min(ROW_TILE, rows)
    out_specs = [pl.BlockSpec((tm, n), lambda i: (i, 0))]
    out_shape = [jax.ShapeDtypeStruct((rows, n), F32)]
    if seq is not None:
        tiles = seq // tm
        kv_t = pl.BlockSpec((None, KV_WIDTH, tm), lambda i: (i // tiles, 0, i % tiles))
        out_specs += [pl.BlockSpec((ATT_WIDTH, tm), lambda i: (0, i)), kv_t, kv_t]
        out_shape += [jax.ShapeDtypeStruct((ATT_WIDTH, rows), F32),
                      jax.ShapeDtypeStruct((rows // seq, KV_WIDTH, seq), F32),
                      jax.ShapeDtypeStruct((rows // seq, KV_WIDTH, seq), F32)]
    if rope is None:
        cos = sin = jnp.zeros((tm, LANES), F32)
    else:
        cos, sin = rope
    tab_blocks = cos.shape[0] // tm
    res = pl.pallas_call(
        functools.partial(_norm_proj_kernel, n_rope=n_rope),
        grid=(rows // tm,),
        in_specs=[
            pl.BlockSpec((tm, d), lambda i: (i, 0)),
            pl.BlockSpec((1, d), lambda i: (0, 0)),
            pl.BlockSpec((None, d, n), lambda i: (layer, 0, 0)),
            pl.BlockSpec((tm, LANES), lambda i: (i % tab_blocks, 0)),
            pl.BlockSpec((tm, LANES), lambda i: (i % tab_blocks, 0)),
        ],
        out_specs=out_specs,
        out_shape=out_shape,
        compiler_params=_cparams("parallel"),
        name="norm_proj",
    )(x, g.reshape(1, d), w, cos, sin)
    return res[0] if seq is None else res


def _rope_tables(pos):
    half = HEAD_DIM // 2
    inv = ROPE_THETA ** (-jnp.arange(half, dtype=F32) / half)
    ang = pos.astype(F32)[:, None] * inv[None, :]
    cos = jnp.cos(ang)
    sin = jnp.sin(ang)
    reps = LANES // HEAD_DIM
    cos_t = jnp.concatenate([cos, cos] * reps, axis=1)
    sin_t = jnp.concatenate([-sin, sin] * reps, axis=1)
    return cos_t, sin_t


def _resid_proj_kernel(*refs, n_in, first_transposed):
    x_ref = refs[0]
    a_refs = refs[1:1 + n_in]
    w_refs = refs[1 + n_in:1 + 2 * n_in]
    g_ref, wq_ref, o_ref, q_ref = refs[1 + 2 * n_in:]
    acc = x_ref[...]
    for k, (a_ref, w_ref) in enumerate(zip(a_refs, w_refs)):
        a = a_ref[...].astype(BF16)
        acc = acc + (_dot_tn(a, w_ref[...]) if first_transposed and k == 0 else _dot(a, w_ref[...]))
    o_ref[...] = acc
    q_ref[...] = _dot(_rms(acc, g_ref[...]).astype(BF16), wq_ref[...])


def _resid_proj(x, acts, weights, g, wq, first_transposed=False):
    rows, d = x.shape
    tm = min(ROW_TILE, rows)
    n_in = len(acts)
    nq = wq.shape[1]
    row = lambda n: pl.BlockSpec((tm, n), lambda i: (i, 0))
    in_specs = [row(d)] + [row(a.shape[1]) for a in acts]
    if first_transposed:
        in_specs[1] = pl.BlockSpec((acts[0].shape[0], tm), lambda i: (0, i))
    in_specs += [pl.BlockSpec(w.shape, lambda i: (0, 0)) for w in weights]
    in_specs += [pl.BlockSpec((1, d), lambda i: (0, 0)), pl.BlockSpec(wq.shape, lambda i: (0, 0))]
    return pl.pallas_call(
        functools.partial(_resid_proj_kernel, n_in=n_in, first_transposed=first_transposed),
        grid=(rows // tm,),
        in_specs=in_specs,
        out_specs=[row(d), row(nq)],
        out_shape=[jax.ShapeDtypeStruct((rows, d), F32), jax.ShapeDtypeStruct((rows, nq), F32)],
        compiler_params=_cparams("parallel"),
        name="resid_proj",
    )(x, *acts, *weights, g.reshape(1, d), wq)


def _topk_rank(gate, n_cand, n_valid, axis):
    idx = lax.broadcasted_iota(jnp.int32, gate.shape, axis)
    cnt = jnp.zeros(gate.shape, F32)
    for m in range(n_cand):
        gm = gate[m:m + 1, :] if axis == 0 else gate[:, m:m + 1]
        beats = jnp.where(gm > gate, 1.0, jnp.where(gm == gate, jnp.where(m < idx, 1.0, 0.0), 0.0))
        cnt = cnt + beats * jnp.where(m < n_valid, 1.0, 0.0)
    return cnt, idx


def _moba_prompt_kernel(qt_ref, k_ref, vt_ref, o_ref, kbf, vt, sbuf, pbuf, acc):
    i = pl.program_id(2)
    n_blk = k_ref.shape[0] // MOBA_BLOCK
    nq = 4 * MOBA_BLOCK

    @pl.when(i == 0)
    def _():
        kbf[...] = k_ref[...].astype(BF16)
        vt[...] = vt_ref[...].astype(BF16)

    qt = qt_ref[...]
    zero = jnp.zeros((HEAD_DIM, MOBA_BLOCK), F32)
    qst = jnp.concatenate(
        [jnp.concatenate([qt[j * HEAD_DIM:(j + 1) * HEAD_DIM, :], zero] if j < 2
                         else [zero, qt[j * HEAD_DIM:(j + 1) * HEAD_DIM, :]], axis=0) for j in range(4)],
        axis=1)

    qsc = (qst * (HEAD_DIM ** -0.5 * LOG2_E)).astype(BF16)
    causal = (lax.broadcasted_iota(jnp.int32, (MOBA_BLOCK, nq), 0)
              <= lax.broadcasted_iota(jnp.int32, (MOBA_BLOCK, nq), 1) % MOBA_BLOCK)
    bidx = lax.broadcasted_iota(jnp.int32, (n_blk, nq), 0)

    def attend(own):
        blocks = [slice(n * MOBA_BLOCK, (n + 1) * MOBA_BLOCK) for n in range(own + 1)]
        cmax = []
        gate = jnp.zeros((n_blk, nq), F32)
        for n, rows in enumerate(blocks):
            s = _dot(kbf[rows, :], qsc)
            if n == own:
                s = jnp.where(causal, s, NEG_INF)
            else:
                gate = jnp.where(bidx == n, jnp.sum(s, axis=0, keepdims=True), gate)
            sbuf[rows, :] = s
            cmax.append(jnp.max(s, axis=0, keepdims=True))
        if own > MOBA_TOPK:
            cnt, _ = _topk_rank(gate, own, own, 0)
            bias = jnp.where((bidx < own) & (cnt < MOBA_TOPK), 0.0, NEG_INF)
        else:
            bias = jnp.zeros((n_blk, nq), F32)
        m = cmax[own]
        for n in range(own):
            m = jnp.maximum(m, cmax[n] + bias[n:n + 1, :])
        l = jnp.zeros((1, nq), F32)
        out = jnp.zeros((LANES, nq), F32)
        for n, rows in enumerate(blocks):
            shift = m if n == own else m - bias[n:n + 1, :]
            p = jnp.exp2(sbuf[rows, :] - shift)
            l = l + jnp.sum(p, axis=0, keepdims=True)
            pbuf[rows, :] = p.astype(BF16)
            out = out + _dot(vt[:, rows], pbuf[rows, :])
        acc[...] = out / l

    for own in range(n_blk):
        pl.when(i == own)(functools.partial(attend, own))

    for j in range(4):
        r0 = 0 if j < 2 else HEAD_DIM
        o_ref[j * HEAD_DIM:(j + 1) * HEAD_DIM, :] = acc[r0:r0 + HEAD_DIM, j * MOBA_BLOCK:(j + 1) * MOBA_BLOCK]


def _moba_prompt(h, q_t, v_t, batch, seq):
    n_blk = seq // MOBA_BLOCK
    kcol = (CB_K * COL) // LANES
    return pl.pallas_call(
        _moba_prompt_kernel,
        grid=(batch, KV_HEADS // 2, n_blk),
        in_specs=[
            pl.BlockSpec((COL, MOBA_BLOCK), lambda b, p, i: (p, b * n_blk + i)),
            pl.BlockSpec((seq, LANES), lambda b, p, i: (b, kcol + p)),
            pl.BlockSpec((None, LANES, seq), lambda b, p, i: (b, p, 0)),
        ],
        out_specs=pl.BlockSpec((COL, MOBA_BLOCK), lambda b, p, i: (p, b * n_blk + i)),
        out_shape=jax.ShapeDtypeStruct((ATT_WIDTH, batch * seq), F32),
        scratch_shapes=[
            pltpu.VMEM((seq, LANES), BF16),
            pltpu.VMEM((LANES, seq), BF16),
            pltpu.VMEM((seq, 4 * MOBA_BLOCK), F32),
            pltpu.VMEM((seq, 4 * MOBA_BLOCK), BF16),
            pltpu.VMEM((LANES, 4 * MOBA_BLOCK), F32),
        ],
        compiler_params=_cparams("parallel", "parallel", "arbitrary"),
        name="moba_prompt",
    )(q_t, h, v_t)


def _moba_sample_kernel(pt_ref, qbd_ref, kn_ref, vn_ref, *rest, n_pages, page):
    del pt_ref
    per_step = qbd_ref.shape[0]
    n_refs = per_step * n_pages
    o_ref = rest[2 * n_refs]
    kc_all, vc_all = rest[2 * n_refs + 1:]
    for g in range(per_step):
        _moba_sample_request(qbd_ref.at[g], kn_ref.at[g], vn_ref.at[g],
                             rest[g * n_pages:(g + 1) * n_pages],
                             rest[n_refs + g * n_pages:n_refs + (g + 1) * n_pages],
                             o_ref.at[g], kc_all.at[g], vc_all.at[g], page)


def _moba_sample_request(qbd_ref, kn_ref, vn_ref, kp, vp, o_ref, kc, vc, page):
    n_pages = len(kp)
    n_blk = n_pages * page // MOBA_BLOCK
    n_rows = qbd_ref.shape[0]
    seq_new = kn_ref.shape[0]

    for pg in range(n_pages):
        kc[:, pg * page:(pg + 1) * page] = kp[pg][...].astype(BF16)
        vc[:, pg * page:(pg + 1) * page] = vp[pg][...].astype(BF16)

    qbd = qbd_ref[...]
    qs = qbd * HEAD_DIM ** -0.5
    s_all = _dot(qs.astype(BF16), kc[...])

    lane = lax.broadcasted_iota(jnp.int32, (n_rows, LANES), 1)
    gate = jnp.zeros((n_rows, LANES), F32)
    for n in range(n_blk):
        gate = jnp.where(lane == n, jnp.mean(s_all[:, n * MOBA_BLOCK:(n + 1) * MOBA_BLOCK], axis=-1, keepdims=True),
                         gate)
    cnt, bidx = _topk_rank(gate, n_blk, n_blk, 1)
    bias = jnp.where((bidx < n_blk) & (cnt < MOBA_TOPK), 0.0, NEG_INF)
    s_blk = [s_all[:, n * MOBA_BLOCK:(n + 1) * MOBA_BLOCK] + bias[:, n:n + 1] for n in range(n_blk)]

    kn = kn_ref[...]
    vn = vn_ref[...]
    tok = lax.broadcasted_iota(jnp.int32, (n_rows, 1), 0) % seq_new
    s_own = []
    for t in range(seq_new):
        st = jnp.sum(qs * kn[t:t + 1, :], axis=-1, keepdims=True)
        s_own.append(jnp.where(t <= tok, st, NEG_INF))

    m = s_own[0]
    for st in s_own[1:]:
        m = jnp.maximum(m, st)
    for sb in s_blk:
        m = jnp.maximum(m, jnp.max(sb, axis=-1, keepdims=True))

    l = jnp.zeros((n_rows, 1), F32)
    out = jnp.zeros((n_rows, KV_WIDTH), F32)
    for t in range(seq_new):
        pt = jnp.exp(s_own[t] - m)
        l = l + pt
        out = out + pt * vn[t:t + 1, :]
    for n in range(n_blk):
        pn = jnp.exp(s_blk[n] - m)
        l = l + jnp.sum(pn, axis=-1, keepdims=True)
        out = out + _dot_nt(pn.astype(BF16), vc[:, n * MOBA_BLOCK:(n + 1) * MOBA_BLOCK])
    o_ref[...] = out / l


def _moba_sample(qbd, k_new, v_new, cache_kt, cache_vt, page_table, layer):
    n_req, n_pages = page_table.shape
    depth, n_pool, _, page = cache_kt.shape
    n_rows = qbd.shape[1]
    seq_new = k_new.shape[1]

    per_step = math.gcd(n_req, MOBA_REQ_PER_STEP)
    n_refs = per_step * n_pages
    cache_kt = cache_kt.reshape(depth * n_pool, KV_WIDTH, page)
    cache_vt = cache_vt.reshape(depth * n_pool, KV_WIDTH, page)
    page_table = page_table + layer * n_pool

    def page_spec(j):
        return pl.BlockSpec((None, KV_WIDTH, page), lambda r, pt: (pt[r * n_refs + j], 0, 0))

    in_specs = [
        pl.BlockSpec((per_step, n_rows, KV_WIDTH), lambda r, pt: (r, 0, 0)),
        pl.BlockSpec((per_step, seq_new, KV_WIDTH), lambda r, pt: (r, 0, 0)),
        pl.BlockSpec((per_step, seq_new, KV_WIDTH), lambda r, pt: (r, 0, 0)),
    ]
    in_specs += [page_spec(j) for j in range(n_refs)]
    in_specs += [page_spec(j) for j in range(n_refs)]
    grid_spec = pltpu.PrefetchScalarGridSpec(
        num_scalar_prefetch=1,
        grid=(n_req // per_step,),
        in_specs=in_specs,
        out_specs=pl.BlockSpec((per_step, n_rows, KV_WIDTH), lambda r, pt: (r, 0, 0)),
        scratch_shapes=[
            pltpu.VMEM((per_step, KV_WIDTH, n_pages * page), BF16),
            pltpu.VMEM((per_step, KV_WIDTH, n_pages * page), BF16),
        ],
    )
    return pl.pallas_call(
        functools.partial(_moba_sample_kernel, n_pages=n_pages, page=page),
        grid_spec=grid_spec,
        out_shape=jax.ShapeDtypeStruct((n_req, n_rows, KV_WIDTH), F32),
        compiler_params=_cparams("arbitrary"),
        name="moba_sample",
    )(page_table.reshape(-1), qbd, k_new, v_new, *([cache_kt] * n_refs), *([cache_vt] * n_refs))


def _hgrn_gates(hq, hf, lb):
    q = _silu(hq)
    f = lb + (1.0 - lb) * _sigmoid(hf)
    k = (1.0 - lb) * _sigmoid(-hf)
    return q, k, f


def _head_rms(o, norm_g):
    head_mean = jnp.where(_block_diag_mask(HG_WIDTH, HG_WIDTH, HG_DK), 1.0 / HG_DK, 0.0)
    ms = jnp.dot(o * o, head_mean, precision=HIGHEST, preferred_element_type=F32)
    return o * lax.rsqrt(ms + EPS) * norm_g


def _short_conv(ext_ref, rows, cw):
    return (cw[0:1, :] * ext_ref[pl.ds(HALO - 2, rows), :]
            + cw[1:2, :] * ext_ref[pl.ds(HALO - 1, rows), :]
            + cw[2:3, :] * ext_ref[pl.ds(HALO, rows), :])


def _hgrn_prompt_kernel(hq_ref, hf_ref, hi_ref, hg_ref, cb_ref, cc_ref, ch_ref, lb_ref, ng_ref, cw_ref,
                        oh_ref, oc_ref, st_ref, tail_ref, st, ext):
    tt = pl.program_id(1)
    rows = hq_ref.shape[0]
    n_chunks = rows // HG_CHUNK
    width = HG_WIDTH
    bd = _block_diag_mask(width, width, HG_DK)

    @pl.when(tt == 0)
    def _():
        ext[pl.ds(0, HALO), :] = jnp.zeros((HALO, CONV_CH), F32)

    @pl.when(tt > 0)
    def _():
        ext[pl.ds(0, HALO), :] = ext[pl.ds(rows, HALO), :]

    u = cc_ref[...] * ch_ref[...]
    ext[pl.ds(HALO, rows), :] = u
    oc_ref[...] = cb_ref[...] * _short_conv(ext, rows, cw_ref[...])
    tail_ref[0] = u[rows - 8:rows, :]

    @pl.when(tt == 0)
    def _():
        st[...] = jnp.zeros(st.shape, F32)

    q, k, f = _hgrn_gates(hq_ref[...], hf_ref[...], lb_ref[...])
    logf = jnp.log(f)
    v = hi_ref[...]
    r_i =lax.broadcasted_iota(jnp.int32, (rows, rows), 0)
    c_i = lax.broadcasted_iota(jnp.int32, (rows, rows), 1)
    tri = jnp.where((r_i // HG_CHUNK == c_i // HG_CHUNK) & (c_i <= r_i), 1.0, 0.0)
    a_all = jnp.dot(tri, logf, precision=HIGHEST, preferred_element_type=F32)

    L = HG_CHUNK
    t_i = lax.broadcasted_iota(jnp.int32, (L, width), 0)
    tw = lax.broadcasted_iota(jnp.int32, (L, width), 0)
    sw_col = lax.broadcasted_iota(jnp.int32, (L, width), 1) % L
    head_ones = jnp.where(bd, 1.0, 0.0).astype(BF16)
    outs = []
    for c in range(n_chunks):
        sl = slice(c * L, (c + 1) * L)
        a, qc, kc, vc = a_all[sl], q[sl], k[sl], v[sl]
        a_end = a[L - 1:L, :]
        st_old = st[...]
        o = _dot_nt((qc * jnp.exp(a)).astype(BF16), st_old.astype(BF16))

        sw = jnp.zeros((L, width), F32)
        b = L // 2
        while b >= SUBLANES:
            nb = L // b
            rq = jnp.concatenate(
                [jnp.broadcast_to(a[j * b - 1:j * b, :], (b, width)) if j % 2 else a[j * b:(j + 1) * b, :]
                 for j in range(nb)], axis=0)
            rk = jnp.concatenate(
                [a[j * b:(j + 1) * b, :] if j % 2 else jnp.broadcast_to(a[(j + 1) * b - 1:(j + 1) * b, :], (b, width))
                 for j in range(nb)], axis=0)
            odd = (t_i // b) % 2 == 1
            ql = jnp.where(odd, qc * jnp.exp(a - rq), 0.0).astype(BF16)
            kl = jnp.where(odd, 0.0, kc * jnp.exp(rk - a))
            k_bd = jnp.concatenate([kl.astype(BF16)] * HG_HEADS, axis=0) * head_ones
            sc = _dot_nt(ql, k_bd)
            pair = ((tw // b) % 2 == 1) & (sw_col // b == tw // b - 1)
            sw = sw + jnp.where(pair, sc, 0.0)
            b //= 2
        a_g = a.reshape(L // SUBLANES, SUBLANES, width)
        r4 = jnp.broadcast_to(a_g[:, HG_LEAF - 1:HG_LEAF, :], a_g.shape).reshape(L, width)
        upper = (t_i % SUBLANES) >= HG_LEAF
        ql = jnp.where(upper, qc * jnp.exp(jnp.where(upper, a - r4, 0.0)), 0.0).astype(BF16)
        kl = jnp.where(upper, 0.0, kc * jnp.exp(jnp.where(upper, 0.0, r4 - a)))
        k_bd = jnp.concatenate([kl.astype(BF16)] * HG_HEADS, axis=0) * head_ones
        pair = ((tw % SUBLANES) >= HG_LEAF) & (sw_col // SUBLANES == tw // SUBLANES) & (sw_col % SUBLANES < HG_LEAF)
        sw = sw + jnp.where(pair, _dot_nt(ql, k_bd), 0.0)
        v_bd =jnp.concatenate([vc.astype(BF16)] * HG_HEADS, axis=0) * head_ones
        o = o + _dot(sw.astype(BF16), v_bd)

        prods = []
        vrs = []
        for delta in range(HG_LEAF):
            if delta:
                kr, ar, vr = (pltpu.roll(x.reshape(L // SUBLANES, SUBLANES, width), delta, 1).reshape(L, width)
                              for x in (kc, a, vc))
            else:
                kr, ar, vr = kc, a, vc
            live = (t_i % HG_LEAF) >= delta
            prods.append(jnp.where(live, qc * kr * jnp.exp(jnp.where(live, a - ar, 0.0)), 0.0))
            vrs.append(vr)
        sc = _dot(jnp.concatenate(prods, axis=0).astype(BF16), head_ones)
        for delta in range(HG_LEAF):
            o = o + sc[delta * L:(delta + 1) * L] * vrs[delta]
        outs.append(o)

        ke = (kc * jnp.exp(a_end - a)).astype(BF16)
        upd = _dot_tn(vc.astype(BF16), ke)
        st[...] = st_old * jnp.exp(a_end) + jnp.where(bd, upd, 0.0)

    o_all = jnp.concatenate(outs, axis=0)
    oh_ref[...] = _head_rms(o_all, ng_ref[...]) * _silu(hg_ref[...])
    st_ref[0] = st[...]


def _hgrn_prompt(h, lb, norm_g, cw, batch, seq):
    rows = 256
    nt = seq // rows

    def col(cb):
        return pl.BlockSpec((rows, COL), lambda b, t: (b * nt + t, cb))

    const = lambda shape: pl.BlockSpec(shape, lambda b, t: (0, 0))
    row_out = pl.BlockSpec((rows, COL), lambda b, t: (b * nt + t, 0))
    return pl.pallas_call(
        _hgrn_prompt_kernel,
        grid=(batch, nt),
        in_specs=[col(CB_HQ), col(CB_HF), col(CB_HI), col(CB_HG), col(CB_CB), col(CB_CC), col(CB_CH),
                  const((1, HG_WIDTH)), const((1, HG_WIDTH)), const((3, CONV_CH))],
        out_specs=[row_out, row_out,
                   pl.BlockSpec((1, HG_WIDTH, HG_WIDTH), lambda b, t: (b, 0, 0)),
                   pl.BlockSpec((1, 8, CONV_CH), lambda b, t: (b, 0, 0))],
        out_shape=[jax.ShapeDtypeStruct((batch * seq, HG_WIDTH), F32),
                   jax.ShapeDtypeStruct((batch * seq, CONV_CH), F32),
                   jax.ShapeDtypeStruct((batch, HG_WIDTH, HG_WIDTH), F32),
                   jax.ShapeDtypeStruct((batch, 8, CONV_CH), F32)],
        scratch_shapes=[pltpu.VMEM((HG_WIDTH, HG_WIDTH), F32),
                        pltpu.VMEM((HALO + rows + HALO, CONV_CH), F32)],
        compiler_params=_cparams("parallel", "arbitrary"),
        name="hgrn_prompt",
    )(h, h, h, h, h, h, h, lb.reshape(1, -1), norm_g.reshape(1, -1), cw)


def _mix_sample_kernel(hq_ref, hf_ref, hi_ref, hg_ref, cb_ref, cc_ref, ch_ref, s0_ref, cprev_ref,
                       lb_ref, ng_ref, cw_ref, oh_ref, oc_ref, s_ref, cnew_ref, fg, kg, qg):
    seq = hq_ref.shape[0]

    full = [cprev_ref[0], cprev_ref[1]] + [cc_ref[t] * ch_ref[t] for t in range(seq)]
    for t in range(seq):
        oc_ref[t] = cb_ref[t] * (cw_ref[0] * full[t] + cw_ref[1] * full[t + 1] + cw_ref[2] * full[t + 2])
    cnew_ref[0] = full[seq]
    cnew_ref[1] = full[seq + 1]

    lb = lb_ref[...]
    for t in range(seq):
        q, k, f = _hgrn_gates(hq_ref[t], hf_ref[t], lb)
        fg[t], kg[t], qg[t] = f, k, q
    v = [hi_ref[t] for t in range(seq)]

    def body(dk, o):
        s = s0_ref[dk]
        row = pl.ds(dk, 1)
        new_o = []
        for t in range(seq):
            s = fg[t, row, :] * s + kg[t, row, :] * v[t]
            new_o.append(o[t] + qg[t, row, :] * s)
        s_ref[dk] = s
        return tuple(new_o)

    o = lax.fori_loop(0, HG_DK, body, tuple(jnp.zeros(v[0].shape, F32) for _ in range(seq)))
    for t in range(seq):
        ms = jnp.mean(o[t] * o[t], axis=0, keepdims=True)
        oh_ref[t] = o[t] * lax.rsqrt(ms + EPS) * ng_ref[...] * _silu(hg_ref[t])


def _mix_sample(act_t, state_t, layer, cprev_t, lb, norm_g, cw):
    seq, _, n_req = act_t.shape
    hd = HG_DK

    def act(k):
        return pl.BlockSpec((seq, hd, n_req), lambda h: (0, k * HG_HEADS + h, 0))

    per_head = lambda lead: pl.BlockSpec((lead, hd, n_req), lambda h: (0, h, 0))
    chan = pl.BlockSpec((None, hd, n_req), lambda h: (layer, h, 0))
    taps = pl.BlockSpec((None, 3, hd, n_req), lambda h: (layer, 0, h, 0))
    state_in = pl.BlockSpec((None, None, hd, hd, n_req), lambda h: (layer, h, 0, 0, 0))
    state_out = pl.BlockSpec((None, hd, hd, n_req), lambda h: (h, 0, 0, 0))
    return pl.pallas_call(
        _mix_sample_kernel,
        grid=(HG_HEADS,),
        in_specs=[act(k) for k in range(7)] + [state_in, per_head(2), chan, chan, taps],
        out_specs=[per_head(seq), per_head(seq), state_out, per_head(2)],
        out_shape=[jax.ShapeDtypeStruct((seq, HG_WIDTH, n_req), F32),
                   jax.ShapeDtypeStruct((seq, CONV_CH, n_req), F32),
                   jax.ShapeDtypeStruct((HG_HEADS, hd, hd, n_req), F32),
                   jax.ShapeDtypeStruct((2, CONV_CH, n_req), F32)],
        scratch_shapes=[pltpu.VMEM((seq, hd, n_req), F32)] * 3,
        compiler_params=_cparams("parallel"),
        name="mix_sample",
    )(*([act_t] * 7), state_t, cprev_t, lb, norm_g, cw)


def _xattn_prompt_kernel(q_ref, mk_ref, mv_ref, o_ref):
    q = q_ref[...] * (XA_WIDTH // XA_HEADS) ** -0.5
    mk = mk_ref[...].astype(BF16)
    mv = mv_ref[...].astype(BF16)
    head = lax.broadcasted_iota(jnp.int32, q.shape, 1) // (XA_WIDTH // XA_HEADS)
    out = jnp.zeros(q.shape, F32)
    for hh in range(XA_HEADS):
        mine = head == hh
        s = _dot_nt(jnp.where(mine, q, 0.0).astype(BF16), mk)
        m = jnp.max(s, axis=-1, keepdims=True)
        p = jnp.exp(s - m)
        l = jnp.sum(p, axis=-1, keepdims=True)
        out = jnp.where(mine, _dot(p.astype(BF16), mv) / l, out)
    o_ref[...] = out


def _xattn_prompt(q, mkv, batch, seq):
    tm = min(ROW_TILE, seq)
    nt = seq // tm
    return pl.pallas_call(
        _xattn_prompt_kernel,
        grid=(batch, nt),
        in_specs=[pl.BlockSpec((tm, XA_WIDTH), lambda b, t: (b * nt + t, 0)),
                  pl.BlockSpec((N_MEM, XA_WIDTH), lambda b, t: (b, 0)),
                  pl.BlockSpec((N_MEM, XA_WIDTH), lambda b, t: (b, 1))],
        out_specs=pl.BlockSpec((tm, XA_WIDTH), lambda b, t: (b * nt + t, 0)),
        out_shape=jax.ShapeDtypeStruct((batch * seq, XA_WIDTH), F32),
        compiler_params=_cparams("parallel", "parallel"),
        name="xattn_prompt",
    )(q, mkv, mkv)


def _xattn_sample_kernel(qbd_ref, mk_ref, mv_ref, o_ref):
    for r in range(qbd_ref.shape[0]):
        q = (qbd_ref[r] * (XA_WIDTH // XA_HEADS) ** -0.5).astype(BF16)
        s = _dot(q, mk_ref[r].astype(BF16))
        m = jnp.max(s, axis=-1, keepdims=True)
        p = jnp.exp(s - m)
        l = jnp.sum(p, axis=-1, keepdims=True)
        o_ref[r] = _dot_nt(p.astype(BF16), mv_ref[r].astype(BF16)) / l


def _xattn_sample(qbd, mem_kt, mem_vt, layer):
    n_req, n_rows, _ = qbd.shape
    per_step = math.gcd(n_req, XATTN_REQ_PER_STEP)
    mem_spec = pl.BlockSpec((None, per_step, XA_WIDTH, N_MEM), lambda r: (layer, r, 0, 0))
    return pl.pallas_call(
        _xattn_sample_kernel,
        grid=(n_req // per_step,),
        in_specs=[pl.BlockSpec((per_step, n_rows, XA_WIDTH), lambda r: (r, 0, 0)), mem_spec, mem_spec],
        out_specs=pl.BlockSpec((per_step, n_rows, XA_WIDTH), lambda r: (r, 0, 0)),
        out_shape=jax.ShapeDtypeStruct((n_req, n_rows, XA_WIDTH), F32),
        compiler_params=_cparams("parallel"),
        name="xattn_sample",
    )(qbd, mem_kt, mem_vt)


def _ffn_prompt_kernel(x_ref, halo_ref, ox_ref, ox_halo_ref, wxo_ref, g_ref, wup_ref, cw_ref, wdn_ref, gf_ref,
                       o_ref, u_ref, xn_sc, ext_a, ext_b, hmid, *, seq, final_norm):
    rows = x_ref.shape[0]
    wxo = wxo_ref[...]
    x = x_ref[...] + _dot(ox_ref[...].astype(BF16), wxo)
    o_ref[...] = x
    g = g_ref[...]
    x_halo = halo_ref[...] + _dot(ox_halo_ref[...].astype(BF16), wxo)
    first = (pl.program_id(0) * rows) % seq == 0
    xn_sc[pl.ds(0, HALO), :] = jnp.where(first, 0.0, _rms(x_halo, g)).astype(BF16)
    xn_sc[pl.ds(HALO, rows), :] = _rms(x, g).astype(BF16)
    xn = xn_sc[...]


    n_chunks = D_FF // COL

    def up(j):
        for half, ext in ((0, ext_a), (1, ext_b)):
            c0 = half * D_FF + j * COL
            u = _dot(xn, wup_ref[:, c0:c0 + COL])
            u_ref[0, :, c0:c0 + COL] = u[HALO + rows - 8:, :]
            ext[j % 2] = u

    def conv(ext, j, c0):
        cw = cw_ref[:, c0:c0 + COL]
        s1 = ext[j % 2, pl.ds(HALO - 1, rows), :]
        s2 = ext[j % 2, pl.ds(HALO - 2, rows), :]
        return cw[0:1, :] * s2 + cw[1:2, :] * s1 + cw[2:3, :] * ext[j % 2, pl.ds(HALO, rows), :]

    def gate(j):
        hmid[j % 2] = (_silu(conv(ext_a, j, j * COL)) * conv(ext_b, j, D_FF + j * COL)).astype(BF16)

    def down(j):
        o_ref[...] += _dot(hmid[j % 2], wdn_ref[j * COL:(j + 1) * COL, :])

    up(0)
    for j in range(n_chunks):
        if j + 1 < n_chunks:
            up(j + 1)
        gate(j)
        if j >= 1:
            down(j - 1)
    down(n_chunks - 1)
    if final_norm:
        o_ref[...] = _rms(o_ref[...], gf_ref[...])


def _ffn_sample_kernel(x_ref, ox_ref, wxo_ref, p0_ref, p1_ref, g_ref, wup_ref, cw_ref, wdn_ref, gf_ref,
                       o_ref, u2_ref, u3_ref, acc, *, final_norm):
    seq, n_req, d = x_ref.shape
    ox = ox_ref[...].reshape(seq * n_req, ox_ref.shape[2])
    x = x_ref[...].reshape(seq * n_req, d) + _dot(ox.astype(BF16), wxo_ref[...])
    xn = _rms(x, g_ref[...]).astype(BF16)
    acc[...] = jnp.zeros(acc.shape, F32)

    def conv(u, c0):
        cw = cw_ref[:, pl.ds(c0, COL)]
        full = [p0_ref[:, pl.ds(c0, COL)], p1_ref[:, pl.ds(c0, COL)]]
        full += [u[t * n_req:(t + 1) * n_req, :] for t in range(seq)]
        u2_ref[:, pl.ds(c0, COL)] = full[seq]
        u3_ref[:, pl.ds(c0, COL)] = full[seq + 1]
        return jnp.concatenate(
            [cw[0:1, :] * full[t] + cw[1:2, :] * full[t + 1] + cw[2:3, :] * full[t + 2] for t in range(seq)],
            axis=0)

    def body(j, carry):
        ca = pl.multiple_of(j * COL, COL)
        cb = pl.multiple_of(D_FF + j * COL, COL)
        ya = conv(_dot(xn, wup_ref[:, pl.ds(ca, COL)]), ca)
        yb = conv(_dot(xn, wup_ref[:, pl.ds(cb, COL)]), cb)
        acc[...] += _dot((_silu(ya) * yb).astype(BF16), wdn_ref[pl.ds(ca, COL), :])
        return carry

    lax.fori_loop(0, D_FF // COL, body, 0)
    y = x + acc[...]
    if final_norm:
        y = _rms(y, gf_ref[...])
    o_ref[...] = y.reshape(seq, n_req, d)


def _resident(shape, layer=None):
    if layer is None:
        return pl.BlockSpec(shape, lambda i: (0,) * len(shape), pipeline_mode=pl.Buffered(1))
    return pl.BlockSpec((None,) + shape, lambda i: (layer,) + (0,) * len(shape), pipeline_mode=pl.Buffered(1))


def _ffn_prompt(x, ox, w_xo, g, w_up, cw, w_dn, g_final, seq, final_norm, layer):
    rows, d = x.shape
    tm = min(FFN_ROW_TILE, seq)
    n_up = w_up.shape[2]
    nx = ox.shape[1]
    tiles_per_seq = seq // tm
    x_spec = pl.BlockSpec((tm, d), lambda i: (i, 0))
    halo = lambda n: pl.BlockSpec((HALO, n), lambda i: (jnp.maximum(i * (tm // HALO) - 1, 0), 0))
    return pl.pallas_call(
        functools.partial(_ffn_prompt_kernel, seq=seq, final_norm=final_norm),
        grid=(rows // tm,),
        in_specs=[x_spec, halo(d), pl.BlockSpec((tm, nx), lambda i: (i, 0)), halo(nx), _resident((nx, d)),
                  _resident((1, d)), _resident((d, n_up), layer), _resident((3, n_up)),
                  _resident((D_FF, d), layer), _resident((1, d))],
        out_specs=[x_spec, pl.BlockSpec((1, 8, n_up), lambda i: (i // tiles_per_seq, 0, 0))],
        out_shape=[jax.ShapeDtypeStruct((rows, d), F32),
                   jax.ShapeDtypeStruct((rows // seq, 8, n_up), F32)],
        scratch_shapes=[pltpu.VMEM((HALO + tm, d), BF16),
                        pltpu.VMEM((2, HALO + tm, COL), F32),
                        pltpu.VMEM((2, HALO + tm, COL), F32),
                        pltpu.VMEM((2, tm, COL), BF16)],
        compiler_params=_cparams("arbitrary"),
        name="ffn_prompt",
    )(x, x, ox, ox, w_xo, g.reshape(1, d), w_up, cw, w_dn, g_final.reshape(1, d))


def _ffn_sample(x, ox, w_xo, prev, g, w_up, cw, w_dn, g_final, final_norm, layer):
    seq, n_req, d = x.shape
    n_up = w_up.shape[2]
    nx = ox.shape[2]
    full = lambda shape: pl.BlockSpec(shape, lambda i: (0,) * len(shape))
    return pl.pallas_call(
        functools.partial(_ffn_sample_kernel, final_norm=final_norm),
        grid=(1,),
        in_specs=[_resident((seq, n_req, d)), _resident((seq, n_req, nx)), _resident((nx, d)),
                  _resident((n_req, n_up)), _resident((n_req, n_up)),
                  _resident((1, d)), _resident((d, n_up), layer), _resident((3, n_up)),
                  _resident((D_FF, d), layer), _resident((1, d))],
        out_specs=[full((seq, n_req, d)), full((n_req, n_up)), full((n_req, n_up))],
        out_shape=[jax.ShapeDtypeStruct((seq, n_req, d), F32),
                   jax.ShapeDtypeStruct((n_req, n_up), F32),
                   jax.ShapeDtypeStruct((n_req, n_up), F32)],
        scratch_shapes=[pltpu.VMEM((seq * n_req, d), F32)],
        compiler_params=_cparams("arbitrary"),
        name="ffn_sample",
    )(x, ox, w_xo, prev[:, 0], prev[:, 1], g.reshape(1, d), w_up, cw, w_dn, g_final.reshape(1, d))


def _head_rows(q, n_heads, group):
    n_req, seq, _ = q.shape
    n_kv = n_heads // group
    qh = q.reshape(n_req, seq, n_heads, HEAD_DIM).transpose(0, 2, 1, 3)
    onehot = jax.nn.one_hot(jnp.arange(n_heads) // group, n_kv, dtype=q.dtype)
    out = qh[:, :, :, None, :] * onehot[None, :, None, :, None]
    return out.reshape(n_req, n_heads * seq, n_kv * HEAD_DIM)


def _head_rows_inverse(o, n_heads, group, seq):
    n_req = o.shape[0]
    n_kv = n_heads // group
    o5 = o.reshape(n_req, n_heads, seq, n_kv, HEAD_DIM)
    onehot = jax.nn.one_hot(jnp.arange(n_heads) // group, n_kv, dtype=o.dtype)
    picked = (o5 * onehot[None, :, None, :, None]).sum(axis=3)
    return picked.transpose(0, 2, 1, 3).reshape(n_req * seq, n_heads * HEAD_DIM)


def kernel(x_prompt, x_sample, cache_k, cache_v, cache_mem_k, cache_mem_v, state_hgrn, state_conv,
           state_ffn, page_table, mem_prompt, g_mix, w_in, w_out, hg_lb_logits, hg_norm_g, conv_w,
           g_xattn, g_mem, w_xq, w_mk, w_mv, w_xo, g_ffn, w_up, ffn_conv_w, w_down, g_final):
    B, T, D = x_prompt.shape
    DB, S, _ = x_sample.shape
    depth = w_in.shape[0]
    n_pages = page_table.shape[1]
    page = cache_k.shape[2]
    past = n_pages * page

    lb_p = jax.nn.softmax(hg_lb_logits.astype(F32), axis=0)
    lower = jnp.cumsum(lb_p, axis=0) - lb_p[0:1]

    rope_p = _rope_tables(jnp.arange(T, dtype=jnp.int32))
    rope_s = _rope_tables(jnp.tile(past + jnp.arange(S, dtype=jnp.int32), DB))

    w_in_b, w_out_b, w_xq_b, w_xo_b = (w.astype(BF16) for w in (w_in, w_out, w_xq, w_xo))
    w_up_b, w_down_b = w_up.astype(BF16), w_down.astype(BF16)
    w_mkv_b = jnp.concatenate([w_mk, w_mv], axis=-1).astype(BF16)

    def token_minor(c):
        return c.transpose(0, 1, 3, 4, 2).reshape(c.shape[0], c.shape[1], c.shape[3] * c.shape[4], c.shape[2])

    cache_k4, cache_v4 = token_minor(cache_k), token_minor(cache_v)
    mem_k4, mem_v4 = token_minor(cache_mem_k), token_minor(cache_mem_v)
    state_t = state_hgrn.transpose(0, 2, 3, 4, 1)
    spread = lambda a: jnp.broadcast_to(a[..., None], a.shape + (DB,))
    lower_s, norm_g_s, conv_w_s = spread(lower), spread(hg_norm_g), spread(conv_w)
    mem2 = mem_prompt.reshape(B * N_MEM, D)

    xp = x_prompt.reshape(B * T, D)
    xs = x_sample.reshape(DB * S, D)
    outs = [[] for _ in range(12)]
    for l in range(depth):
        last = l == depth - 1
        h, q_t, k1, v1 = _norm_proj(xp, g_mix[l], w_in_b, l, rope=rope_p,
                                    n_rope=(ATT_WIDTH + KV_WIDTH) // COL, seq=T)
        a = _moba_prompt(h, q_t, v1, B, T)
        o_h, o_c, st_t, c_tail = _hgrn_prompt(h, lower[l], hg_norm_g[l], conv_w[l], B, T)
        st5 = st_t.reshape(B, HG_HEADS, HG_DK, HG_HEADS, HG_DK)
        h1 = jnp.stack([st5[:, hh, :, hh, :] for hh in range(HG_HEADS)], axis=1).swapaxes(-1, -2)
        c1 = c_tail[:, 6:8, :]
        w_out_parts = [w_out_b[l, :ATT_WIDTH], w_out_b[l, ATT_WIDTH:ATT_WIDTH + HG_WIDTH],
                       w_out_b[l, ATT_WIDTH + HG_WIDTH:]]
        xp, qx = _resid_proj(xp, [a, o_h, o_c], w_out_parts, g_xattn[l], w_xq_b[l], first_transposed=True)
        mkv = _norm_proj(mem2, g_mem[l], w_mkv_b, l)
        mk = mkv[:, :XA_WIDTH].reshape(B, N_MEM, XA_HEADS, XA_WIDTH // XA_HEADS)
        mv = mkv[:, XA_WIDTH:].reshape(B, N_MEM, XA_HEADS, XA_WIDTH // XA_HEADS)
        ox = _xattn_prompt(qx, mkv, B, T)
        xp, u_tail = _ffn_prompt(xp, ox, w_xo_b[l], g_ffn[l], w_up_b, ffn_conv_w[l], w_down_b,
                                 g_final, T, last, l)
        fs1 = u_tail[:, 6:8, :]

        hs = _norm_proj(xs, g_mix[l], w_in_b, l, rope=rope_s, n_rope=(ATT_WIDTH + KV_WIDTH) // COL)
        hs3 = hs.reshape(DB, S, IN_WIDTH)
        k_new = hs3[:, :, CB_K * COL:(CB_K + 1) * COL]
        v_new = hs3[:, :, CB_V * COL:(CB_V + 1) * COL]
        qbd = _head_rows(hs3[:, :, :ATT_WIDTH], ATT_HEADS, ATT_HEADS // KV_HEADS)
        o_att = _moba_sample(qbd, k_new, v_new, cache_k4, cache_v4, page_table, l)
        a_s = _head_rows_inverse(o_att, ATT_HEADS, ATT_HEADS // KV_HEADS, S)
        act_t = hs3[:, :, CB_HQ * COL:].transpose(1, 2, 0)
        oh_t, oc_t, h2, c2_t = _mix_sample(act_t, state_t, l, state_conv[l].transpose(1, 2, 0),
                                           lower_s, norm_g_s, conv_w_s)
        oh_s, oc_s, c2 = (a.transpose(2, 0, 1) for a in (oh_t, oc_t, c2_t))
        xs, qxs = _resid_proj(xs, [a_s, oh_s.reshape(DB * S, HG_WIDTH), oc_s.reshape(DB * S, CONV_CH)],
                              w_out_parts, g_xattn[l], w_xq_b[l])
        qxbd = _head_rows(qxs.reshape(DB, S, XA_WIDTH), XA_HEADS, 1)
        oxs = _head_rows_inverse(_xattn_sample(qxbd, mem_k4, mem_v4, l), XA_HEADS, 1, S)
        xs_t, u_a, u_b = _ffn_sample(xs.reshape(DB, S, D).swapaxes(0, 1),
                                     oxs.reshape(DB, S, XA_WIDTH).swapaxes(0, 1), w_xo_b[l], state_ffn[l],
                                     g_ffn[l], w_up_b, ffn_conv_w[l], w_down_b, g_final, last, l)
        xs = xs_t.swapaxes(0, 1).reshape(DB * S, D)
        fs2 = jnp.stack([u_a, u_b], axis=1)

        for lst, val in zip(outs, (k1, v1, k_new.reshape(DB, S, KV_HEADS, HEAD_DIM),
                                   v_new.reshape(DB, S, KV_HEADS, HEAD_DIM), h1, h2, c1, c2, fs1, fs2, mk, mv)):
            lst.append(val)

    res = [jnp.stack(o) for o in outs]
    for j in (0, 1):
        res[j] = res[j].reshape(depth, B, KV_HEADS, HEAD_DIM, T).transpose(0, 1, 4, 2, 3)
    res[5] = res[5].transpose(0, 4, 1, 2, 3)
    return (xp.reshape(B, T, D), xs.reshape(DB, S, D)) + tuple(res)
```

```python
import functools
import math

import numpy as np
import jax
import jax.numpy as jnp
from jax import lax
from jax.experimental import pallas as pl
from jax.experimental.pallas import tpu as pltpu

F32 = jnp.float32
BF16 = jnp.bfloat16
HIGHEST = lax.Precision.HIGHEST
NEG_INF = float("-inf")
LOG2_E = 1.4426950408889634

D_MODEL = 1024
HEAD_DIM = 64
ATT_HEADS = 8
KV_HEADS = 4
ATT_WIDTH = ATT_HEADS * HEAD_DIM
KV_WIDTH = KV_HEADS * HEAD_DIM
MOBA_BLOCK = 256
MOBA_TOPK = 3
ROPE_THETA = 10000.0
HG_HEADS = 4
HG_DK = 64
HG_WIDTH = HG_HEADS * HG_DK
HG_CHUNK = 64
SUBLANES = 8
HG_LEAF = 4
CONV_CH = 256
N_MEM = 256
XA_HEADS = 4
XA_WIDTH = 256
D_FF = 2816
EPS = 1e-6
IN_WIDTH = 2816
COL = 256
LANES = 128
ROW_TILE = 512
FFN_ROW_TILE = 512
HALO = 16
MOBA_REQ_PER_STEP = 4
XATTN_REQ_PER_STEP = 16
VMEM_LIMIT = 56 * 1024 * 1024

CB_K, CB_V, CB_HQ, CB_HF, CB_HI, CB_HG, CB_CB, CB_CC, CB_CH = 2, 3, 4, 5, 6, 7, 8, 9, 10


def _cparams(*sem):
    return pltpu.CompilerParams(dimension_semantics=sem, vmem_limit_bytes=VMEM_LIMIT)


def _rms(x, g):
    ms = jnp.mean(x * x, axis=-1, keepdims=True)
    return x * lax.rsqrt(ms + EPS) * g


def _sigmoid(z):
    return 1.0 / (1.0 + jnp.exp2(z * -LOG2_E))


def _silu(z):
    return z * _sigmoid(z)


def _dot(a, b):
    return jnp.dot(a, b, preferred_element_type=F32)


def _dot_nt(a, b, precision=None):
    return lax.dot_general(a, b, (((1,), (1,)), ((), ())), precision=precision,
                           preferred_element_type=F32)


def _dot_tn(a, b):
    return lax.dot_general(a, b, (((0,), (0,)), ((), ())), preferred_element_type=F32)


def _block_diag_mask(rows, cols, blk):
    r = lax.broadcasted_iota(jnp.int32, (rows, cols), 0) // blk
    c = lax.broadcasted_iota(jnp.int32, (rows, cols), 1) // blk
    return r == c


def _norm_proj_kernel(x_ref, g_ref, w_ref, cos_ref, sin_ref, o_ref, *t_refs, n_rope):
    xn = _rms(x_ref[...], g_ref[...]).astype(BF16)
    n_chunks = o_ref.shape[1] // COL
    if n_rope:
        cos = jnp.concatenate([cos_ref[...]] * (COL // LANES), axis=1)
        sin = jnp.concatenate([sin_ref[...]] * (COL // LANES), axis=1)
        lane = lax.broadcasted_iota(jnp.int32, cos.shape, 1)
        first_half = (lane % HEAD_DIM) < HEAD_DIM // 2
    for c in range(n_chunks):
        y = _dot(xn, w_ref[:, c * COL:(c + 1) * COL])
        if c < n_rope:
            partner = jnp.where(first_half, pltpu.roll(y, COL - HEAD_DIM // 2, 1),
                                pltpu.roll(y, HEAD_DIM // 2, 1))
            y = y * cos + partner * sin
        o_ref[:, c * COL:(c + 1) * COL] = y
        if t_refs and c < CB_HQ:
            qt_ref, kt_ref, vt_ref = t_refs
            if c < CB_K:
                qt_ref[c * COL:(c + 1) * COL, :] = y.T
            else:
                (kt_ref if c == CB_K else vt_ref)[...] = y.T


def _norm_proj(x, g, w, layer, rope=None, n_rope=0, seq=None):
    rows, d = x.shape
    n = w.shape[2]
    tm = min(ROW_TILE, rows)
    out_specs = [pl.BlockSpec((tm, n), lambda i: (i, 0))]
    out_shape = [jax.ShapeDtypeStruct((rows, n), F32)]
    if seq is not None:
        tiles = seq // tm
        kv_t = pl.BlockSpec((None, KV_WIDTH, tm), lambda i: (i // tiles, 0, i % tiles))
        out_specs += [pl.BlockSpec((ATT_WIDTH, tm), lambda i: (0, i)), kv_t, kv_t]
        out_shape += [jax.ShapeDtypeStruct((ATT_WIDTH, rows), F32),
                      jax.ShapeDtypeStruct((rows // seq, KV_WIDTH, seq), F32),
                      jax.ShapeDtypeStruct((rows // seq, KV_WIDTH, seq), F32)]
    if rope is None:
        cos = sin = jnp.zeros((tm, LANES), F32)
    else:
        cos, sin = rope
    tab_blocks = cos.shape[0] // tm
    res = pl.pallas_call(
        functools.partial(_norm_proj_kernel, n_rope=n_rope),
        grid=(rows // tm,),
        in_specs=[
            pl.BlockSpec((tm, d), lambda i: (i, 0)),
            pl.BlockSpec((1, d), lambda i: (0, 0)),
            pl.BlockSpec((None, d, n), lambda i: (layer, 0, 0)),
            pl.BlockSpec((tm, LANES), lambda i: (i % tab_blocks, 0)),
            pl.BlockSpec((tm, LANES), lambda i: (i % tab_blocks, 0)),
        ],
        out_specs=out_specs,
        out_shape=out_shape,
        compiler_params=_cparams("parallel"),
        name="norm_proj",
    )(x, g.reshape(1, d), w, cos, sin)
    return res[0] if seq is None else res


def _rope_tables(pos):
    half = HEAD_DIM // 2
    inv = ROPE_THETA ** (-jnp.arange(half, dtype=F32) / half)
    ang = pos.astype(F32)[:, None] * inv[None, :]
    cos = jnp.cos(ang)
    sin = jnp.sin(ang)
    reps = LANES // HEAD_DIM
    cos_t = jnp.concatenate([cos, cos] * reps, axis=1)
    sin_t = jnp.concatenate([-sin, sin] * reps, axis=1)
    return cos_t, sin_t


def _resid_proj_kernel(*refs, n_in, first_transposed):
    x_ref = refs[0]
    a_refs = refs[1:1 + n_in]
    w_refs = refs[1 + n_in:1 + 2 * n_in]
    g_ref, wq_ref, o_ref, q_ref = refs[1 + 2 * n_in:]
    acc = x_ref[...]
    for k, (a_ref, w_ref) in enumerate(zip(a_refs, w_refs)):
        a = a_ref[...].astype(BF16)
        acc = acc + (_dot_tn(a, w_ref[...]) if first_transposed and k == 0 else _dot(a, w_ref[...]))
    o_ref[...] = acc
    q_ref[...] = _dot(_rms(acc, g_ref[...]).astype(BF16), wq_ref[...]).astype(BF16)


def _resid_proj(x, acts, weights, g, wq, first_transposed=False):
    rows, d = x.shape
    tm = min(ROW_TILE, rows)
    n_in = len(acts)
    nq = wq.shape[1]
    row = lambda n: pl.BlockSpec((tm, n), lambda i: (i, 0))
    in_specs = [row(d)] + [row(a.shape[1]) for a in acts]
    if first_transposed:
        in_specs[1] = pl.BlockSpec((acts[0].shape[0], tm), lambda i: (0, i))
    in_specs += [pl.BlockSpec(w.shape, lambda i: (0, 0)) for w in weights]
    in_specs += [pl.BlockSpec((1, d), lambda i: (0, 0)), pl.BlockSpec(wq.shape, lambda i: (0, 0))]
    return pl.pallas_call(
        functools.partial(_resid_proj_kernel, n_in=n_in, first_transposed=first_transposed),
        grid=(rows // tm,),
        in_specs=in_specs,
        out_specs=[row(d), row(nq)],
        out_shape=[jax.ShapeDtypeStruct((rows, d), F32), jax.ShapeDtypeStruct((rows, nq), BF16)],
        compiler_params=_cparams("parallel"),
        name="resid_proj",
    )(x, *acts, *weights, g.reshape(1, d), wq)


def _topk_rank(gate, n_cand, n_valid, axis):
    idx = lax.broadcasted_iota(jnp.int32, gate.shape, axis)
    cnt = jnp.zeros(gate.shape, F32)
    for m in range(n_cand):
        gm = gate[m:m + 1, :] if axis == 0 else gate[:, m:m + 1]
        beats = jnp.where(gm > gate, 1.0, jnp.where(gm == gate, jnp.where(m < idx, 1.0, 0.0), 0.0))
        cnt = cnt + beats * jnp.where(m < n_valid, 1.0, 0.0)
    return cnt, idx


def _moba_prompt_kernel(qt_ref, k_ref, vt_ref, o_ref, kbf, vt, sbuf, pbuf, acc):
    i = pl.program_id(2)
    n_blk = k_ref.shape[0] // MOBA_BLOCK
    nq = 4 * MOBA_BLOCK

    @pl.when(i == 0)
    def _():
        kbf[...] = k_ref[...].astype(BF16)
        vt[...] = vt_ref[...].astype(BF16)

    qt = qt_ref[...]
    zero = jnp.zeros((HEAD_DIM, MOBA_BLOCK), F32)
    qst = jnp.concatenate(
        [jnp.concatenate([qt[j * HEAD_DIM:(j + 1) * HEAD_DIM, :], zero] if j < 2
                         else [zero, qt[j * HEAD_DIM:(j + 1) * HEAD_DIM, :]], axis=0) for j in range(4)],
        axis=1)

    qsc = (qst * (HEAD_DIM ** -0.5 * LOG2_E)).astype(BF16)
    causal = (lax.broadcasted_iota(jnp.int32, (MOBA_BLOCK, nq), 0)
              <= lax.broadcasted_iota(jnp.int32, (MOBA_BLOCK, nq), 1) % MOBA_BLOCK)
    bidx = lax.broadcasted_iota(jnp.int32, (n_blk, nq), 0)

    def attend(own):
        blocks = [slice(n * MOBA_BLOCK, (n + 1) * MOBA_BLOCK) for n in range(own + 1)]
        cmax = []
        gate = jnp.zeros((n_blk, nq), F32)
        for n, rows in enumerate(blocks):
            s = _dot(kbf[rows, :], qsc)
            if n == own:
                s = jnp.where(causal, s, NEG_INF)
            else:
                gate = jnp.where(bidx == n, jnp.sum(s, axis=0, keepdims=True), gate)
            sbuf[rows, :] = s
            cmax.append(jnp.max(s, axis=0, keepdims=True))
        if own > MOBA_TOPK:
            cnt, _ = _topk_rank(gate, own, own, 0)
            bias = jnp.where((bidx < own) & (cnt < MOBA_TOPK), 0.0, NEG_INF)
        else:
            bias = jnp.zeros((n_blk, nq), F32)
        m = cmax[own]
        for n in range(own):
            m = jnp.maximum(m, cmax[n] + bias[n:n + 1, :])
        l = jnp.zeros((1, nq), F32)
        out = jnp.zeros((LANES, nq), F32)
        for n, rows in enumerate(blocks):
            shift = m if n == own else m - bias[n:n + 1, :]
            p = jnp.exp2(sbuf[rows, :] - shift)
            l = l + jnp.sum(p, axis=0, keepdims=True)
            pbuf[rows, :] = p.astype(BF16)
            out = out + _dot(vt[:, rows], pbuf[rows, :])
        acc[...] = out / l

    for own in range(n_blk):
        pl.when(i == own)(functools.partial(attend, own))

    for j in range(4):
        r0 = 0 if j < 2 else HEAD_DIM
        o_ref[j * HEAD_DIM:(j + 1) * HEAD_DIM, :] = (
            acc[r0:r0 + HEAD_DIM, j * MOBA_BLOCK:(j + 1) * MOBA_BLOCK].astype(BF16))


def _moba_prompt(h, q_t, v_t, batch, seq):
    n_blk = seq // MOBA_BLOCK
    kcol = (CB_K * COL) // LANES
    return pl.pallas_call(
        _moba_prompt_kernel,
        grid=(batch, KV_HEADS // 2, n_blk),
        in_specs=[
            pl.BlockSpec((COL, MOBA_BLOCK), lambda b, p, i: (p, b * n_blk + i)),
            pl.BlockSpec((seq, LANES), lambda b, p, i: (b, kcol + p)),
            pl.BlockSpec((None, LANES, seq), lambda b, p, i: (b, p, 0)),
        ],
        out_specs=pl.BlockSpec((COL, MOBA_BLOCK), lambda b, p, i: (p, b * n_blk + i)),
        out_shape=jax.ShapeDtypeStruct((ATT_WIDTH, batch * seq), BF16),
        scratch_shapes=[
            pltpu.VMEM((seq, LANES), BF16),
            pltpu.VMEM((LANES, seq), BF16),
            pltpu.VMEM((seq, 4 * MOBA_BLOCK), F32),
            pltpu.VMEM((seq, 4 * MOBA_BLOCK), BF16),
            pltpu.VMEM((LANES, 4 * MOBA_BLOCK), F32),
        ],
        compiler_params=_cparams("parallel", "parallel", "arbitrary"),
        name="moba_prompt",
    )(q_t, h, v_t)


def _moba_sample_kernel(pt_ref, qbd_ref, kn_ref, vn_ref, *rest, n_pages, page):
    del pt_ref
    per_step = qbd_ref.shape[0]
    n_refs = per_step * n_pages
    o_ref = rest[2 * n_refs]
    kc_all, vc_all = rest[2 * n_refs + 1:]
    for g in range(per_step):
        _moba_sample_request(qbd_ref.at[g], kn_ref.at[g], vn_ref.at[g],
                             rest[g * n_pages:(g + 1) * n_pages],
                             rest[n_refs + g * n_pages:n_refs + (g + 1) * n_pages],
                             o_ref.at[g], kc_all.at[g], vc_all.at[g], page)


def _moba_sample_request(qbd_ref, kn_ref, vn_ref, kp, vp, o_ref, kc, vc, page):
    n_pages = len(kp)
    n_blk = n_pages * page // MOBA_BLOCK
    n_rows = qbd_ref.shape[0]
    seq_new = kn_ref.shape[0]

    for pg in range(n_pages):
        kc[:, pg * page:(pg + 1) * page] = kp[pg][...].astype(BF16)
        vc[:, pg * page:(pg + 1) * page] = vp[pg][...].astype(BF16)

    qbd = qbd_ref[...]
    qs = qbd * HEAD_DIM ** -0.5
    s_all = _dot(qs.astype(BF16), kc[...])

    lane = lax.broadcasted_iota(jnp.int32, (n_rows, LANES), 1)
    gate = jnp.zeros((n_rows, LANES), F32)
    for n in range(n_blk):
        gate = jnp.where(lane == n, jnp.mean(s_all[:, n * MOBA_BLOCK:(n + 1) * MOBA_BLOCK], axis=-1, keepdims=True),
                         gate)
    cnt, bidx = _topk_rank(gate, n_blk, n_blk, 1)
    bias = jnp.where((bidx < n_blk) & (cnt < MOBA_TOPK), 0.0, NEG_INF)
    s_blk = [s_all[:, n * MOBA_BLOCK:(n + 1) * MOBA_BLOCK] + bias[:, n:n + 1] for n in range(n_blk)]

    kn = kn_ref[...]
    vn = vn_ref[...]
    tok = lax.broadcasted_iota(jnp.int32, (n_rows, 1), 0) % seq_new
    s_own = []
    for t in range(seq_new):
        st = jnp.sum(qs * kn[t:t + 1, :], axis=-1, keepdims=True)
        s_own.append(jnp.where(t <= tok, st, NEG_INF))

    m = s_own[0]
    for st in s_own[1:]:
        m = jnp.maximum(m, st)
    for sb in s_blk:
        m = jnp.maximum(m, jnp.max(sb, axis=-1, keepdims=True))

    l = jnp.zeros((n_rows, 1), F32)
    out = jnp.zeros((n_rows, KV_WIDTH), F32)
    for t in range(seq_new):
        pt = jnp.exp(s_own[t] - m)
        l = l + pt
        out = out + pt * vn[t:t + 1, :]
    for n in range(n_blk):
        pn = jnp.exp(s_blk[n] - m)
        l = l + jnp.sum(pn, axis=-1, keepdims=True)
        out = out + _dot_nt(pn.astype(BF16), vc[:, n * MOBA_BLOCK:(n + 1) * MOBA_BLOCK])
    o_ref[...] = out / l


def _moba_sample(qbd, k_new, v_new, cache_kt, cache_vt, page_table, layer):
    n_req, n_pages = page_table.shape
    depth, n_pool, _, page = cache_kt.shape
    n_rows = qbd.shape[1]
    seq_new = k_new.shape[1]

    per_step = math.gcd(n_req, MOBA_REQ_PER_STEP)
    n_refs = per_step * n_pages
    cache_kt = cache_kt.reshape(depth * n_pool, KV_WIDTH, page)
    cache_vt = cache_vt.reshape(depth * n_pool, KV_WIDTH, page)
    page_table = page_table + layer * n_pool

    def page_spec(j):
        return pl.BlockSpec((None, KV_WIDTH, page), lambda r, pt: (pt[r * n_refs + j], 0, 0))

    in_specs = [
        pl.BlockSpec((per_step, n_rows, KV_WIDTH), lambda r, pt: (r, 0, 0)),
        pl.BlockSpec((per_step, seq_new, KV_WIDTH), lambda r, pt: (r, 0, 0)),
        pl.BlockSpec((per_step, seq_new, KV_WIDTH), lambda r, pt: (r, 0, 0)),
    ]
    in_specs += [page_spec(j) for j in range(n_refs)]
    in_specs += [page_spec(j) for j in range(n_refs)]
    grid_spec = pltpu.PrefetchScalarGridSpec(
        num_scalar_prefetch=1,
        grid=(n_req // per_step,),
        in_specs=in_specs,
        out_specs=pl.BlockSpec((per_step, n_rows, KV_WIDTH), lambda r, pt: (r, 0, 0)),
        scratch_shapes=[
            pltpu.VMEM((per_step, KV_WIDTH, n_pages * page), BF16),
            pltpu.VMEM((per_step, KV_WIDTH, n_pages * page), BF16),
        ],
    )
    return pl.pallas_call(
        functools.partial(_moba_sample_kernel, n_pages=n_pages, page=page),
        grid_spec=grid_spec,
        out_shape=jax.ShapeDtypeStruct((n_req, n_rows, KV_WIDTH), F32),
        compiler_params=_cparams("arbitrary"),
        name="moba_sample",
    )(page_table.reshape(-1), qbd, k_new, v_new, *([cache_kt] * n_refs), *([cache_vt] * n_refs))


def _hgrn_gates(hq, hf, lb):
    q = _silu(hq)
    f = lb + (1.0 - lb) * _sigmoid(hf)
    k = (1.0 - lb) * _sigmoid(-hf)
    return q, k, f


def _head_rms(o, norm_g):
    head_mean = jnp.where(_block_diag_mask(HG_WIDTH, HG_WIDTH, HG_DK), 1.0 / HG_DK, 0.0)
    ms = jnp.dot(o * o, head_mean, precision=HIGHEST, preferred_element_type=F32)
    return o * lax.rsqrt(ms + EPS) * norm_g


def _short_conv(ext_ref, rows, cw):
    return (cw[0:1, :] * ext_ref[pl.ds(HALO - 2, rows), :]
            + cw[1:2, :] * ext_ref[pl.ds(HALO - 1, rows), :]
            + cw[2:3, :] * ext_ref[pl.ds(HALO, rows), :])


def _hgrn_prompt_kernel(hq_ref, hf_ref, hi_ref, hg_ref, cb_ref, cc_ref, ch_ref, lb_ref, ng_ref, cw_ref,
                        oh_ref, oc_ref, st_ref, tail_ref, st, ext):
    tt = pl.program_id(1)
    rows = hq_ref.shape[0]
    n_chunks = rows // HG_CHUNK
    width = HG_WIDTH
    bd = _block_diag_mask(width, width, HG_DK)

    @pl.when(tt == 0)
    def _():
        ext[pl.ds(0, HALO), :] = jnp.zeros((HALO, CONV_CH), F32)

    @pl.when(tt > 0)
    def _():
        ext[pl.ds(0, HALO), :] = ext[pl.ds(rows, HALO), :]

    u = cc_ref[...] * ch_ref[...]
    ext[pl.ds(HALO, rows), :] = u
    oc_ref[...] = (cb_ref[...] * _short_conv(ext, rows, cw_ref[...])).astype(BF16)
    tail_ref[0] = u[rows - 8:rows, :]

    @pl.when(tt == 0)
    def _():
        st[...] = jnp.zeros(st.shape, F32)

    q, k, f = _hgrn_gates(hq_ref[...], hf_ref[...], lb_ref[...])
    logf = jnp.log(f)
    v = hi_ref[...]
    r_i =lax.broadcasted_iota(jnp.int32, (rows, rows), 0)
    c_i = lax.broadcasted_iota(jnp.int32, (rows, rows), 1)
    tri = jnp.where((r_i // HG_CHUNK == c_i // HG_CHUNK) & (c_i <= r_i), 1.0, 0.0)
    a_all = jnp.dot(tri, logf, precision=HIGHEST, preferred_element_type=F32)

    L = HG_CHUNK
    t_i = lax.broadcasted_iota(jnp.int32, (L, width), 0)
    tw = lax.broadcasted_iota(jnp.int32, (L, width), 0)
    sw_col = lax.broadcasted_iota(jnp.int32, (L, width), 1) % L
    head_ones = jnp.where(bd, 1.0, 0.0).astype(BF16)
    outs = []
    for c in range(n_chunks):
        sl = slice(c * L, (c + 1) * L)
        a, qc, kc, vc = a_all[sl], q[sl], k[sl], v[sl]
        a_end = a[L - 1:L, :]
        st_old = st[...]
        o = _dot_nt((qc * jnp.exp(a)).astype(BF16), st_old.astype(BF16))

        sw = jnp.zeros((L, width), F32)
        b = L // 2
        while b >= SUBLANES:
            nb = L // b
            rq = jnp.concatenate(
                [jnp.broadcast_to(a[j * b - 1:j * b, :], (b, width)) if j % 2 else a[j * b:(j + 1) * b, :]
                 for j in range(nb)], axis=0)
            rk = jnp.concatenate(
                [a[j * b:(j + 1) * b, :] if j % 2 else jnp.broadcast_to(a[(j + 1) * b - 1:(j + 1) * b, :], (b, width))
                 for j in range(nb)], axis=0)
            odd = (t_i // b) % 2 == 1
            ql = jnp.where(odd, qc * jnp.exp(a - rq), 0.0).astype(BF16)
            kl = jnp.where(odd, 0.0, kc * jnp.exp(rk - a))
            k_bd = jnp.concatenate([kl.astype(BF16)] * HG_HEADS, axis=0) * head_ones
            sc = _dot_nt(ql, k_bd)
            pair = ((tw // b) % 2 == 1) & (sw_col // b == tw // b - 1)
            sw = sw + jnp.where(pair, sc, 0.0)
            b //= 2
        a_g = a.reshape(L // SUBLANES, SUBLANES, width)
        r4 = jnp.broadcast_to(a_g[:, HG_LEAF - 1:HG_LEAF, :], a_g.shape).reshape(L, width)
        upper = (t_i % SUBLANES) >= HG_LEAF
        ql = jnp.where(upper, qc * jnp.exp(jnp.where(upper, a - r4, 0.0)), 0.0).astype(BF16)
        kl = jnp.where(upper, 0.0, kc * jnp.exp(jnp.where(upper, 0.0, r4 - a)))
        k_bd = jnp.concatenate([kl.astype(BF16)] * HG_HEADS, axis=0) * head_ones
        pair = ((tw % SUBLANES) >= HG_LEAF) & (sw_col // SUBLANES == tw // SUBLANES) & (sw_col % SUBLANES < HG_LEAF)
        sw = sw + jnp.where(pair, _dot_nt(ql, k_bd), 0.0)
        v_bd =jnp.concatenate([vc.astype(BF16)] * HG_HEADS, axis=0) * head_ones
        o = o + _dot(sw.astype(BF16), v_bd)

        prods = []
        vrs = []
        for delta in range(HG_LEAF):
            if delta:
                kr, ar, vr = (pltpu.roll(x.reshape(L // SUBLANES, SUBLANES, width), delta, 1).reshape(L, width)
                              for x in (kc, a, vc))
            else:
                kr, ar, vr = kc, a, vc
            live = (t_i % HG_LEAF) >= delta
            prods.append(jnp.where(live, qc * kr * jnp.exp(jnp.where(live, a - ar, 0.0)), 0.0))
            vrs.append(vr)
        sc = _dot(jnp.concatenate(prods, axis=0).astype(BF16), head_ones)
        for delta in range(HG_LEAF):
            o = o + sc[delta * L:(delta + 1) * L] * vrs[delta]
        outs.append(o)

        ke = (kc * jnp.exp(a_end - a)).astype(BF16)
        upd = _dot_tn(vc.astype(BF16), ke)
        st[...] = st_old * jnp.exp(a_end) + jnp.where(bd, upd, 0.0)

    o_all = jnp.concatenate(outs, axis=0)
    oh_ref[...] = (_head_rms(o_all, ng_ref[...]) * _silu(hg_ref[...])).astype(BF16)
    st_ref[0] = st[...]


def _hgrn_prompt(h, lb, norm_g, cw, batch, seq):
    rows = 256
    nt = seq // rows

    def col(cb):
        return pl.BlockSpec((rows, COL), lambda b, t: (b * nt + t, cb))

    const = lambda shape: pl.BlockSpec(shape, lambda b, t: (0, 0))
    row_out = pl.BlockSpec((rows, COL), lambda b, t: (b * nt + t, 0))
    return pl.pallas_call(
        _hgrn_prompt_kernel,
        grid=(batch, nt),
        in_specs=[col(CB_HQ), col(CB_HF), col(CB_HI), col(CB_HG), col(CB_CB), col(CB_CC), col(CB_CH),
                  const((1, HG_WIDTH)), const((1, HG_WIDTH)), const((3, CONV_CH))],
        out_specs=[row_out, row_out,
                   pl.BlockSpec((1, HG_WIDTH, HG_WIDTH), lambda b, t: (b, 0, 0)),
                   pl.BlockSpec((1, 8, CONV_CH), lambda b, t: (b, 0, 0))],
        out_shape=[jax.ShapeDtypeStruct((batch * seq, HG_WIDTH), BF16),
                   jax.ShapeDtypeStruct((batch * seq, CONV_CH), BF16),
                   jax.ShapeDtypeStruct((batch, HG_WIDTH, HG_WIDTH), F32),
                   jax.ShapeDtypeStruct((batch, 8, CONV_CH), F32)],
        scratch_shapes=[pltpu.VMEM((HG_WIDTH, HG_WIDTH), F32),
                        pltpu.VMEM((HALO + rows + HALO, CONV_CH), F32)],
        compiler_params=_cparams("parallel", "arbitrary"),
        name="hgrn_prompt",
    )(h, h, h, h, h, h, h, lb.reshape(1, -1), norm_g.reshape(1, -1), cw)


def _mix_sample_kernel(hq_ref, hf_ref, hi_ref, hg_ref, cb_ref, cc_ref, ch_ref, s0_ref, cprev_ref,
                       lb_ref, ng_ref, cw_ref, oh_ref, oc_ref, s_ref, cnew_ref, fg, kg, qg):
    seq = hq_ref.shape[0]

    full = [cprev_ref[0], cprev_ref[1]] + [cc_ref[t] * ch_ref[t] for t in range(seq)]
    for t in range(seq):
        oc_ref[t] = cb_ref[t] * (cw_ref[0] * full[t] + cw_ref[1] * full[t + 1] + cw_ref[2] * full[t + 2])
    cnew_ref[0] = full[seq]
    cnew_ref[1] = full[seq + 1]

    lb = lb_ref[...]
    for t in range(seq):
        q, k, f = _hgrn_gates(hq_ref[t], hf_ref[t], lb)
        fg[t], kg[t], qg[t] = f, k, q
    v = [hi_ref[t] for t in range(seq)]

    def body(dk, o):
        s = s0_ref[dk]
        row = pl.ds(dk, 1)
        new_o = []
        for t in range(seq):
            s = fg[t, row, :] * s + kg[t, row, :] * v[t]
            new_o.append(o[t] + qg[t, row, :] * s)
        s_ref[dk] = s
        return tuple(new_o)

    o = lax.fori_loop(0, HG_DK, body, tuple(jnp.zeros(v[0].shape, F32) for _ in range(seq)))
    for t in range(seq):
        ms = jnp.mean(o[t] * o[t], axis=0, keepdims=True)
        oh_ref[t] = o[t] * lax.rsqrt(ms + EPS) * ng_ref[...] * _silu(hg_ref[t])


def _mix_sample(act_t, state_t, layer, cprev_t, lb, norm_g, cw):
    seq, _, n_req = act_t.shape
    hd = HG_DK

    def act(k):
        return pl.BlockSpec((seq, hd, n_req), lambda h: (0, k * HG_HEADS + h, 0))

    per_head = lambda lead: pl.BlockSpec((lead, hd, n_req), lambda h: (0, h, 0))
    chan = pl.BlockSpec((None, hd, n_req), lambda h: (layer, h, 0))
    taps = pl.BlockSpec((None, 3, hd, n_req), lambda h: (layer, 0, h, 0))
    state_in = pl.BlockSpec((None, None, hd, hd, n_req), lambda h: (layer, h, 0, 0, 0))
    state_out = pl.BlockSpec((None, hd, hd, n_req), lambda h: (h, 0, 0, 0))
    return pl.pallas_call(
        _mix_sample_kernel,
        grid=(HG_HEADS,),
        in_specs=[act(k) for k in range(7)] + [state_in, per_head(2), chan, chan, taps],
        out_specs=[per_head(seq), per_head(seq), state_out, per_head(2)],
        out_shape=[jax.ShapeDtypeStruct((seq, HG_WIDTH, n_req), F32),
                   jax.ShapeDtypeStruct((seq, CONV_CH, n_req), F32),
                   jax.ShapeDtypeStruct((HG_HEADS, hd, hd, n_req), F32),
                   jax.ShapeDtypeStruct((2, CONV_CH, n_req), F32)],
        scratch_shapes=[pltpu.VMEM((seq, hd, n_req), F32)] * 3,
        compiler_params=_cparams("parallel"),
        name="mix_sample",
    )(*([act_t] * 7), state_t, cprev_t, lb, norm_g, cw)


def _xattn_prompt_kernel(q_ref, mk_ref, mv_ref, o_ref):
    q = q_ref[...].astype(F32) * (XA_WIDTH // XA_HEADS) ** -0.5
    mk = mk_ref[...].astype(BF16)
    mv = mv_ref[...].astype(BF16)
    head = lax.broadcasted_iota(jnp.int32, q.shape, 1) // (XA_WIDTH // XA_HEADS)
    out = jnp.zeros(q.shape, F32)
    for hh in range(XA_HEADS):
        mine = head == hh
        s = _dot_nt(jnp.where(mine, q, 0.0).astype(BF16), mk)
        m = jnp.max(s, axis=-1, keepdims=True)
        p = jnp.exp(s - m)
        l = jnp.sum(p, axis=-1, keepdims=True)
        out = jnp.where(mine, _dot(p.astype(BF16), mv) / l, out)
    o_ref[...] = out.astype(BF16)


def _xattn_prompt(q, mkv, batch, seq):
    tm = min(ROW_TILE, seq)
    nt = seq // tm
    return pl.pallas_call(
        _xattn_prompt_kernel,
        grid=(batch, nt),
        in_specs=[pl.BlockSpec((tm, XA_WIDTH), lambda b, t: (b * nt + t, 0)),
                  pl.BlockSpec((N_MEM, XA_WIDTH), lambda b, t: (b, 0)),
                  pl.BlockSpec((N_MEM, XA_WIDTH), lambda b, t: (b, 1))],
        out_specs=pl.BlockSpec((tm, XA_WIDTH), lambda b, t: (b * nt + t, 0)),
        out_shape=jax.ShapeDtypeStruct((batch * seq, XA_WIDTH), BF16),
        compiler_params=_cparams("parallel", "parallel"),
        name="xattn_prompt",
    )(q, mkv, mkv)


def _xattn_sample_kernel(qbd_ref, mk_ref, mv_ref, o_ref):
    for r in range(qbd_ref.shape[0]):
        q = (qbd_ref[r] * (XA_WIDTH // XA_HEADS) ** -0.5).astype(BF16)
        s = _dot(q, mk_ref[r].astype(BF16))
        m = jnp.max(s, axis=-1, keepdims=True)
        p = jnp.exp(s - m)
        l = jnp.sum(p, axis=-1, keepdims=True)
        o_ref[r] = _dot_nt(p.astype(BF16), mv_ref[r].astype(BF16)) / l


def _xattn_sample(qbd, mem_kt, mem_vt, layer):
    n_req, n_rows, _ = qbd.shape
    per_step = math.gcd(n_req, XATTN_REQ_PER_STEP)
    mem_spec = pl.BlockSpec((None, per_step, XA_WIDTH, N_MEM), lambda r: (layer, r, 0, 0))
    return pl.pallas_call(
        _xattn_sample_kernel,
        grid=(n_req // per_step,),
        in_specs=[pl.BlockSpec((per_step, n_rows, XA_WIDTH), lambda r: (r, 0, 0)), mem_spec, mem_spec],
        out_specs=pl.BlockSpec((per_step, n_rows, XA_WIDTH), lambda r: (r, 0, 0)),
        out_shape=jax.ShapeDtypeStruct((n_req, n_rows, XA_WIDTH), F32),
        compiler_params=_cparams("parallel"),
        name="xattn_sample",
    )(qbd, mem_kt, mem_vt)


def _ffn_prompt_kernel(x_ref, halo_ref, ox_ref, ox_halo_ref, wxo_ref, g_ref, wup_ref, cw_ref, wdn_ref, gf_ref,
                       o_ref, u_ref, xn_sc, ext_a, ext_b, hmid, *, seq, final_norm):
    rows = x_ref.shape[0]
    wxo = wxo_ref[...]
    x = x_ref[...] + _dot(ox_ref[...].astype(BF16), wxo)
    o_ref[...] = x
    g = g_ref[...]
    x_halo = halo_ref[...] + _dot(ox_halo_ref[...].astype(BF16), wxo)
    first = (pl.program_id(0) * rows) % seq == 0
    xn_sc[pl.ds(0, HALO), :] = jnp.where(first, 0.0, _rms(x_halo, g)).astype(BF16)
    xn_sc[pl.ds(HALO, rows), :] = _rms(x, g).astype(BF16)
    xn = xn_sc[...]


    n_chunks = D_FF // COL

    def up(j):
        for half, ext in ((0, ext_a), (1, ext_b)):
            c0 = half * D_FF + j * COL
            u = _dot(xn, wup_ref[:, c0:c0 + COL])
            u_ref[0, :, c0:c0 + COL] = u[HALO + rows - 8:, :]
            ext[j % 2] = u

    def conv(ext, j, c0):
        cw = cw_ref[:, c0:c0 + COL]
        s1 = ext[j % 2, pl.ds(HALO - 1, rows), :]
        s2 = ext[j % 2, pl.ds(HALO - 2, rows), :]
        return cw[0:1, :] * s2 + cw[1:2, :] * s1 + cw[2:3, :] * ext[j % 2, pl.ds(HALO, rows), :]

    def gate(j):
        hmid[j % 2] = (_silu(conv(ext_a, j, j * COL)) * conv(ext_b, j, D_FF + j * COL)).astype(BF16)

    def down(j):
        o_ref[...] += _dot(hmid[j % 2], wdn_ref[j * COL:(j + 1) * COL, :])

    up(0)
    for j in range(n_chunks):
        if j + 1 < n_chunks:
            up(j + 1)
        gate(j)
        if j >= 1:
            down(j - 1)
    down(n_chunks - 1)
    if final_norm:
        o_ref[...] = _rms(o_ref[...], gf_ref[...])


def _ffn_sample_kernel(x_ref, ox_ref, wxo_ref, p0_ref, p1_ref, g_ref, wup_ref, cw_ref, wdn_ref, gf_ref,
                       o_ref, u2_ref, u3_ref, acc, *, final_norm):
    seq, n_req, d = x_ref.shape
    ox = ox_ref[...].reshape(seq * n_req, ox_ref.shape[2])
    x = x_ref[...].reshape(seq * n_req, d) + _dot(ox.astype(BF16), wxo_ref[...])
    xn = _rms(x, g_ref[...]).astype(BF16)
    acc[...] = jnp.zeros(acc.shape, F32)

    def conv(u, c0):
        cw = cw_ref[:, pl.ds(c0, COL)]
        full = [p0_ref[:, pl.ds(c0, COL)], p1_ref[:, pl.ds(c0, COL)]]
        full += [u[t * n_req:(t + 1) * n_req, :] for t in range(seq)]
        u2_ref[:, pl.ds(c0, COL)] = full[seq]
        u3_ref[:, pl.ds(c0, COL)] = full[seq + 1]
        return jnp.concatenate(
            [cw[0:1, :] * full[t] + cw[1:2, :] * full[t + 1] + cw[2:3, :] * full[t + 2] for t in range(seq)],
            axis=0)

    def body(j, carry):
        ca = pl.multiple_of(j * COL, COL)
        cb = pl.multiple_of(D_FF + j * COL, COL)
        ya = conv(_dot(xn, wup_ref[:, pl.ds(ca, COL)]), ca)
        yb = conv(_dot(xn, wup_ref[:, pl.ds(cb, COL)]), cb)
        acc[...] += _dot((_silu(ya) * yb).astype(BF16), wdn_ref[pl.ds(ca, COL), :])
        return carry

    lax.fori_loop(0, D_FF // COL, body, 0)
    y = x + acc[...]
    if final_norm:
        y = _rms(y, gf_ref[...])
    o_ref[...] = y.reshape(seq, n_req, d)


def _resident(shape, layer=None):
    if layer is None:
        return pl.BlockSpec(shape, lambda i: (0,) * len(shape), pipeline_mode=pl.Buffered(1))
    return pl.BlockSpec((None,) + shape, lambda i: (layer,) + (0,) * len(shape), pipeline_mode=pl.Buffered(1))


def _ffn_prompt(x, ox, w_xo, g, w_up, cw, w_dn, g_final, seq, final_norm, layer):
    rows, d = x.shape
    tm = min(FFN_ROW_TILE, seq)
    n_up = w_up.shape[2]
    nx = ox.shape[1]
    tiles_per_seq = seq // tm
    x_spec = pl.BlockSpec((tm, d), lambda i: (i, 0))
    halo = lambda n: pl.BlockSpec((HALO, n), lambda i: (jnp.maximum(i * (tm // HALO) - 1, 0), 0))
    return pl.pallas_call(
        functools.partial(_ffn_prompt_kernel, seq=seq, final_norm=final_norm),
        grid=(rows // tm,),
        in_specs=[x_spec, halo(d), pl.BlockSpec((tm, nx), lambda i: (i, 0)), halo(nx), _resident((nx, d)),
                  _resident((1, d)), _resident((d, n_up), layer), _resident((3, n_up)),
                  _resident((D_FF, d), layer), _resident((1, d))],
        out_specs=[x_spec, pl.BlockSpec((1, 8, n_up), lambda i: (i // tiles_per_seq, 0, 0))],
        out_shape=[jax.ShapeDtypeStruct((rows, d), F32),
                   jax.ShapeDtypeStruct((rows // seq, 8, n_up), F32)],
        scratch_shapes=[pltpu.VMEM((HALO + tm, d), BF16),
                        pltpu.VMEM((2, HALO + tm, COL), F32),
                        pltpu.VMEM((2, HALO + tm, COL), F32),
                        pltpu.VMEM((2, tm, COL), BF16)],
        compiler_params=_cparams("arbitrary"),
        name="ffn_prompt",
    )(x, x, ox, ox, w_xo, g.reshape(1, d), w_up, cw, w_dn, g_final.reshape(1, d))


def _ffn_sample(x, ox, w_xo, prev, g, w_up, cw, w_dn, g_final, final_norm, layer):
    seq, n_req, d = x.shape
    n_up = w_up.shape[2]
    nx = ox.shape[2]
    full = lambda shape: pl.BlockSpec(shape, lambda i: (0,) * len(shape))
    return pl.pallas_call(
        functools.partial(_ffn_sample_kernel, final_norm=final_norm),
        grid=(1,),
        in_specs=[_resident((seq, n_req, d)), _resident((seq, n_req, nx)), _resident((nx, d)),
                  _resident((n_req, n_up)), _resident((n_req, n_up)),
                  _resident((1, d)), _resident((d, n_up), layer), _resident((3, n_up)),
                  _resident((D_FF, d), layer), _resident((1, d))],
        out_specs=[full((seq, n_req, d)), full((n_req, n_up)), full((n_req, n_up))],
        out_shape=[jax.ShapeDtypeStruct((seq, n_req, d), F32),
                   jax.ShapeDtypeStruct((n_req, n_up), F32),
                   jax.ShapeDtypeStruct((n_req, n_up), F32)],
        scratch_shapes=[pltpu.VMEM((seq * n_req, d), F32)],
        compiler_params=_cparams("arbitrary"),
        name="ffn_sample",
    )(x, ox, w_xo, prev[:, 0], prev[:, 1], g.reshape(1, d), w_up, cw, w_dn, g_final.reshape(1, d))


def _head_rows(q, n_heads, group):
    n_req, seq, _ = q.shape
    n_kv = n_heads // group
    qh = q.reshape(n_req, seq, n_heads, HEAD_DIM).transpose(0, 2, 1, 3)
    onehot = jax.nn.one_hot(jnp.arange(n_heads) // group, n_kv, dtype=q.dtype)
    out = qh[:, :, :, None, :] * onehot[None, :, None, :, None]
    return out.reshape(n_req, n_heads * seq, n_kv * HEAD_DIM)


def _head_rows_inverse(o, n_heads, group, seq):
    n_req = o.shape[0]
    n_kv = n_heads // group
    o5 = o.reshape(n_req, n_heads, seq, n_kv, HEAD_DIM)
    onehot = jax.nn.one_hot(jnp.arange(n_heads) // group, n_kv, dtype=o.dtype)
    picked = (o5 * onehot[None, :, None, :, None]).sum(axis=3)
    return picked.transpose(0, 2, 1, 3).reshape(n_req * seq, n_heads * HEAD_DIM)


def kernel(x_prompt, x_sample, cache_k, cache_v, cache_mem_k, cache_mem_v, state_hgrn, state_conv,
           state_ffn, page_table, mem_prompt, g_mix, w_in, w_out, hg_lb_logits, hg_norm_g, conv_w,
           g_xattn, g_mem, w_xq, w_mk, w_mv, w_xo, g_ffn, w_up, ffn_conv_w, w_down, g_final):
    B, T, D = x_prompt.shape
    DB, S, _ = x_sample.shape
    depth = w_in.shape[0]
    n_pages = page_table.shape[1]
    page = cache_k.shape[2]
    past = n_pages * page

    lb_p = jax.nn.softmax(hg_lb_logits.astype(F32), axis=0)
    lower = jnp.cumsum(lb_p, axis=0) - lb_p[0:1]

    rope_p = _rope_tables(jnp.arange(T, dtype=jnp.int32))
    rope_s = _rope_tables(jnp.tile(past + jnp.arange(S, dtype=jnp.int32), DB))

    w_in_b, w_out_b, w_xq_b, w_xo_b = (w.astype(BF16) for w in (w_in, w_out, w_xq, w_xo))
    w_up_b, w_down_b = w_up.astype(BF16), w_down.astype(BF16)
    w_mkv_b = jnp.concatenate([w_mk, w_mv], axis=-1).astype(BF16)

    def token_minor(c):
        return c.transpose(0, 1, 3, 4, 2).reshape(c.shape[0], c.shape[1], c.shape[3] * c.shape[4], c.shape[2])

    cache_k4, cache_v4 = token_minor(cache_k), token_minor(cache_v)
    mem_k4, mem_v4 = token_minor(cache_mem_k), token_minor(cache_mem_v)
    state_t = state_hgrn.transpose(0, 2, 3, 4, 1)
    spread = lambda a: jnp.broadcast_to(a[..., None], a.shape + (DB,))
    lower_s, norm_g_s, conv_w_s = spread(lower), spread(hg_norm_g), spread(conv_w)
    mem2 = mem_prompt.reshape(B * N_MEM, D)

    xp = x_prompt.reshape(B * T, D)
    xs = x_sample.reshape(DB * S, D)
    outs = [[] for _ in range(12)]
    for l in range(depth):
        last = l == depth - 1
        h, q_t, k1, v1 = _norm_proj(xp, g_mix[l], w_in_b, l, rope=rope_p,
                                    n_rope=(ATT_WIDTH + KV_WIDTH) // COL, seq=T)
        a = _moba_prompt(h, q_t, v1, B, T)
        o_h, o_c, st_t, c_tail = _hgrn_prompt(h, lower[l], hg_norm_g[l], conv_w[l], B, T)
        st5 = st_t.reshape(B, HG_HEADS, HG_DK, HG_HEADS, HG_DK)
        h1 = jnp.stack([st5[:, hh, :, hh, :] for hh in range(HG_HEADS)], axis=1).swapaxes(-1, -2)
        c1 = c_tail[:, 6:8, :]
        w_out_parts = [w_out_b[l, :ATT_WIDTH], w_out_b[l, ATT_WIDTH:ATT_WIDTH + HG_WIDTH],
                       w_out_b[l, ATT_WIDTH + HG_WIDTH:]]
        xp, qx = _resid_proj(xp, [a, o_h, o_c], w_out_parts, g_xattn[l], w_xq_b[l], first_transposed=True)
        mkv = _norm_proj(mem2, g_mem[l], w_mkv_b, l)
        mk = mkv[:, :XA_WIDTH].reshape(B, N_MEM, XA_HEADS, XA_WIDTH // XA_HEADS)
        mv = mkv[:, XA_WIDTH:].reshape(B, N_MEM, XA_HEADS, XA_WIDTH // XA_HEADS)
        ox = _xattn_prompt(qx, mkv, B, T)
        xp, u_tail = _ffn_prompt(xp, ox, w_xo_b[l], g_ffn[l], w_up_b, ffn_conv_w[l], w_down_b,
                                 g_final, T, last, l)
        fs1 = u_tail[:, 6:8, :]

        hs = _norm_proj(xs, g_mix[l], w_in_b, l, rope=rope_s, n_rope=(ATT_WIDTH + KV_WIDTH) // COL)
        hs3 = hs.reshape(DB, S, IN_WIDTH)
        k_new = hs3[:, :, CB_K * COL:(CB_K + 1) * COL]
        v_new = hs3[:, :, CB_V * COL:(CB_V + 1) * COL]
        qbd = _head_rows(hs3[:, :, :ATT_WIDTH], ATT_HEADS, ATT_HEADS // KV_HEADS)
        o_att = _moba_sample(qbd, k_new, v_new, cache_k4, cache_v4, page_table, l)
        a_s = _head_rows_inverse(o_att, ATT_HEADS, ATT_HEADS // KV_HEADS, S)
        act_t = hs3[:, :, CB_HQ * COL:].transpose(1, 2, 0)
        oh_t, oc_t, h2, c2_t = _mix_sample(act_t, state_t, l, state_conv[l].transpose(1, 2, 0),
                                           lower_s, norm_g_s, conv_w_s)
        oh_s, oc_s, c2 = (a.transpose(2, 0, 1) for a in (oh_t, oc_t, c2_t))
        xs, qxs = _resid_proj(xs, [a_s, oh_s.reshape(DB * S, HG_WIDTH), oc_s.reshape(DB * S, CONV_CH)],
                              w_out_parts, g_xattn[l], w_xq_b[l])
        qxbd = _head_rows(qxs.reshape(DB, S, XA_WIDTH), XA_HEADS, 1)
        oxs = _head_rows_inverse(_xattn_sample(qxbd, mem_k4, mem_v4, l), XA_HEADS, 1, S)
        xs_t, u_a, u_b = _ffn_sample(xs.reshape(DB, S, D).swapaxes(0, 1),
                                     oxs.reshape(DB, S, XA_WIDTH).swapaxes(0, 1), w_xo_b[l], state_ffn[l],
                                     g_ffn[l], w_up_b, ffn_conv_w[l], w_down_b, g_final, last, l)
        xs = xs_t.swapaxes(0, 1).reshape(DB * S, D)
        fs2 = jnp.stack([u_a, u_b], axis=1)

        for lst, val in zip(outs, (k1, v1, k_new.reshape(DB, S, KV_HEADS, HEAD_DIM),
                                   v_new.reshape(DB, S, KV_HEADS, HEAD_DIM), h1, h2, c1, c2, fs1, fs2, mk, mv)):
            lst.append(val)

    res = [jnp.stack(o) for o in outs]
    for j in (0, 1):
        res[j] = res[j].reshape(depth, B, KV_HEADS, HEAD_DIM, T).transpose(0, 1, 4, 2, 3)
    res[5] = res[5].transpose(0, 4, 1, 2, 3)
    return (xp.reshape(B, T, D), xs.reshape(DB, S, D)) + tuple(res)
```
